```python
import math
import jax, jax.numpy as jnp
from jax import lax
import numpy as np

D_MODEL = 1024
BATCH = 8
SEQ = 2048
DEPTH = 1

CHUNK = 64
PLE_DIM = 256
SG_BLOCK = 128
SG_GROUPS = 4
SG_WIDTH = 512
SG_GROUP_CH = SG_WIDTH // SG_GROUPS
SB_HEADS = 8
SB_HEAD_DIM = 64
SB_WIDTH = SB_HEADS * SB_HEAD_DIM
SB_QBLOCK = 128
IN_SIZES = (SG_WIDTH, SG_WIDTH, SB_WIDTH, SB_WIDTH, SB_WIDTH, D_MODEL, D_MODEL)
IN_PROJ_WIDTH = 2 * SG_WIDTH + 3 * SB_WIDTH + 2 * D_MODEL
IN_SPLITS = (SG_WIDTH, 2 * SG_WIDTH, 2 * SG_WIDTH + SB_WIDTH, 2 * SG_WIDTH + 2 * SB_WIDTH,
             2 * SG_WIDTH + 3 * SB_WIDTH, 2 * SG_WIDTH + 3 * SB_WIDTH + D_MODEL)
N_EXPERTS = 64
N_GROUPS = 8
TOPK_GROUPS = 4
TOP_K = 8
EXPERT_FF = 256
SHARED_FF = 256
ROUTED_SCALE = 2.5
MOE_BLOCK = 128
LN_EPS = 1e-5
DEEPNORM_ALPHA = (2.0 * DEPTH) ** 0.25
DEEPNORM_BETA = (8.0 * DEPTH) ** -0.25

kernel_name = "hybrid_sgmlp_stickbreak_moe_deepnorm"


def layer_norm(x, g, b):
    xf = x.astype(jnp.float32)
    mu = jnp.mean(xf, axis=-1, keepdims=True)
    xc = xf - mu
    var = jnp.mean(jnp.square(xc), axis=-1, keepdims=True)
    y = xc * lax.rsqrt(var + LN_EPS)
    return (y * g.astype(jnp.float32) + b.astype(jnp.float32)).astype(x.dtype)


def spatial_gating(u, v, ln_g, ln_b, w_s, b_s):
    bsz, seq, _ = u.shape
    nb = seq // SG_BLOCK
    v = layer_norm(v, ln_g, ln_b).reshape(bsz, nb, SG_BLOCK, SG_GROUPS, SG_GROUP_CH)
    pos = jnp.arange(SG_BLOCK)
    mask = (pos[None, :] // CHUNK) <= (pos[:, None] // CHUNK)
    w = jnp.where(mask[None], w_s, jnp.zeros_like(w_s))
    s = jnp.einsum('gij,bnjgc->bnigc', w, v) + b_s.T[None, None, :, :, None]
    return u * s.reshape(bsz, seq, SG_WIDTH)


def stick_breaking_attention(q, k, v):
    bsz, seq, _, _ = q.shape
    q = q.transpose(0, 2, 1, 3)
    k = k.transpose(0, 2, 1, 3)
    v = v.transpose(0, 2, 1, 3)
    scale = 1.0 / math.sqrt(SB_HEAD_DIM)
    outs = []
    for blk in range(seq // SB_QBLOCK):
        lo = blk * SB_QBLOCK
        hi = lo + SB_QBLOCK
        z = jnp.einsum('bhqd,bhkd->bhqk', q[:, :, lo:hi], k[:, :, :hi]).astype(jnp.float32) * scale
        t_idx = lo + jnp.arange(SB_QBLOCK)[:, None]
        s_idx = jnp.arange(hi)[None, :]
        causal = s_idx < t_idx
        log_1m = jnp.where(causal, jax.nn.log_sigmoid(-z), 0.0)
        tail = lax.cumsum(log_1m, axis=3, reverse=True) - log_1m
        a = jnp.where(causal, jnp.exp(jax.nn.log_sigmoid(z) + tail), 0.0)
        outs.append(jnp.einsum('bhqk,bhkd->bhqd', a.astype(v.dtype), v[:, :, :hi]))
    o = jnp.concatenate(outs, axis=2)
    return o.transpose(0, 2, 1, 3).reshape(bsz, seq, SB_WIDTH)


def moe_ffn(x, router_w, router_bias, w_gate, w_up, w_down, sh_gate, sh_up, sh_down):
    bsz, seq, dm = x.shape
    n_tok = bsz * seq
    xf = x.reshape(n_tok, dm)
    scores = jax.nn.sigmoid((xf @ router_w).astype(jnp.float32))
    sel = scores + router_bias.astype(jnp.float32)
    grp = jnp.sum(lax.top_k(sel.reshape(n_tok, N_GROUPS, N_EXPERTS // N_GROUPS), 2)[0], axis=-1)
    _, gidx = lax.top_k(grp, TOPK_GROUPS)
    gmask = jnp.any(gidx[:, :, None] == jnp.arange(N_GROUPS)[None, None, :], axis=1)
    emask = jnp.repeat(gmask, N_EXPERTS // N_GROUPS, axis=1)
    _, eidx = lax.top_k(jnp.where(emask, sel, -jnp.inf), TOP_K)
    wsel = jnp.take_along_axis(scores, eidx, axis=1)
    wsel = wsel / jnp.sum(wsel, axis=-1, keepdims=True) * ROUTED_SCALE
    n_assign = n_tok * TOP_K
    flat_e = eidx.reshape(n_assign).astype(jnp.int32)
    flat_tok = jnp.repeat(jnp.arange(n_tok, dtype=jnp.int32), TOP_K)
    flat_w = wsel.reshape(n_assign)
    order = jnp.argsort(flat_e)
    sorted_e = flat_e[order]
    counts = jnp.bincount(flat_e, length=N_EXPERTS).astype(jnp.int32)
    starts = jnp.cumsum(counts) - counts
    padded = (counts + MOE_BLOCK - 1) // MOE_BLOCK * MOE_BLOCK
    pad_ends = jnp.cumsum(padded)
    pad_starts = pad_ends - padded
    dest = pad_starts[sorted_e] + jnp.arange(n_assign, dtype=jnp.int32) - starts[sorted_e]
    n_blocks = n_assign // MOE_BLOCK + N_EXPERTS
    n_rows = n_blocks * MOE_BLOCK
    row_tok = jnp.full((n_rows,), n_tok, jnp.int32).at[dest].set(flat_tok[order])
    row_w = jnp.zeros((n_rows,), jnp.float32).at[dest].set(flat_w[order])
    block_e = jnp.minimum(jnp.searchsorted(pad_ends, jnp.arange(n_blocks, dtype=jnp.int32) * MOE_BLOCK,
                                           side='right'), N_EXPERTS - 1)
    x_pad = jnp.concatenate([xf, jnp.zeros((1, dm), xf.dtype)], axis=0)

    def expert_block(args):
        tok, e = args
        xb = x_pad[tok]
        hb = jax.nn.silu(xb @ w_gate[e]) * (xb @ w_up[e])
        return hb @ w_down[e]

    yb = lax.map(expert_block, (row_tok.reshape(n_blocks, MOE_BLOCK), block_e))
    yb = yb.reshape(n_rows, dm) * row_w[:, None].astype(yb.dtype)
    routed = jax.ops.segment_sum(yb, row_tok, num_segments=n_tok + 1)[:n_tok]
    shared = (jax.nn.silu(xf @ sh_gate) * (xf @ sh_up)) @ sh_down
    return (routed + shared).reshape(bsz, seq, dm)


def setup_inputs(seed: int = 0) -> dict:
    key = jax.random.key(seed)
    ks = jax.random.split(key, 26)
    L = DEPTH
    D = D_MODEL

    def nrm(k, shape, scale):
        return jax.random.normal(k, shape, jnp.float32) * scale

    return {
        "x": nrm(ks[0], (BATCH, SEQ, D), 1.0),
        "p": nrm(ks[1], (DEPTH, BATCH, SEQ, PLE_DIM), 1.0),
        "ln_in_g": 1.0 + nrm(ks[2], (D,), 0.02),
        "ln_in_b": nrm(ks[3], (D,), 0.02),
        "w_in": nrm(ks[4], (L, D, IN_PROJ_WIDTH), D ** -0.5),
        "sg_ln_g": 1.0 + nrm(ks[5], (L, SG_WIDTH), 0.02),
        "sg_ln_b": nrm(ks[6], (L, SG_WIDTH), 0.02),
        "sg_w_s": nrm(ks[7], (L, SG_GROUPS, SG_BLOCK, SG_BLOCK), SG_BLOCK ** -0.5),
        "sg_b_s": 1.0 + nrm(ks[8], (L, SG_GROUPS, SG_BLOCK), 0.1),
        "w_branch_a": nrm(ks[9], (L, SG_WIDTH, D), SG_WIDTH ** -0.5),
        "w_branch_b": nrm(ks[10], (L, SB_WIDTH, D), SB_WIDTH ** -0.5),
        "w_out": nrm(ks[11], (L, D, D), D ** -0.5 * DEEPNORM_BETA),
        "ln1_g": 1.0 + nrm(ks[12], (L, D), 0.02),
        "ln1_b": nrm(ks[13], (L, D), 0.02),
        "router_w": nrm(ks[14], (L, D, N_EXPERTS), D ** -0.5),
        "router_bias": nrm(ks[15], (L, N_EXPERTS), 0.01),
        "moe_w_gate": nrm(ks[16], (L, N_EXPERTS, D, EXPERT_FF), D ** -0.5),
        "moe_w_up": nrm(ks[17], (L, N_EXPERTS, D, EXPERT_FF), D ** -0.5),
        "moe_w_down": nrm(ks[18], (L, N_EXPERTS, EXPERT_FF, D), EXPERT_FF ** -0.5 * DEEPNORM_BETA),
        "sh_w_gate": nrm(ks[19], (L, D, SHARED_FF), D ** -0.5),
        "sh_w_up": nrm(ks[20], (L, D, SHARED_FF), D ** -0.5),
        "sh_w_down": nrm(ks[21], (L, SHARED_FF, D), SHARED_FF ** -0.5 * DEEPNORM_BETA),
        "ple_w_proj": nrm(ks[22], (L, PLE_DIM, D), PLE_DIM ** -0.5 * DEEPNORM_BETA),
        "ple_w_gate": nrm(ks[23], (L, D, D), D ** -0.5),
        "ln2_g": 1.0 + nrm(ks[24], (L, D), 0.02),
        "ln2_b": nrm(ks[25], (L, D), 0.02),
    }


def reference(x, p, ln_in_g, ln_in_b, w_in, sg_ln_g, sg_ln_b, sg_w_s, sg_b_s, w_branch_a, w_branch_b,
              w_out, ln1_g, ln1_b, router_w, router_bias, moe_w_gate, moe_w_up, moe_w_down,
              sh_w_gate, sh_w_up, sh_w_down, ple_w_proj, ple_w_gate, ln2_g, ln2_b):
    bsz, seq, _ = x.shape
    h = layer_norm(x, ln_in_g, ln_in_b)
    for i in range(DEPTH):
        proj = h @ w_in[i]
        u, v_sg, q, k, v_sb, g_a, g_b = jnp.split(proj, IN_SPLITS, axis=-1)
        y_a = spatial_gating(jax.nn.gelu(u), jax.nn.gelu(v_sg), sg_ln_g[i], sg_ln_b[i], sg_w_s[i], sg_b_s[i])
        y_b = stick_breaking_attention(q.reshape(bsz, seq, SB_HEADS, SB_HEAD_DIM),
                                       k.reshape(bsz, seq, SB_HEADS, SB_HEAD_DIM),
                                       v_sb.reshape(bsz, seq, SB_HEADS, SB_HEAD_DIM))
        merged = jax.nn.sigmoid(g_a) * (y_a @ w_branch_a[i]) + jax.nn.sigmoid(g_b) * (y_b @ w_branch_b[i])
        h = layer_norm(DEEPNORM_ALPHA * h + merged @ w_out[i], ln1_g[i], ln1_b[i])
        r = DEEPNORM_ALPHA * h + moe_ffn(h, router_w[i], router_bias[i], moe_w_gate[i], moe_w_up[i],
                                         moe_w_down[i], sh_w_gate[i], sh_w_up[i], sh_w_down[i])
        r = r + jax.nn.sigmoid(r @ ple_w_gate[i]) * (p[i] @ ple_w_proj[i])
        h = layer_norm(r, ln2_g[i], ln2_b[i])
    return h
```

```python
import functools
import math

import jax
import jax.numpy as jnp
from jax import lax
from jax.experimental import pallas as pl
from jax.experimental.pallas import tpu as pltpu

F32 = jnp.float32
BF16 = jnp.bfloat16
I32 = jnp.int32

D_MODEL = 1024
DEPTH = 1
CHUNK = 64
PLE_DIM = 256
SG_BLOCK = 128
SG_GROUPS = 4
SG_WIDTH = 512
SB_HEADS = 8
SB_HEAD_DIM = 64
SB_WIDTH = SB_HEADS * SB_HEAD_DIM
N_EXPERTS = 64
N_GROUPS = 8
GROUP_SIZE = N_EXPERTS // N_GROUPS
TOPK_GROUPS = 4
TOP_K = 8
EXPERT_FF = 256
SHARED_FF = 256
ROUTED_SCALE = 2.5
LN_EPS = 1e-5
DEEPNORM_ALPHA = (2.0 * DEPTH) ** 0.25

_OFF_U, _OFF_V, _OFF_Q, _OFF_K, _OFF_VS, _OFF_GA, _OFF_GB, _OFF_END = (
    0, 512, 1024, 1536, 2048, 2560, 3584, 4608)

LANES = 128
HEAD_PAIR = LANES
ROW_BLOCK = 256
VMEM_LIMIT = 56 * 1024 * 1024

_NT = (((1,), (1,)), ((), ()))


def _layer_norm(x, g, b):
    mu = jnp.mean(x, axis=-1, keepdims=True)
    xc = x - mu
    var = jnp.mean(xc * xc, axis=-1, keepdims=True)
    return xc * lax.rsqrt(var + LN_EPS) * g + b


def _const_spec(shape):
    nd = len(shape)
    return pl.BlockSpec(shape, lambda *_: (0,) * nd, pipeline_mode=pl.Buffered(1))


def _inproj_body(x_ref, g_ref, b_ref, w_ref, sgg_ref, sgb_ref, ws_ref, bs_ref, wba_ref,
                 q_ref, k_ref, v_ref, ma_ref, gb_ref, ya_scr):
    tm = x_ref.shape[0]
    hb = _layer_norm(x_ref[...], g_ref[...], b_ref[...]).astype(BF16)

    def proj(lo, hi):
        return jnp.dot(hb, w_ref[:, lo:hi], preferred_element_type=F32)

    q_ref[...] = (proj(_OFF_Q, _OFF_K) * (1.0 / math.sqrt(SB_HEAD_DIM))).astype(BF16)
    k_ref[...] = proj(_OFF_K, _OFF_VS).astype(BF16)
    v_ref[...] = proj(_OFF_VS, _OFF_GA).astype(BF16)
    gb_ref[...] = jax.nn.sigmoid(proj(_OFF_GB, _OFF_END)).astype(BF16)

    yu = jax.nn.gelu(proj(_OFF_U, _OFF_V))
    vn = _layer_norm(jax.nn.gelu(proj(_OFF_V, _OFF_Q)), sgg_ref[...], sgb_ref[...]).astype(BF16)

    pi = lax.broadcasted_iota(I32, (SG_BLOCK, SG_BLOCK), 0) // CHUNK
    pj = lax.broadcasted_iota(I32, (SG_BLOCK, SG_BLOCK), 1) // CHUNK
    chunk_causal = pj <= pi
    gch = SG_WIDTH // SG_GROUPS
    for g in range(SG_GROUPS):
        wm = jnp.where(chunk_causal, ws_ref[g], 0.0).astype(BF16)
        bias = bs_ref[:, g:g + 1]
        for r in range(tm // SG_BLOCK):
            rows = slice(r * SG_BLOCK, (r + 1) * SG_BLOCK)
            cols = slice(g * gch, (g + 1) * gch)
            s = jnp.dot(wm, vn[rows, cols], preferred_element_type=F32) + bias
            ya_scr[rows, cols] = (yu[rows, cols] * s).astype(BF16)

    ba = jnp.dot(ya_scr[...], wba_ref[...], preferred_element_type=F32)
    ma_ref[...] = (jax.nn.sigmoid(proj(_OFF_GA, _OFF_GB)) * ba).astype(BF16)


def _inproj(x2, ln_g, ln_b, w_in, sg_g, sg_b, w_s, b_sT, w_ba, *, tm):
    t = x2.shape[0]
    row = lambda w: pl.BlockSpec((tm, w), lambda i: (i, 0))
    return pl.pallas_call(
        _inproj_body,
        grid=(t // tm,),
        in_specs=[
            row(D_MODEL),
            _const_spec((1, D_MODEL)), _const_spec((1, D_MODEL)),
            _const_spec(w_in.shape),
            _const_spec((1, SG_WIDTH)), _const_spec((1, SG_WIDTH)),
            _const_spec(w_s.shape), _const_spec(b_sT.shape),
            _const_spec(w_ba.shape),
        ],
        out_specs=[row(SB_WIDTH), row(SB_WIDTH), row(SB_WIDTH), row(D_MODEL), row(D_MODEL)],
        out_shape=[
            jax.ShapeDtypeStruct((t, SB_WIDTH), BF16),
            jax.ShapeDtypeStruct((t, SB_WIDTH), BF16),
            jax.ShapeDtypeStruct((t, SB_WIDTH), BF16),
            jax.ShapeDtypeStruct((t, D_MODEL), BF16),
            jax.ShapeDtypeStruct((t, D_MODEL), BF16),
        ],
        scratch_shapes=[pltpu.VMEM((tm, SG_WIDTH), BF16)],
        compiler_params=pltpu.CompilerParams(
            dimension_semantics=("arbitrary",), vmem_limit_bytes=VMEM_LIMIT),
        name="inproj",
    )(x2, ln_g, ln_b, w_in, sg_g, sg_b, w_s, b_sT, w_ba)


def _attn_body(q_ref, k_ref, v_ref, o_ref, acc_scr, car_scr, *, tq):
    qi = pl.program_id(2)
    q2 = q_ref[0]
    lane = lax.broadcasted_iota(I32, (tq, HEAD_PAIR), 1)
    zero = jnp.zeros_like(q2)
    qh = (jnp.where(lane < SB_HEAD_DIM, q2, zero), jnp.where(lane >= SB_HEAD_DIM, q2, zero))
    rr = lax.broadcasted_iota(I32, (tq, tq), 0)
    cc = lax.broadcasted_iota(I32, (tq, tq), 1)
    later = (rr > cc).astype(BF16)
    causal = cc < rr

    def block(kb, vb, diag):
        for h in range(2):
            z = lax.dot_general(qh[h], kb, _NT, preferred_element_type=F32)
            lm = -(jnp.maximum(z, 0.0) + jnp.log(1.0 + jnp.exp(-jnp.abs(z))))
            if diag:
                lm = jnp.where(causal, lm, 0.0)
            hi = lm.astype(BF16)
            lo = (lm - hi.astype(F32)).astype(BF16)
            tail = (jnp.dot(hi, later, preferred_element_type=F32)
                    + jnp.dot(lo, later, preferred_element_type=F32))
            a = jnp.exp(z + lm + tail + car_scr[h])
            if diag:
                a = jnp.where(causal, a, 0.0)
            acc_scr[h] += jnp.dot(a.astype(BF16), vb, preferred_element_type=F32)
            car_scr[h] += jnp.sum(lm, axis=1, keepdims=True)

    acc_scr[...] = jnp.zeros_like(acc_scr)
    car_scr[...] = jnp.zeros_like(car_scr)
    d0 = pl.multiple_of(qi * tq, tq)
    block(k_ref[0, pl.ds(d0, tq), :], v_ref[0, pl.ds(d0, tq), :], True)

    def earlier(j, c):
        k0 = pl.multiple_of((qi - 1 - j) * tq, tq)
        block(k_ref[0, pl.ds(k0, tq), :], v_ref[0, pl.ds(k0, tq), :], False)
        return c

    lax.fori_loop(0, qi, earlier, 0)
    o_ref[0] = jnp.where(lane < SB_HEAD_DIM, acc_scr[0], acc_scr[1]).astype(BF16)


def _attn(q3, k3, v3, *, tq):
    b, s, _ = q3.shape
    qspec = pl.BlockSpec((1, tq, HEAD_PAIR), lambda bi, hp, qi: (bi, qi, hp))
    kvspec = pl.BlockSpec((1, s, HEAD_PAIR), lambda bi, hp, qi: (bi, 0, hp))
    return pl.pallas_call(
        functools.partial(_attn_body, tq=tq),
        grid=(b, SB_WIDTH // HEAD_PAIR, s // tq),
        in_specs=[qspec, kvspec, kvspec],
        out_specs=qspec,
        out_shape=jax.ShapeDtypeStruct((b, s, SB_WIDTH), BF16),
        scratch_shapes=[pltpu.VMEM((2, tq, HEAD_PAIR), F32), pltpu.VMEM((2, tq, 1), F32)],
        compiler_params=pltpu.CompilerParams(
            dimension_semantics=("arbitrary", "arbitrary", "arbitrary"),
            vmem_limit_bytes=VMEM_LIMIT),
        name="attn",
    )(q3, k3, v3)


def _post_body(x_ref, g_ref, b_ref, ma_ref, gb_ref, yb_ref, wbb_ref, wout_ref, l1g_ref, l1b_ref,
               rwh_ref, rwl_ref, rb_ref,
               h1_ref, eidx_ref, wsel_ref, pos_ref, cnt_ref, cnt_scr):
    tm = x_ref.shape[0]
    step = pl.program_id(0)

    @pl.when(step == 0)
    def _():
        cnt_scr[...] = jnp.zeros_like(cnt_scr)

    h = _layer_norm(x_ref[...], g_ref[...], b_ref[...])
    bb = jnp.dot(yb_ref[...], wbb_ref[...], preferred_element_type=F32)
    merged = ma_ref[...].astype(F32) + gb_ref[...].astype(F32) * bb
    o = jnp.dot(merged.astype(BF16), wout_ref[...], preferred_element_type=F32)
    h1 = _layer_norm(DEEPNORM_ALPHA * h + o, l1g_ref[...], l1b_ref[...])
    h1_ref[...] = h1

    hh = h1.astype(BF16)
    hl = (h1 - hh.astype(F32)).astype(BF16)
    logits = (lax.dot_general(rwh_ref[...], hh, _NT, preferred_element_type=F32)
              + lax.dot_general(rwh_ref[...], hl, _NT, preferred_element_type=F32)
              + lax.dot_general(rwl_ref[...], hh, _NT, preferred_element_type=F32))
    scores = jax.nn.sigmoid(logits)
    sel = scores + rb_ref[...]

    sub = lax.broadcasted_iota(I32, (GROUP_SIZE, tm), 0).astype(F32)
    neg = jnp.float32(-jnp.inf)
    selg = [sel[g * GROUP_SIZE:(g + 1) * GROUP_SIZE] for g in range(N_GROUPS)]
    scg = [scores[g * GROUP_SIZE:(g + 1) * GROUP_SIZE] for g in range(N_GROUPS)]
    eid = [sub + float(g * GROUP_SIZE) for g in range(N_GROUPS)]

    grp = []
    for g in range(N_GROUPS):
        m1 = jnp.max(selg[g], axis=0, keepdims=True)
        i1 = jnp.min(jnp.where(selg[g] == m1, sub, float(GROUP_SIZE)), axis=0, keepdims=True)
        m2 = jnp.max(jnp.where(sub == i1, neg, selg[g]), axis=0, keepdims=True)
        grp.append(m1 + m2)
    cur = []
    for g in range(N_GROUPS):
        beaten = jnp.zeros((1, tm), F32)
        for o_g in range(N_GROUPS):
            if o_g == g:
                continue
            wins = (grp[o_g] > grp[g]) if o_g > g else (grp[o_g] >= grp[g])
            beaten = beaten + jnp.where(wins, 1.0, 0.0)
        cur.append(jnp.where(beaten < float(TOPK_GROUPS), selg[g], neg))

    def all_reduce(vals, op):
        acc = vals[0]
        for v in vals[1:]:
            acc = op(acc, v)
        return acc

    idx_rows, w_rows = [], []
    member = [jnp.zeros((GROUP_SIZE, tm), F32) for _ in range(N_GROUPS)]
    for _ in range(TOP_K):
        m = jnp.max(all_reduce(cur, jnp.maximum), axis=0, keepdims=True)
        cand = [jnp.where(cur[g] == m, eid[g], float(N_EXPERTS)) for g in range(N_GROUPS)]
        idx = jnp.min(all_reduce(cand, jnp.minimum), axis=0, keepdims=True)
        hit = [eid[g] == idx for g in range(N_GROUPS)]
        w = jnp.sum(all_reduce([jnp.where(hit[g], scg[g], 0.0) for g in range(N_GROUPS)], jnp.add),
                    axis=0, keepdims=True)
        cur = [jnp.where(hit[g], neg, cur[g]) for g in range(N_GROUPS)]
        member = [jnp.where(hit[g], 1.0, member[g]) for g in range(N_GROUPS)]
        idx_rows.append(idx)
        w_rows.append(w)

    wsum = all_reduce(w_rows, jnp.add)
    for kk in range(TOP_K):
        eidx_ref[kk:kk + 1, :] = idx_rows[kk].astype(I32)
        wsel_ref[kk:kk + 1, :] = w_rows[kk] / wsum * ROUTED_SCALE

    tt = lax.broadcasted_iota(I32, (tm, tm), 0)
    uu = lax.broadcasted_iota(I32, (tm, tm), 1)
    before = (tt < uu).astype(BF16)
    memb = jnp.concatenate(member, axis=0)
    prefix = jnp.dot(memb.astype(BF16), before, preferred_element_type=F32) + cnt_scr[...]
    for kk in range(TOP_K):
        picked = [jnp.where(eid[g] == idx_rows[kk],
                            prefix[g * GROUP_SIZE:(g + 1) * GROUP_SIZE], 0.0) for g in range(N_GROUPS)]
        pos_ref[kk:kk + 1, :] = jnp.sum(all_reduce(picked, jnp.add), axis=0, keepdims=True).astype(I32)
    total = cnt_scr[...] + jnp.sum(memb, axis=1, keepdims=True)
    cnt_scr[...] = total
    cnt_ref[...] = jnp.broadcast_to(total, cnt_ref.shape).astype(I32)


def _post(x2, ln_g, ln_b, ma, gb, yb, w_bb, w_out, l1g, l1b, rw_hi, rw_lo, r_bias, *, tm):
    t = x2.shape[0]
    row = lambda w: pl.BlockSpec((tm, w), lambda i: (i, 0))
    tok = pl.BlockSpec((TOP_K, tm), lambda i: (0, i))
    return pl.pallas_call(
        _post_body,
        grid=(t // tm,),
        in_specs=[
            row(D_MODEL), _const_spec((1, D_MODEL)), _const_spec((1, D_MODEL)),
            row(D_MODEL), row(D_MODEL), row(SB_WIDTH),
            _const_spec(w_bb.shape), _const_spec(w_out.shape),
            _const_spec((1, D_MODEL)), _const_spec((1, D_MODEL)),
            _const_spec(rw_hi.shape), _const_spec(rw_lo.shape), _const_spec(r_bias.shape),
        ],
        out_specs=[row(D_MODEL), tok, tok, tok,
                   pl.BlockSpec((N_EXPERTS, LANES), lambda i: (0, 0))],
        out_shape=[
            jax.ShapeDtypeStruct((t, D_MODEL), F32),
            jax.ShapeDtypeStruct((TOP_K, t), I32),
            jax.ShapeDtypeStruct((TOP_K, t), F32),
            jax.ShapeDtypeStruct((TOP_K, t), I32),
            jax.ShapeDtypeStruct((N_EXPERTS, LANES), I32),
        ],
        scratch_shapes=[pltpu.VMEM((N_EXPERTS, 1), F32)],
        compiler_params=pltpu.CompilerParams(
            dimension_semantics=("arbitrary",), vmem_limit_bytes=VMEM_LIMIT),
        name="post",
    )(x2, ln_g, ln_b, ma, gb, yb, w_bb, w_out, l1g, l1b, rw_hi, rw_lo, r_bias)


def _dispatch_body(zrow_ref, dest_ref, h_ref, xs_ref, zero_scr, zsem, sem):
    tm = h_ref.shape[0]
    step = pl.program_id(0)

    def zero_copy(r):
        return pltpu.make_async_copy(zero_scr, xs_ref.at[pl.ds(r, ROW_BLOCK)], zsem)

    def row_copy(t, d):
        return pltpu.make_async_copy(h_ref.at[pl.ds(t, 1)], xs_ref.at[pl.ds(d, 1)], sem)

    @pl.when(step == 0)
    def _():
        zero_scr[...] = jnp.zeros_like(zero_scr)

        def start(e, c):
            r = zrow_ref[e]

            @pl.when(r >= 0)
            def _():
                zero_copy(pl.multiple_of(jnp.maximum(r, 0), ROW_BLOCK)).start()
            return c

        def wait(e, c):
            @pl.when(zrow_ref[e] >= 0)
            def _():
                zero_copy(0).wait()
            return c

        lax.fori_loop(0, N_EXPERTS, start, 0)
        lax.fori_loop(0, N_EXPERTS, wait, 0)

    def issue(t, c):
        for kk in range(TOP_K):
            row_copy(t, dest_ref[kk, t]).start()
        return c

    def drain(t, c):
        for kk in range(TOP_K):
            row_copy(0, 0).wait()
        return c

    lax.fori_loop(0, tm, issue, 0)
    lax.fori_loop(0, tm, drain, 0)


def _dispatch(zrow, dest, h1, n_rows, *, tm):
    t = h1.shape[0]
    return pl.pallas_call(
        _dispatch_body,
        grid_spec=pltpu.PrefetchScalarGridSpec(
            num_scalar_prefetch=1,
            grid=(t // tm,),
            in_specs=[
                pl.BlockSpec((TOP_K, tm), lambda i, z: (0, i), memory_space=pltpu.SMEM),
                pl.BlockSpec((tm, D_MODEL), lambda i, z: (i, 0)),
            ],
            out_specs=pl.BlockSpec(memory_space=pl.ANY),
            scratch_shapes=[pltpu.VMEM((ROW_BLOCK, D_MODEL), F32),
                            pltpu.SemaphoreType.DMA(()), pltpu.SemaphoreType.DMA(())],
        ),
        out_shape=jax.ShapeDtypeStruct((n_rows, D_MODEL), F32),
        compiler_params=pltpu.CompilerParams(
            dimension_semantics=("arbitrary",), vmem_limit_bytes=VMEM_LIMIT),
        name="dispatch",
    )(zrow, dest, h1)


def _experts_body(be_ref, nu_ref, xs_ref, wg_ref, wu_ref, wd_ref, ys_ref):
    blk = pl.program_id(0)

    @pl.when(blk < nu_ref[0])
    def _():
        xb = xs_ref[...].astype(BF16)
        gate = jnp.dot(xb, wg_ref[0], preferred_element_type=F32)
        up = jnp.dot(xb, wu_ref[0], preferred_element_type=F32)
        hid = (jax.nn.silu(gate) * up).astype(BF16)
        ys_ref[...] = jnp.dot(hid, wd_ref[0], preferred_element_type=F32)

    @pl.when(blk >= nu_ref[0])
    def _():
        ys_ref[...] = jnp.zeros_like(ys_ref)


def _experts(block_e, n_used, xs, w_gate, w_up, w_down):
    n_rows = xs.shape[0]
    n_blocks = n_rows // ROW_BLOCK
    last = lambda b, be, nu: jnp.minimum(b, nu[0] - 1)
    return pl.pallas_call(
        _experts_body,
        grid_spec=pltpu.PrefetchScalarGridSpec(
            num_scalar_prefetch=2,
            grid=(n_blocks,),
            in_specs=[
                pl.BlockSpec((ROW_BLOCK, D_MODEL), lambda b, be, nu: (last(b, be, nu), 0)),
                pl.BlockSpec((1, D_MODEL, EXPERT_FF), lambda b, be, nu: (be[last(b, be, nu)], 0, 0)),
                pl.BlockSpec((1, D_MODEL, EXPERT_FF), lambda b, be, nu: (be[last(b, be, nu)], 0, 0)),
                pl.BlockSpec((1, EXPERT_FF, D_MODEL), lambda b, be, nu: (be[last(b, be, nu)], 0, 0)),
            ],
            out_specs=pl.BlockSpec((ROW_BLOCK, D_MODEL), lambda b, be, nu: (b, 0)),
        ),
        out_shape=jax.ShapeDtypeStruct((n_rows, D_MODEL), F32),
        compiler_params=pltpu.CompilerParams(
            dimension_semantics=("arbitrary",), vmem_limit_bytes=VMEM_LIMIT),
        name="experts",
    )(block_e, n_used, xs, w_gate, w_up, w_down)


def _combine_body(dest_ref, h1_ref, w_ref, p_ref, ys_ref, shg_ref, shu_ref, shd_ref,
                  pg_ref, pp_ref, g_ref, b_ref, o_ref, buf, sem):
    tm = h1_ref.shape[0]

    def row_copy(kk, t, d):
        return pltpu.make_async_copy(ys_ref.at[pl.ds(d, 1)], buf.at[kk, pl.ds(t, 1)], sem)

    def issue(t, c):
        for kk in range(TOP_K):
            row_copy(kk, t, dest_ref[kk, t]).start()
        return c

    def drain(t, c):
        for kk in range(TOP_K):
            row_copy(kk, 0, 0).wait()
        return c

    lax.fori_loop(0, tm, issue, 0)

    h1 = h1_ref[...]
    hb = h1.astype(BF16)
    hid = (jax.nn.silu(jnp.dot(hb, shg_ref[...], preferred_element_type=F32))
           * jnp.dot(hb, shu_ref[...], preferred_element_type=F32)).astype(BF16)
    r = DEEPNORM_ALPHA * h1 + jnp.dot(hid, shd_ref[...], preferred_element_type=F32)
    emb = jnp.dot(p_ref[...].astype(BF16), pp_ref[...], preferred_element_type=F32)

    lax.fori_loop(0, tm, drain, 0)
    w = w_ref[...]
    for kk in range(TOP_K):
        r = r + w[:, kk:kk + 1] * buf[kk]
    gate = jax.nn.sigmoid(jnp.dot(r.astype(BF16), pg_ref[...], preferred_element_type=F32))
    o_ref[...] = _layer_norm(r + gate * emb, g_ref[...], b_ref[...])


def _combine(dest, h1, wtok, p2, ys, sh_g, sh_u, sh_d, ple_g, ple_p, l2g, l2b, *, tm):
    t = h1.shape[0]
    row = lambda w: pl.BlockSpec((tm, w), lambda i: (i, 0))
    return pl.pallas_call(
        _combine_body,
        grid=(t // tm,),
        in_specs=[
            pl.BlockSpec((TOP_K, tm), lambda i: (0, i), memory_space=pltpu.SMEM),
            row(D_MODEL), row(TOP_K), row(PLE_DIM),
            pl.BlockSpec(memory_space=pl.ANY),
            _const_spec(sh_g.shape), _const_spec(sh_u.shape), _const_spec(sh_d.shape),
            _const_spec(ple_g.shape), _const_spec(ple_p.shape),
            _const_spec((1, D_MODEL)), _const_spec((1, D_MODEL)),
        ],
        out_specs=row(D_MODEL),
        out_shape=jax.ShapeDtypeStruct((t, D_MODEL), F32),
        scratch_shapes=[pltpu.VMEM((TOP_K, tm, D_MODEL), F32), pltpu.SemaphoreType.DMA(())],
        compiler_params=pltpu.CompilerParams(
            dimension_semantics=("arbitrary",), vmem_limit_bytes=VMEM_LIMIT),
        name="combine",
    )(dest, h1, wtok, p2, ys, sh_g, sh_u, sh_d, ple_g, ple_p, l2g, l2b)


def _block_layout(counts, n_blocks):
    padded = (counts + ROW_BLOCK - 1) // ROW_BLOCK * ROW_BLOCK
    pad_ends = jnp.cumsum(padded)
    pad_starts = pad_ends - padded
    block_e = jnp.minimum(
        jnp.searchsorted(pad_ends, jnp.arange(n_blocks, dtype=I32) * ROW_BLOCK, side="right"),
        N_EXPERTS - 1).astype(I32)
    n_used = (pad_ends[-1:] // ROW_BLOCK).astype(I32)
    has_pad = (counts % ROW_BLOCK) != 0
    zrow = jnp.where(has_pad, pad_ends - ROW_BLOCK, -1).astype(I32)
    return pad_starts.astype(I32), block_e, n_used, zrow


def kernel(x, p, ln_in_g, ln_in_b, w_in, sg_ln_g, sg_ln_b, sg_w_s, sg_b_s, w_branch_a, w_branch_b,
           w_out, ln1_g, ln1_b, router_w, router_bias, moe_w_gate, moe_w_up, moe_w_down,
           sh_w_gate, sh_w_up, sh_w_down, ple_w_proj, ple_w_gate, ln2_g, ln2_b):
    bsz, seq, dm = x.shape
    t = bsz * seq
    vec = lambda a: a.reshape(1, -1).astype(F32)
    assert DEPTH == 1
    h = x.reshape(t, dm)
    for i in range(DEPTH):
        q, k, v, ma, gb = _inproj(
            h, vec(ln_in_g), vec(ln_in_b), w_in[i].astype(BF16), vec(sg_ln_g[i]), vec(sg_ln_b[i]),
            sg_w_s[i], sg_b_s[i].T, w_branch_a[i].astype(BF16), tm=256)
        yb = _attn(q.reshape(bsz, seq, SB_WIDTH), k.reshape(bsz, seq, SB_WIDTH),
                   v.reshape(bsz, seq, SB_WIDTH), tq=256).reshape(t, SB_WIDTH)

        rw_t = router_w[i].T
        rw_hi = rw_t.astype(BF16)
        rw_lo = (rw_t - rw_hi.astype(F32)).astype(BF16)
        h1, eidx, wsel, pos, cnt = _post(
            h, vec(ln_in_g), vec(ln_in_b), ma, gb, yb, w_branch_b[i].astype(BF16),
            w_out[i].astype(BF16), vec(ln1_g[i]), vec(ln1_b[i]), rw_hi, rw_lo,
            router_bias[i].reshape(N_EXPERTS, 1).astype(F32), tm=256)

        n_blocks = t * TOP_K // ROW_BLOCK + N_EXPERTS
        pad_starts, block_e, n_used, zrow = _block_layout(cnt[:, 0], n_blocks)
        dest = jnp.take(pad_starts, eidx) + pos

        xs = _dispatch(zrow, dest, h1, n_blocks * ROW_BLOCK, tm=256)
        ys = _experts(block_e, n_used, xs, moe_w_gate[i].astype(BF16), moe_w_up[i].astype(BF16),
                      moe_w_down[i].astype(BF16))
        h = _combine(dest, h1, wsel.T, p[i].reshape(t, PLE_DIM), ys,
                     sh_w_gate[i].astype(BF16), sh_w_up[i].astype(BF16), sh_w_down[i].astype(BF16),
                     ple_w_gate[i].astype(BF16), ple_w_proj[i].astype(BF16),
                     vec(ln2_g[i]), vec(ln2_b[i]), tm=256)
    return h.reshape(bsz, seq, dm)
```

```python
import functools
import math

import jax
import jax.numpy as jnp
from jax import lax
from jax.experimental import pallas as pl
from jax.experimental.pallas import tpu as pltpu

F32 = jnp.float32
BF16 = jnp.bfloat16
I32 = jnp.int32

D_MODEL = 1024
DEPTH = 1
CHUNK = 64
PLE_DIM = 256
SG_BLOCK = 128
SG_GROUPS = 4
SG_WIDTH = 512
SB_HEADS = 8
SB_HEAD_DIM = 64
SB_WIDTH = SB_HEADS * SB_HEAD_DIM
N_EXPERTS = 64
N_GROUPS = 8
GROUP_SIZE = N_EXPERTS // N_GROUPS
TOPK_GROUPS = 4
TOP_K = 8
EXPERT_FF = 256
SHARED_FF = 256
ROUTED_SCALE = 2.5
LN_EPS = 1e-5
DEEPNORM_ALPHA = (2.0 * DEPTH) ** 0.25

_OFF_U, _OFF_V, _OFF_Q, _OFF_K, _OFF_VS, _OFF_GA, _OFF_GB, _OFF_END = (
    0, 512, 1024, 1536, 2048, 2560, 3584, 4608)

LANES = 128
ATTN_LANES = 256
ATTN_HEADS = ATTN_LANES // SB_HEAD_DIM
ROW_BLOCK = 256
VMEM_LIMIT = 56 * 1024 * 1024

_NT = (((1,), (1,)), ((), ()))


def _layer_norm(x, g, b):
    mu = jnp.mean(x, axis=-1, keepdims=True)
    xc = x - mu
    var = jnp.mean(xc * xc, axis=-1, keepdims=True)
    return xc * lax.rsqrt(var + LN_EPS) * g + b


def _const_spec(shape):
    nd = len(shape)
    return pl.BlockSpec(shape, lambda *_: (0,) * nd, pipeline_mode=pl.Buffered(1))


def _inproj_body(x_ref, g_ref, b_ref, w_ref, sgg_ref, sgb_ref, ws_ref, bs_ref, wba_ref,
                 q_ref, k_ref, v_ref, ma_ref, gb_ref, ya_scr):
    tm = x_ref.shape[0]
    hb = _layer_norm(x_ref[...], g_ref[...], b_ref[...]).astype(BF16)

    def proj(lo, hi):
        return jnp.dot(hb, w_ref[:, lo:hi], preferred_element_type=F32)

    q_ref[...] = (proj(_OFF_Q, _OFF_K) * (1.0 / math.sqrt(SB_HEAD_DIM))).astype(BF16)
    k_ref[...] = proj(_OFF_K, _OFF_VS).astype(BF16)
    v_ref[...] = proj(_OFF_VS, _OFF_GA).astype(BF16)
    gb_ref[...] = jax.nn.sigmoid(proj(_OFF_GB, _OFF_END)).astype(BF16)

    yu = jax.nn.gelu(proj(_OFF_U, _OFF_V))
    vn = _layer_norm(jax.nn.gelu(proj(_OFF_V, _OFF_Q)), sgg_ref[...], sgb_ref[...]).astype(BF16)

    pi = lax.broadcasted_iota(I32, (SG_BLOCK, SG_BLOCK), 0) // CHUNK
    pj = lax.broadcasted_iota(I32, (SG_BLOCK, SG_BLOCK), 1) // CHUNK
    chunk_causal = pj <= pi
    gch = SG_WIDTH // SG_GROUPS
    for g in range(SG_GROUPS):
        wm = jnp.where(chunk_causal, ws_ref[g], 0.0).astype(BF16)
        bias = bs_ref[:, g:g + 1]
        for r in range(tm // SG_BLOCK):
            rows = slice(r * SG_BLOCK, (r + 1) * SG_BLOCK)
            cols = slice(g * gch, (g + 1) * gch)
            s = jnp.dot(wm, vn[rows, cols], preferred_element_type=F32) + bias
            ya_scr[rows, cols] = (yu[rows, cols] * s).astype(BF16)

    ba = jnp.dot(ya_scr[...], wba_ref[...], preferred_element_type=F32)
    ma_ref[...] = (jax.nn.sigmoid(proj(_OFF_GA, _OFF_GB)) * ba).astype(BF16)


def _inproj(x2, ln_g, ln_b, w_in, sg_g, sg_b, w_s, b_sT, w_ba, *, tm):
    t = x2.shape[0]
    row = lambda w: pl.BlockSpec((tm, w), lambda i: (i, 0))
    return pl.pallas_call(
        _inproj_body,
        grid=(t // tm,),
        in_specs=[
            row(D_MODEL),
            _const_spec((1, D_MODEL)), _const_spec((1, D_MODEL)),
            _const_spec(w_in.shape),
            _const_spec((1, SG_WIDTH)), _const_spec((1, SG_WIDTH)),
            _const_spec(w_s.shape), _const_spec(b_sT.shape),
            _const_spec(w_ba.shape),
        ],
        out_specs=[row(SB_WIDTH), row(SB_WIDTH), row(SB_WIDTH), row(D_MODEL), row(D_MODEL)],
        out_shape=[
            jax.ShapeDtypeStruct((t, SB_WIDTH), BF16),
            jax.ShapeDtypeStruct((t, SB_WIDTH), BF16),
            jax.ShapeDtypeStruct((t, SB_WIDTH), BF16),
            jax.ShapeDtypeStruct((t, D_MODEL), BF16),
            jax.ShapeDtypeStruct((t, D_MODEL), BF16),
        ],
        scratch_shapes=[pltpu.VMEM((tm, SG_WIDTH), BF16)],
        compiler_params=pltpu.CompilerParams(
            dimension_semantics=("arbitrary",), vmem_limit_bytes=VMEM_LIMIT),
        name="inproj",
    )(x2, ln_g, ln_b, w_in, sg_g, sg_b, w_s, b_sT, w_ba)


def _attn_body(q_ref, k_ref, v_ref, o_ref, acc_scr, car_scr, *, tq):
    qi = pl.program_id(2)
    q2 = q_ref[0]
    lane_head = lax.broadcasted_iota(I32, (tq, ATTN_LANES), 1) // SB_HEAD_DIM
    zero = jnp.zeros_like(q2)
    in_head = [lane_head == h for h in range(ATTN_HEADS)]
    qh = [jnp.where(in_head[h], q2, zero) for h in range(ATTN_HEADS)]
    rr = lax.broadcasted_iota(I32, (tq, tq), 0)
    cc = lax.broadcasted_iota(I32, (tq, tq), 1)
    later = (rr > cc).astype(BF16)
    causal = cc < rr

    def block(kb, vb, diag):
        for h in range(ATTN_HEADS):
            z = lax.dot_general(qh[h], kb, _NT, preferred_element_type=F32)
            lm = -(jnp.maximum(z, 0.0) + jnp.log(1.0 + jnp.exp(-jnp.abs(z))))
            lb = lm + z
            if diag:
                lm = jnp.where(causal, lm, 0.0)
            tail = jnp.dot(lm.astype(BF16), later, preferred_element_type=F32)
            a = jnp.exp(lb + tail + car_scr[h])
            if diag:
                a = jnp.where(causal, a, 0.0)
            acc_scr[h] += jnp.dot(a.astype(BF16), vb, preferred_element_type=F32)
            car_scr[h] += jnp.sum(lm, axis=1, keepdims=True)

    acc_scr[...] = jnp.zeros_like(acc_scr)
    car_scr[...] = jnp.zeros_like(car_scr)
    d0 = pl.multiple_of(qi * tq, tq)
    block(k_ref[0, pl.ds(d0, tq), :], v_ref[0, pl.ds(d0, tq), :], True)

    def earlier(j, c):
        k0 = pl.multiple_of((qi - 1 - j) * tq, tq)
        block(k_ref[0, pl.ds(k0, tq), :], v_ref[0, pl.ds(k0, tq), :], False)
        return c

    lax.fori_loop(0, qi, earlier, 0)
    out = acc_scr[0]
    for h in range(1, ATTN_HEADS):
        out = jnp.where(in_head[h], acc_scr[h], out)
    o_ref[0] = out.astype(BF16)


def _attn(q3, k3, v3, *, tq):
    b, s, _ = q3.shape
    qspec = pl.BlockSpec((1, tq, ATTN_LANES), lambda bi, hp, qi: (bi, qi, hp))
    kvspec = pl.BlockSpec((1, s, ATTN_LANES), lambda bi, hp, qi: (bi, 0, hp))
    return pl.pallas_call(
        functools.partial(_attn_body, tq=tq),
        grid=(b, SB_WIDTH // ATTN_LANES, s // tq),
        in_specs=[qspec, kvspec, kvspec],
        out_specs=qspec,
        out_shape=jax.ShapeDtypeStruct((b, s, SB_WIDTH), BF16),
        scratch_shapes=[pltpu.VMEM((ATTN_HEADS, tq, ATTN_LANES), F32),
                        pltpu.VMEM((ATTN_HEADS, tq, 1), F32)],
        compiler_params=pltpu.CompilerParams(
            dimension_semantics=("arbitrary", "arbitrary", "arbitrary"),
            vmem_limit_bytes=VMEM_LIMIT),
        name="attn",
    )(q3, k3, v3)


def _post_body(x_ref, g_ref, b_ref, ma_ref, gb_ref, yb_ref, wbb_ref, wout_ref, l1g_ref, l1b_ref,
               rwh_ref, rwl_ref, rb_ref,
               h1_ref, eidx_ref, wsel_ref, pos_ref, cnt_ref, cnt_scr):
    tm = x_ref.shape[0]
    step = pl.program_id(0)

    @pl.when(step == 0)
    def _():
        cnt_scr[...] = jnp.zeros_like(cnt_scr)

    h = _layer_norm(x_ref[...], g_ref[...], b_ref[...])
    bb = jnp.dot(yb_ref[...], wbb_ref[...], preferred_element_type=F32)
    merged = ma_ref[...].astype(F32) + gb_ref[...].astype(F32) * bb
    o = jnp.dot(merged.astype(BF16), wout_ref[...], preferred_element_type=F32)
    h1 = _layer_norm(DEEPNORM_ALPHA * h + o, l1g_ref[...], l1b_ref[...])
    h1_ref[...] = h1

    hh = h1.astype(BF16)
    hl = (h1 - hh.astype(F32)).astype(BF16)
    logits = (lax.dot_general(rwh_ref[...], hh, _NT, preferred_element_type=F32)
              + lax.dot_general(rwh_ref[...], hl, _NT, preferred_element_type=F32)
              + lax.dot_general(rwl_ref[...], hh, _NT, preferred_element_type=F32))
    scores = jax.nn.sigmoid(logits)
    sel = scores + rb_ref[...]

    sub = lax.broadcasted_iota(I32, (GROUP_SIZE, tm), 0).astype(F32)
    neg = jnp.float32(-jnp.inf)
    selg = [sel[g * GROUP_SIZE:(g + 1) * GROUP_SIZE] for g in range(N_GROUPS)]
    scg = [scores[g * GROUP_SIZE:(g + 1) * GROUP_SIZE] for g in range(N_GROUPS)]
    eid = [sub + float(g * GROUP_SIZE) for g in range(N_GROUPS)]

    grp = []
    for g in range(N_GROUPS):
        m1 = jnp.max(selg[g], axis=0, keepdims=True)
        i1 = jnp.min(jnp.where(selg[g] == m1, sub, float(GROUP_SIZE)), axis=0, keepdims=True)
        m2 = jnp.max(jnp.where(sub == i1, neg, selg[g]), axis=0, keepdims=True)
        grp.append(m1 + m2)
    cur = []
    for g in range(N_GROUPS):
        beaten = jnp.zeros((1, tm), F32)
        for o_g in range(N_GROUPS):
            if o_g == g:
                continue
            wins = (grp[o_g] > grp[g]) if o_g > g else (grp[o_g] >= grp[g])
            beaten = beaten + jnp.where(wins, 1.0, 0.0)
        cur.append(jnp.where(beaten < float(TOPK_GROUPS), selg[g], neg))

    def all_reduce(vals, op):
        acc = vals[0]
        for v in vals[1:]:
            acc = op(acc, v)
        return acc

    idx_rows, w_rows = [], []
    member = [jnp.zeros((GROUP_SIZE, tm), F32) for _ in range(N_GROUPS)]
    for _ in range(TOP_K):
        m = jnp.max(all_reduce(cur, jnp.maximum), axis=0, keepdims=True)
        cand = [jnp.where(cur[g] == m, eid[g], float(N_EXPERTS)) for g in range(N_GROUPS)]
        idx = jnp.min(all_reduce(cand, jnp.minimum), axis=0, keepdims=True)
        hit = [eid[g] == idx for g in range(N_GROUPS)]
        w = jnp.sum(all_reduce([jnp.where(hit[g], scg[g], 0.0) for g in range(N_GROUPS)], jnp.add),
                    axis=0, keepdims=True)
        cur = [jnp.where(hit[g], neg, cur[g]) for g in range(N_GROUPS)]
        member = [jnp.where(hit[g], 1.0, member[g]) for g in range(N_GROUPS)]
        idx_rows.append(idx)
        w_rows.append(w)

    wsum = all_reduce(w_rows, jnp.add)
    for kk in range(TOP_K):
        eidx_ref[kk:kk + 1, :] = idx_rows[kk].astype(I32)
        wsel_ref[kk:kk + 1, :] = w_rows[kk] / wsum * ROUTED_SCALE

    tt = lax.broadcasted_iota(I32, (tm, tm), 0)
    uu = lax.broadcasted_iota(I32, (tm, tm), 1)
    before = (tt < uu).astype(BF16)
    memb = jnp.concatenate(member, axis=0)
    prefix = jnp.dot(memb.astype(BF16), before, preferred_element_type=F32) + cnt_scr[...]
    for kk in range(TOP_K):
        picked = [jnp.where(eid[g] == idx_rows[kk],
                            prefix[g * GROUP_SIZE:(g + 1) * GROUP_SIZE], 0.0) for g in range(N_GROUPS)]
        pos_ref[kk:kk + 1, :] = jnp.sum(all_reduce(picked, jnp.add), axis=0, keepdims=True).astype(I32)
    total = cnt_scr[...] + jnp.sum(memb, axis=1, keepdims=True)
    cnt_scr[...] = total
    cnt_ref[...] = jnp.broadcast_to(total, cnt_ref.shape).astype(I32)


def _post(x2, ln_g, ln_b, ma, gb, yb, w_bb, w_out, l1g, l1b, rw_hi, rw_lo, r_bias, *, tm):
    t = x2.shape[0]
    row = lambda w: pl.BlockSpec((tm, w), lambda i: (i, 0))
    tok = pl.BlockSpec((TOP_K, tm), lambda i: (0, i))
    return pl.pallas_call(
        _post_body,
        grid=(t // tm,),
        in_specs=[
            row(D_MODEL), _const_spec((1, D_MODEL)), _const_spec((1, D_MODEL)),
            row(D_MODEL), row(D_MODEL), row(SB_WIDTH),
            _const_spec(w_bb.shape), _const_spec(w_out.shape),
            _const_spec((1, D_MODEL)), _const_spec((1, D_MODEL)),
            _const_spec(rw_hi.shape), _const_spec(rw_lo.shape), _const_spec(r_bias.shape),
        ],
        out_specs=[row(D_MODEL), tok, tok, tok,
                   pl.BlockSpec((N_EXPERTS, LANES), lambda i: (0, 0))],
        out_shape=[
            jax.ShapeDtypeStruct((t, D_MODEL), F32),
            jax.ShapeDtypeStruct((TOP_K, t), I32),
            jax.ShapeDtypeStruct((TOP_K, t), F32),
            jax.ShapeDtypeStruct((TOP_K, t), I32),
            jax.ShapeDtypeStruct((N_EXPERTS, LANES), I32),
        ],
        scratch_shapes=[pltpu.VMEM((N_EXPERTS, 1), F32)],
        compiler_params=pltpu.CompilerParams(
            dimension_semantics=("arbitrary",), vmem_limit_bytes=VMEM_LIMIT),
        name="post",
    )(x2, ln_g, ln_b, ma, gb, yb, w_bb, w_out, l1g, l1b, rw_hi, rw_lo, r_bias)


def _dispatch_body(zrow_ref, dest_ref, h_ref, xs_ref, zero_scr, zsem, sem):
    tm = h_ref.shape[0]
    step = pl.program_id(0)

    def zero_copy(r):
        return pltpu.make_async_copy(zero_scr, xs_ref.at[pl.ds(r, ROW_BLOCK)], zsem)

    def row_copy(t, d):
        return pltpu.make_async_copy(h_ref.at[pl.ds(t, 1)], xs_ref.at[pl.ds(d, 1)], sem)

    @pl.when(step == 0)
    def _():
        zero_scr[...] = jnp.zeros_like(zero_scr)

        def start(e, c):
            r = zrow_ref[e]

            @pl.when(r >= 0)
            def _():
                zero_copy(pl.multiple_of(jnp.maximum(r, 0), ROW_BLOCK)).start()
            return c

        def wait(e, c):
            @pl.when(zrow_ref[e] >= 0)
            def _():
                zero_copy(0).wait()
            return c

        lax.fori_loop(0, N_EXPERTS, start, 0)
        lax.fori_loop(0, N_EXPERTS, wait, 0)

    def issue(t, c):
        for kk in range(TOP_K):
            row_copy(t, dest_ref[kk, t]).start(priority=kk % 2)
        return c

    def drain(t, c):
        for kk in range(TOP_K):
            row_copy(0, 0).wait()
        return c

    lax.fori_loop(0, tm, issue, 0)
    lax.fori_loop(0, tm, drain, 0)


def _dispatch(zrow, dest, h1, n_rows, *, tm):
    t = h1.shape[0]
    return pl.pallas_call(
        _dispatch_body,
        grid_spec=pltpu.PrefetchScalarGridSpec(
            num_scalar_prefetch=1,
            grid=(t // tm,),
            in_specs=[
                pl.BlockSpec((TOP_K, tm), lambda i, z: (0, i), memory_space=pltpu.SMEM),
                pl.BlockSpec((tm, D_MODEL), lambda i, z: (i, 0)),
            ],
            out_specs=pl.BlockSpec(memory_space=pl.ANY),
            scratch_shapes=[pltpu.VMEM((ROW_BLOCK, D_MODEL), F32),
                            pltpu.SemaphoreType.DMA(()), pltpu.SemaphoreType.DMA(())],
        ),
        out_shape=jax.ShapeDtypeStruct((n_rows, D_MODEL), F32),
        compiler_params=pltpu.CompilerParams(
            dimension_semantics=("arbitrary",), vmem_limit_bytes=VMEM_LIMIT),
        name="dispatch",
    )(zrow, dest, h1)


def _experts_body(be_ref, nu_ref, xs_ref, wg_ref, wu_ref, wd_ref, ys_ref):
    blk = pl.program_id(0)

    @pl.when(blk < nu_ref[0])
    def _():
        xb = xs_ref[...].astype(BF16)
        gate = jnp.dot(xb, wg_ref[0], preferred_element_type=F32)
        up = jnp.dot(xb, wu_ref[0], preferred_element_type=F32)
        hid = (jax.nn.silu(gate) * up).astype(BF16)
        ys_ref[...] = jnp.dot(hid, wd_ref[0], preferred_element_type=F32)

    @pl.when(blk >= nu_ref[0])
    def _():
        ys_ref[...] = jnp.zeros_like(ys_ref)


def _experts(block_e, n_used, xs, w_gate, w_up, w_down):
    n_rows = xs.shape[0]
    n_blocks = n_rows // ROW_BLOCK
    last = lambda b, be, nu: jnp.minimum(b, nu[0] - 1)
    return pl.pallas_call(
        _experts_body,
        grid_spec=pltpu.PrefetchScalarGridSpec(
            num_scalar_prefetch=2,
            grid=(n_blocks,),
            in_specs=[
                pl.BlockSpec((ROW_BLOCK, D_MODEL), lambda b, be, nu: (last(b, be, nu), 0)),
                pl.BlockSpec((1, D_MODEL, EXPERT_FF), lambda b, be, nu: (be[last(b, be, nu)], 0, 0)),
                pl.BlockSpec((1, D_MODEL, EXPERT_FF), lambda b, be, nu: (be[last(b, be, nu)], 0, 0)),
                pl.BlockSpec((1, EXPERT_FF, D_MODEL), lambda b, be, nu: (be[last(b, be, nu)], 0, 0)),
            ],
            out_specs=pl.BlockSpec((ROW_BLOCK, D_MODEL), lambda b, be, nu: (b, 0)),
        ),
        out_shape=jax.ShapeDtypeStruct((n_rows, D_MODEL), F32),
        compiler_params=pltpu.CompilerParams(
            dimension_semantics=("arbitrary",), vmem_limit_bytes=VMEM_LIMIT),
        name="experts",
    )(block_e, n_used, xs, w_gate, w_up, w_down)


def _combine_body(dest_ref, h1_ref, w_ref, p_ref, ys_ref, shg_ref, shu_ref, shd_ref,
                  pg_ref, pp_ref, g_ref, b_ref, o_ref, buf, sem):
    tm = h1_ref.shape[0]

    def row_copy(kk, t, d):
        return pltpu.make_async_copy(ys_ref.at[pl.ds(d, 1)], buf.at[kk, pl.ds(t, 1)], sem)

    def issue(t, c):
        for kk in range(TOP_K):
            row_copy(kk, t, dest_ref[kk, t]).start(priority=kk % 2)
        return c

    def drain(t, c):
        for kk in range(TOP_K):
            row_copy(kk, 0, 0).wait()
        return c

    lax.fori_loop(0, tm, issue, 0)

    h1 = h1_ref[...]
    hb = h1.astype(BF16)
    hid = (jax.nn.silu(jnp.dot(hb, shg_ref[...], preferred_element_type=F32))
           * jnp.dot(hb, shu_ref[...], preferred_element_type=F32)).astype(BF16)
    r = DEEPNORM_ALPHA * h1 + jnp.dot(hid, shd_ref[...], preferred_element_type=F32)
    emb = jnp.dot(p_ref[...].astype(BF16), pp_ref[...], preferred_element_type=F32)

    lax.fori_loop(0, tm, drain, 0)
    w = w_ref[...]
    for kk in range(TOP_K):
        r = r + w[:, kk:kk + 1] * buf[kk]
    gate = jax.nn.sigmoid(jnp.dot(r.astype(BF16), pg_ref[...], preferred_element_type=F32))
    o_ref[...] = _layer_norm(r + gate * emb, g_ref[...], b_ref[...])


def _combine(dest, h1, wtok, p2, ys, sh_g, sh_u, sh_d, ple_g, ple_p, l2g, l2b, *, tm):
    t = h1.shape[0]
    row = lambda w: pl.BlockSpec((tm, w), lambda i: (i, 0))
    return pl.pallas_call(
        _combine_body,
        grid=(t // tm,),
        in_specs=[
            pl.BlockSpec((TOP_K, tm), lambda i: (0, i), memory_space=pltpu.SMEM),
            row(D_MODEL), row(TOP_K), row(PLE_DIM),
            pl.BlockSpec(memory_space=pl.ANY),
            _const_spec(sh_g.shape), _const_spec(sh_u.shape), _const_spec(sh_d.shape),
            _const_spec(ple_g.shape), _const_spec(ple_p.shape),
            _const_spec((1, D_MODEL)), _const_spec((1, D_MODEL)),
        ],
        out_specs=row(D_MODEL),
        out_shape=jax.ShapeDtypeStruct((t, D_MODEL), F32),
        scratch_shapes=[pltpu.VMEM((TOP_K, tm, D_MODEL), F32), pltpu.SemaphoreType.DMA(())],
        compiler_params=pltpu.CompilerParams(
            dimension_semantics=("arbitrary",), vmem_limit_bytes=VMEM_LIMIT),
        name="combine",
    )(dest, h1, wtok, p2, ys, sh_g, sh_u, sh_d, ple_g, ple_p, l2g, l2b)


def _block_layout(counts, n_blocks):
    padded = (counts + ROW_BLOCK - 1) // ROW_BLOCK * ROW_BLOCK
    pad_ends = jnp.cumsum(padded)
    pad_starts = pad_ends - padded
    block_row0 = jnp.arange(n_blocks, dtype=I32) * ROW_BLOCK
    block_e = jnp.minimum(jnp.sum(pad_ends[None, :] <= block_row0[:, None], axis=1),
                          N_EXPERTS - 1).astype(I32)
    n_used = (pad_ends[-1:] // ROW_BLOCK).astype(I32)
    has_pad = (counts % ROW_BLOCK) != 0
    zrow = jnp.where(has_pad, pad_ends - ROW_BLOCK, -1).astype(I32)
    return pad_starts.astype(I32), block_e, n_used, zrow


def kernel(x, p, ln_in_g, ln_in_b, w_in, sg_ln_g, sg_ln_b, sg_w_s, sg_b_s, w_branch_a, w_branch_b,
           w_out, ln1_g, ln1_b, router_w, router_bias, moe_w_gate, moe_w_up, moe_w_down,
           sh_w_gate, sh_w_up, sh_w_down, ple_w_proj, ple_w_gate, ln2_g, ln2_b):
    bsz, seq, dm = x.shape
    t = bsz * seq
    vec = lambda a: a.reshape(1, -1).astype(F32)
    assert DEPTH == 1
    h = x.reshape(t, dm)
    for i in range(DEPTH):
        q, k, v, ma, gb = _inproj(
            h, vec(ln_in_g), vec(ln_in_b), w_in[i].astype(BF16), vec(sg_ln_g[i]), vec(sg_ln_b[i]),
            sg_w_s[i], sg_b_s[i].T, w_branch_a[i].astype(BF16), tm=256)
        yb = _attn(q.reshape(bsz, seq, SB_WIDTH), k.reshape(bsz, seq, SB_WIDTH),
                   v.reshape(bsz, seq, SB_WIDTH), tq=256).reshape(t, SB_WIDTH)

        rw_t = router_w[i].T
        rw_hi = rw_t.astype(BF16)
        rw_lo = (rw_t - rw_hi.astype(F32)).astype(BF16)
        h1, eidx, wsel, pos, cnt = _post(
            h, vec(ln_in_g), vec(ln_in_b), ma, gb, yb, w_branch_b[i].astype(BF16),
            w_out[i].astype(BF16), vec(ln1_g[i]), vec(ln1_b[i]), rw_hi, rw_lo,
            router_bias[i].reshape(N_EXPERTS, 1).astype(F32), tm=256)

        n_blocks = t * TOP_K // ROW_BLOCK + N_EXPERTS
        pad_starts, block_e, n_used, zrow = _block_layout(cnt[:, 0], n_blocks)
        dest = pos
        for e in range(N_EXPERTS):
            dest = dest + jnp.where(eidx == e, pad_starts[e], 0)

        xs = _dispatch(zrow, dest, h1, n_blocks * ROW_BLOCK, tm=256)
        ys = _experts(block_e, n_used, xs, moe_w_gate[i].astype(BF16), moe_w_up[i].astype(BF16),
                      moe_w_down[i].astype(BF16))
        h = _combine(dest, h1, wsel.T, p[i].reshape(t, PLE_DIM), ys,
                     sh_w_gate[i].astype(BF16), sh_w_up[i].astype(BF16), sh_w_down[i].astype(BF16),
                     ple_w_gate[i].astype(BF16), ple_w_proj[i].astype(BF16),
                     vec(ln2_g[i]), vec(ln2_b[i]), tm=256)
    return h.reshape(bsz, seq, dm)
```

```python
import functools
import math

import jax
import jax.numpy as jnp
from jax import lax
from jax.experimental import pallas as pl
from jax.experimental.pallas import tpu as pltpu

F32 = jnp.float32
BF16 = jnp.bfloat16
I32 = jnp.int32

D_MODEL = 1024
DEPTH = 1
CHUNK = 64
PLE_DIM = 256
SG_BLOCK = 128
SG_GROUPS = 4
SG_WIDTH = 512
SB_HEADS = 8
SB_HEAD_DIM = 64
SB_WIDTH = SB_HEADS * SB_HEAD_DIM
N_EXPERTS = 64
N_GROUPS = 8
GROUP_SIZE = N_EXPERTS // N_GROUPS
TOPK_GROUPS = 4
TOP_K = 8
EXPERT_FF = 256
SHARED_FF = 256
ROUTED_SCALE = 2.5
LN_EPS = 1e-5
DEEPNORM_ALPHA = (2.0 * DEPTH) ** 0.25

_OFF_U, _OFF_V, _OFF_Q, _OFF_K, _OFF_VS, _OFF_GA, _OFF_GB, _OFF_END = (
    0, 512, 1024, 1536, 2048, 2560, 3584, 4608)

LANES = 128
TILE_ROWS = 8
assert TILE_ROWS * LANES == D_MODEL
ATTN_LANES = 256
ATTN_HEADS = ATTN_LANES // SB_HEAD_DIM
ROW_BLOCK = 256
VMEM_LIMIT = 56 * 1024 * 1024

_NT = (((1,), (1,)), ((), ()))


def _layer_norm(x, g, b):
    mu = jnp.mean(x, axis=-1, keepdims=True)
    xc = x - mu
    var = jnp.mean(xc * xc, axis=-1, keepdims=True)
    return xc * lax.rsqrt(var + LN_EPS) * g + b


def _tiles_to_rows(ref, n, base=0):
    return jnp.concatenate(
        [ref[pl.ds(base + c, n, stride=TILE_ROWS), :] for c in range(TILE_ROWS)], axis=1)


def _rows_to_tiles(ref, val):
    n = val.shape[0]
    for c in range(TILE_ROWS):
        ref[pl.ds(c, n, stride=TILE_ROWS), :] = val[:, c * LANES:(c + 1) * LANES]


def _const_spec(shape):
    nd = len(shape)
    return pl.BlockSpec(shape, lambda *_: (0,) * nd, pipeline_mode=pl.Buffered(1))


def _inproj_body(x_ref, g_ref, b_ref, w_ref, sgg_ref, sgb_ref, ws_ref, bs_ref, wba_ref,
                 q_ref, k_ref, v_ref, ma_ref, gb_ref, ya_scr):
    tm = x_ref.shape[0]
    hb = _layer_norm(x_ref[...], g_ref[...], b_ref[...]).astype(BF16)

    def proj(lo, hi):
        return jnp.dot(hb, w_ref[:, lo:hi], preferred_element_type=F32)

    q_ref[...] = (proj(_OFF_Q, _OFF_K) * (1.0 / math.sqrt(SB_HEAD_DIM))).astype(BF16)
    k_ref[...] = proj(_OFF_K, _OFF_VS).astype(BF16)
    v_ref[...] = proj(_OFF_VS, _OFF_GA).astype(BF16)
    gb_ref[...] = jax.nn.sigmoid(proj(_OFF_GB, _OFF_END)).astype(BF16)

    yu = jax.nn.gelu(proj(_OFF_U, _OFF_V))
    vn = _layer_norm(jax.nn.gelu(proj(_OFF_V, _OFF_Q)), sgg_ref[...], sgb_ref[...]).astype(BF16)

    pi = lax.broadcasted_iota(I32, (SG_BLOCK, SG_BLOCK), 0) // CHUNK
    pj = lax.broadcasted_iota(I32, (SG_BLOCK, SG_BLOCK), 1) // CHUNK
    chunk_causal = pj <= pi
    gch = SG_WIDTH // SG_GROUPS
    for g in range(SG_GROUPS):
        wm = jnp.where(chunk_causal, ws_ref[g], 0.0).astype(BF16)
        bias = bs_ref[:, g:g + 1]
        for r in range(tm // SG_BLOCK):
            rows = slice(r * SG_BLOCK, (r + 1) * SG_BLOCK)
            cols = slice(g * gch, (g + 1) * gch)
            s = jnp.dot(wm, vn[rows, cols], preferred_element_type=F32) + bias
            ya_scr[rows, cols] = (yu[rows, cols] * s).astype(BF16)

    ba = jnp.dot(ya_scr[...], wba_ref[...], preferred_element_type=F32)
    ma_ref[...] = (jax.nn.sigmoid(proj(_OFF_GA, _OFF_GB)) * ba).astype(BF16)


def _inproj(x2, ln_g, ln_b, w_in, sg_g, sg_b, w_s, b_sT, w_ba, *, tm):
    t = x2.shape[0]
    row = lambda w: pl.BlockSpec((tm, w), lambda i: (i, 0))
    return pl.pallas_call(
        _inproj_body,
        grid=(t // tm,),
        in_specs=[
            row(D_MODEL),
            _const_spec((1, D_MODEL)), _const_spec((1, D_MODEL)),
            _const_spec(w_in.shape),
            _const_spec((1, SG_WIDTH)), _const_spec((1, SG_WIDTH)),
            _const_spec(w_s.shape), _const_spec(b_sT.shape),
            _const_spec(w_ba.shape),
        ],
        out_specs=[row(SB_WIDTH), row(SB_WIDTH), row(SB_WIDTH), row(D_MODEL), row(D_MODEL)],
        out_shape=[
            jax.ShapeDtypeStruct((t, SB_WIDTH), BF16),
            jax.ShapeDtypeStruct((t, SB_WIDTH), BF16),
            jax.ShapeDtypeStruct((t, SB_WIDTH), BF16),
            jax.ShapeDtypeStruct((t, D_MODEL), BF16),
            jax.ShapeDtypeStruct((t, D_MODEL), BF16),
        ],
        scratch_shapes=[pltpu.VMEM((tm, SG_WIDTH), BF16)],
        compiler_params=pltpu.CompilerParams(
            dimension_semantics=("arbitrary",), vmem_limit_bytes=VMEM_LIMIT),
        name="inproj",
    )(x2, ln_g, ln_b, w_in, sg_g, sg_b, w_s, b_sT, w_ba)


def _attn_body(q_ref, k_ref, v_ref, o_ref, acc_scr, car_scr, *, tq):
    qi = pl.program_id(2)
    q2 = q_ref[0]
    lane_head = lax.broadcasted_iota(I32, (tq, ATTN_LANES), 1) // SB_HEAD_DIM
    zero = jnp.zeros_like(q2)
    in_head = [lane_head == h for h in range(ATTN_HEADS)]
    qh = [jnp.where(in_head[h], q2, zero) for h in range(ATTN_HEADS)]
    rr = lax.broadcasted_iota(I32, (tq, tq), 0)
    cc = lax.broadcasted_iota(I32, (tq, tq), 1)
    later = (rr > cc).astype(BF16)
    causal = cc < rr

    def block(kb, vb, diag):
        for h in range(ATTN_HEADS):
            z = lax.dot_general(qh[h], kb, _NT, preferred_element_type=F32)
            lm = -(jnp.maximum(z, 0.0) + jnp.log(1.0 + jnp.exp(-jnp.abs(z))))
            lb = lm + z
            if diag:
                lm = jnp.where(causal, lm, 0.0)
            tail = jnp.dot(lm.astype(BF16), later, preferred_element_type=F32)
            a = jnp.exp(lb + tail + car_scr[h])
            if diag:
                a = jnp.where(causal, a, 0.0)
            acc_scr[h] += jnp.dot(a.astype(BF16), vb, preferred_element_type=F32)
            car_scr[h] += jnp.sum(lm, axis=1, keepdims=True)

    acc_scr[...] = jnp.zeros_like(acc_scr)
    car_scr[...] = jnp.zeros_like(car_scr)
    d0 = pl.multiple_of(qi * tq, tq)
    block(k_ref[0, pl.ds(d0, tq), :], v_ref[0, pl.ds(d0, tq), :], True)

    def earlier(j, c):
        k0 = pl.multiple_of((qi - 1 - j) * tq, tq)
        block(k_ref[0, pl.ds(k0, tq), :], v_ref[0, pl.ds(k0, tq), :], False)
        return c

    lax.fori_loop(0, qi, earlier, 0)
    out = acc_scr[0]
    for h in range(1, ATTN_HEADS):
        out = jnp.where(in_head[h], acc_scr[h], out)
    o_ref[0] = out.astype(BF16)


def _attn(q3, k3, v3, *, tq):
    b, s, _ = q3.shape
    qspec = pl.BlockSpec((1, tq, ATTN_LANES), lambda bi, hp, qi: (bi, qi, hp))
    kvspec = pl.BlockSpec((1, s, ATTN_LANES), lambda bi, hp, qi: (bi, 0, hp))
    return pl.pallas_call(
        functools.partial(_attn_body, tq=tq),
        grid=(b, SB_WIDTH // ATTN_LANES, s // tq),
        in_specs=[qspec, kvspec, kvspec],
        out_specs=qspec,
        out_shape=jax.ShapeDtypeStruct((b, s, SB_WIDTH), BF16),
        scratch_shapes=[pltpu.VMEM((ATTN_HEADS, tq, ATTN_LANES), F32),
                        pltpu.VMEM((ATTN_HEADS, tq, 1), F32)],
        compiler_params=pltpu.CompilerParams(
            dimension_semantics=("arbitrary", "arbitrary", "arbitrary"),
            vmem_limit_bytes=VMEM_LIMIT),
        name="attn",
    )(q3, k3, v3)


def _post_body(x_ref, g_ref, b_ref, ma_ref, gb_ref, yb_ref, wbb_ref, wout_ref, l1g_ref, l1b_ref,
               rwh_ref, rwl_ref, rb_ref,
               h1_ref, eidx_ref, wsel_ref, pos_ref, cnt_ref, cnt_scr):
    tm = x_ref.shape[0]
    step = pl.program_id(0)

    @pl.when(step == 0)
    def _():
        cnt_scr[...] = jnp.zeros_like(cnt_scr)

    h = _layer_norm(x_ref[...], g_ref[...], b_ref[...])
    bb = jnp.dot(yb_ref[...], wbb_ref[...], preferred_element_type=F32)
    merged = ma_ref[...].astype(F32) + gb_ref[...].astype(F32) * bb
    o = jnp.dot(merged.astype(BF16), wout_ref[...], preferred_element_type=F32)
    h1 = _layer_norm(DEEPNORM_ALPHA * h + o, l1g_ref[...], l1b_ref[...])
    _rows_to_tiles(h1_ref, h1)

    hh = h1.astype(BF16)
    hl = (h1 - hh.astype(F32)).astype(BF16)
    logits = (lax.dot_general(rwh_ref[...], hh, _NT, preferred_element_type=F32)
              + lax.dot_general(rwh_ref[...], hl, _NT, preferred_element_type=F32)
              + lax.dot_general(rwl_ref[...], hh, _NT, preferred_element_type=F32))
    scores = jax.nn.sigmoid(logits)
    sel = scores + rb_ref[...]

    sub = lax.broadcasted_iota(I32, (GROUP_SIZE, tm), 0).astype(F32)
    neg = jnp.float32(-jnp.inf)
    selg = [sel[g * GROUP_SIZE:(g + 1) * GROUP_SIZE] for g in range(N_GROUPS)]
    scg = [scores[g * GROUP_SIZE:(g + 1) * GROUP_SIZE] for g in range(N_GROUPS)]
    eid = [sub + float(g * GROUP_SIZE) for g in range(N_GROUPS)]

    grp = []
    for g in range(N_GROUPS):
        m1 = jnp.max(selg[g], axis=0, keepdims=True)
        i1 = jnp.min(jnp.where(selg[g] == m1, sub, float(GROUP_SIZE)), axis=0, keepdims=True)
        m2 = jnp.max(jnp.where(sub == i1, neg, selg[g]), axis=0, keepdims=True)
        grp.append(m1 + m2)
    cur = []
    for g in range(N_GROUPS):
        beaten = jnp.zeros((1, tm), F32)
        for o_g in range(N_GROUPS):
            if o_g == g:
                continue
            wins = (grp[o_g] > grp[g]) if o_g > g else (grp[o_g] >= grp[g])
            beaten = beaten + jnp.where(wins, 1.0, 0.0)
        cur.append(jnp.where(beaten < float(TOPK_GROUPS), selg[g], neg))

    def all_reduce(vals, op):
        acc = vals[0]
        for v in vals[1:]:
            acc = op(acc, v)
        return acc

    idx_rows, w_rows = [], []
    member = [jnp.zeros((GROUP_SIZE, tm), F32) for _ in range(N_GROUPS)]
    for _ in range(TOP_K):
        m = jnp.max(all_reduce(cur, jnp.maximum), axis=0, keepdims=True)
        cand = [jnp.where(cur[g] == m, eid[g], float(N_EXPERTS)) for g in range(N_GROUPS)]
        idx = jnp.min(all_reduce(cand, jnp.minimum), axis=0, keepdims=True)
        hit = [eid[g] == idx for g in range(N_GROUPS)]
        w = jnp.sum(all_reduce([jnp.where(hit[g], scg[g], 0.0) for g in range(N_GROUPS)], jnp.add),
                    axis=0, keepdims=True)
        cur = [jnp.where(hit[g], neg, cur[g]) for g in range(N_GROUPS)]
        member = [jnp.where(hit[g], 1.0, member[g]) for g in range(N_GROUPS)]
        idx_rows.append(idx)
        w_rows.append(w)

    wsum = all_reduce(w_rows, jnp.add)
    for kk in range(TOP_K):
        eidx_ref[kk:kk + 1, :] = idx_rows[kk].astype(I32)
        wsel_ref[kk:kk + 1, :] = w_rows[kk] / wsum * ROUTED_SCALE

    tt = lax.broadcasted_iota(I32, (tm, tm), 0)
    uu = lax.broadcasted_iota(I32, (tm, tm), 1)
    before = (tt < uu).astype(BF16)
    memb = jnp.concatenate(member, axis=0)
    prefix = jnp.dot(memb.astype(BF16), before, preferred_element_type=F32) + cnt_scr[...]
    for kk in range(TOP_K):
        picked = [jnp.where(eid[g] == idx_rows[kk],
                            prefix[g * GROUP_SIZE:(g + 1) * GROUP_SIZE], 0.0) for g in range(N_GROUPS)]
        pos_ref[kk:kk + 1, :] = jnp.sum(all_reduce(picked, jnp.add), axis=0, keepdims=True).astype(I32)
    total = cnt_scr[...] + jnp.sum(memb, axis=1, keepdims=True)
    cnt_scr[...] = total
    cnt_ref[...] = jnp.broadcast_to(total, cnt_ref.shape).astype(I32)


def _post(x2, ln_g, ln_b, ma, gb, yb, w_bb, w_out, l1g, l1b, rw_hi, rw_lo, r_bias, *, tm):
    t = x2.shape[0]
    row = lambda w: pl.BlockSpec((tm, w), lambda i: (i, 0))
    tok = pl.BlockSpec((TOP_K, tm), lambda i: (0, i))
    return pl.pallas_call(
        _post_body,
        grid=(t // tm,),
        in_specs=[
            row(D_MODEL), _const_spec((1, D_MODEL)), _const_spec((1, D_MODEL)),
            row(D_MODEL), row(D_MODEL), row(SB_WIDTH),
            _const_spec(w_bb.shape), _const_spec(w_out.shape),
            _const_spec((1, D_MODEL)), _const_spec((1, D_MODEL)),
            _const_spec(rw_hi.shape), _const_spec(rw_lo.shape), _const_spec(r_bias.shape),
        ],
        out_specs=[pl.BlockSpec((tm * TILE_ROWS, LANES), lambda i: (i, 0)), tok, tok, tok,
                   pl.BlockSpec((N_EXPERTS, LANES), lambda i: (0, 0))],
        out_shape=[
            jax.ShapeDtypeStruct((t * TILE_ROWS, LANES), F32),
            jax.ShapeDtypeStruct((TOP_K, t), I32),
            jax.ShapeDtypeStruct((TOP_K, t), F32),
            jax.ShapeDtypeStruct((TOP_K, t), I32),
            jax.ShapeDtypeStruct((N_EXPERTS, LANES), I32),
        ],
        scratch_shapes=[pltpu.VMEM((N_EXPERTS, 1), F32)],
        compiler_params=pltpu.CompilerParams(
            dimension_semantics=("arbitrary",), vmem_limit_bytes=VMEM_LIMIT),
        name="post",
    )(x2, ln_g, ln_b, ma, gb, yb, w_bb, w_out, l1g, l1b, rw_hi, rw_lo, r_bias)


def _tile_of(r):
    return pl.ds(pl.multiple_of(r * TILE_ROWS, TILE_ROWS), TILE_ROWS)


def _dispatch_body(zrow_ref, dest_ref, h_ref, xs_ref, zero_scr, zsem, sem):
    tm = h_ref.shape[0] // TILE_ROWS
    step = pl.program_id(0)

    def zero_copy(r):
        rows = pl.ds(pl.multiple_of(r * TILE_ROWS, ROW_BLOCK * TILE_ROWS), ROW_BLOCK * TILE_ROWS)
        return pltpu.make_async_copy(zero_scr, xs_ref.at[rows], zsem)

    def row_copy(t, d):
        return pltpu.make_async_copy(h_ref.at[_tile_of(t)], xs_ref.at[_tile_of(d)], sem)

    @pl.when(step == 0)
    def _():
        zero_scr[...] = jnp.zeros_like(zero_scr)

        def start(e, c):
            r = zrow_ref[e]

            @pl.when(r >= 0)
            def _():
                zero_copy(jnp.maximum(r, 0)).start()
            return c

        def wait(e, c):
            @pl.when(zrow_ref[e] >= 0)
            def _():
                zero_copy(0).wait()
            return c

        lax.fori_loop(0, N_EXPERTS, start, 0)
        lax.fori_loop(0, N_EXPERTS, wait, 0)

    def issue(t, c):
        for kk in range(TOP_K):
            row_copy(t, dest_ref[t * TOP_K + kk]).start(priority=kk % 2)
        return c

    def drain(t, c):
        for kk in range(TOP_K):
            row_copy(0, 0).wait()
        return c

    lax.fori_loop(0, tm, issue, 0)
    lax.fori_loop(0, tm, drain, 0)


def _dispatch(zrow, dest_flat, h1t, n_rows, *, tm):
    t = h1t.shape[0] // TILE_ROWS
    return pl.pallas_call(
        _dispatch_body,
        grid_spec=pltpu.PrefetchScalarGridSpec(
            num_scalar_prefetch=1,
            grid=(t // tm,),
            in_specs=[
                pl.BlockSpec((tm * TOP_K,), lambda i, z: (i,), memory_space=pltpu.SMEM),
                pl.BlockSpec((tm * TILE_ROWS, LANES), lambda i, z: (i, 0)),
            ],
            out_specs=pl.BlockSpec(memory_space=pl.ANY),
            scratch_shapes=[pltpu.VMEM((ROW_BLOCK * TILE_ROWS, LANES), F32),
                            pltpu.SemaphoreType.DMA(()), pltpu.SemaphoreType.DMA(())],
        ),
        out_shape=jax.ShapeDtypeStruct((n_rows * TILE_ROWS, LANES), F32),
        compiler_params=pltpu.CompilerParams(
            dimension_semantics=("arbitrary",), vmem_limit_bytes=VMEM_LIMIT),
        name="dispatch",
    )(zrow, dest_flat, h1t)


def _experts_body(be_ref, nu_ref, xs_ref, wg_ref, wu_ref, wd_ref, ys_ref):
    blk = pl.program_id(0)

    @pl.when(blk < nu_ref[0])
    def _():
        xb = _tiles_to_rows(xs_ref, ROW_BLOCK).astype(BF16)
        gate = jnp.dot(xb, wg_ref[0], preferred_element_type=F32)
        up = jnp.dot(xb, wu_ref[0], preferred_element_type=F32)
        hid = (jax.nn.silu(gate) * up).astype(BF16)
        _rows_to_tiles(ys_ref, jnp.dot(hid, wd_ref[0], preferred_element_type=F32))

    @pl.when(blk >= nu_ref[0])
    def _():
        ys_ref[...] = jnp.zeros_like(ys_ref)


def _experts(block_e, n_used, xs, w_gate, w_up, w_down):
    n_blocks = xs.shape[0] // (ROW_BLOCK * TILE_ROWS)
    last = lambda b, be, nu: jnp.minimum(b, nu[0] - 1)
    blk = (ROW_BLOCK * TILE_ROWS, LANES)
    return pl.pallas_call(
        _experts_body,
        grid_spec=pltpu.PrefetchScalarGridSpec(
            num_scalar_prefetch=2,
            grid=(n_blocks,),
            in_specs=[
                pl.BlockSpec(blk, lambda b, be, nu: (last(b, be, nu), 0)),
                pl.BlockSpec((1, D_MODEL, EXPERT_FF), lambda b, be, nu: (be[last(b, be, nu)], 0, 0)),
                pl.BlockSpec((1, D_MODEL, EXPERT_FF), lambda b, be, nu: (be[last(b, be, nu)], 0, 0)),
                pl.BlockSpec((1, EXPERT_FF, D_MODEL), lambda b, be, nu: (be[last(b, be, nu)], 0, 0)),
            ],
            out_specs=pl.BlockSpec(blk, lambda b, be, nu: (b, 0)),
        ),
        out_shape=jax.ShapeDtypeStruct(xs.shape, F32),
        compiler_params=pltpu.CompilerParams(
            dimension_semantics=("arbitrary",), vmem_limit_bytes=VMEM_LIMIT),
        name="experts",
    )(block_e, n_used, xs, w_gate, w_up, w_down)


def _combine_body(dest_ref, h1_ref, w_ref, p_ref, ys_ref, shg_ref, shu_ref, shd_ref,
                  pg_ref, pp_ref, g_ref, b_ref, o_ref, buf, sem):
    tm = h1_ref.shape[0] // TILE_ROWS

    def row_copy(kk, t, d):
        return pltpu.make_async_copy(ys_ref.at[_tile_of(d)], buf.at[_tile_of(kk * tm + t)], sem)

    def issue(t, c):
        for kk in range(TOP_K):
            row_copy(kk, t, dest_ref[t * TOP_K + kk]).start(priority=kk % 2)
        return c

    def drain(t, c):
        for kk in range(TOP_K):
            row_copy(kk, 0, 0).wait()
        return c

    lax.fori_loop(0, tm, issue, 0)

    h1 = _tiles_to_rows(h1_ref, tm)
    hb = h1.astype(BF16)
    hid = (jax.nn.silu(jnp.dot(hb, shg_ref[...], preferred_element_type=F32))
           * jnp.dot(hb, shu_ref[...], preferred_element_type=F32)).astype(BF16)
    r = DEEPNORM_ALPHA * h1 + jnp.dot(hid, shd_ref[...], preferred_element_type=F32)
    emb = jnp.dot(p_ref[...].astype(BF16), pp_ref[...], preferred_element_type=F32)

    lax.fori_loop(0, tm, drain, 0)
    w = w_ref[...]
    for kk in range(TOP_K):
        r = r + w[:, kk:kk + 1] * _tiles_to_rows(buf, tm, base=kk * tm * TILE_ROWS)
    gate = jax.nn.sigmoid(jnp.dot(r.astype(BF16), pg_ref[...], preferred_element_type=F32))
    o_ref[...] = _layer_norm(r + gate * emb, g_ref[...], b_ref[...])


def _combine(dest_flat, h1t, wtok, p2, ys, sh_g, sh_u, sh_d, ple_g, ple_p, l2g, l2b, *, tm):
    t = h1t.shape[0] // TILE_ROWS
    row = lambda w: pl.BlockSpec((tm, w), lambda i: (i, 0))
    return pl.pallas_call(
        _combine_body,
        grid=(t // tm,),
        in_specs=[
            pl.BlockSpec((tm * TOP_K,), lambda i: (i,), memory_space=pltpu.SMEM),
            pl.BlockSpec((tm * TILE_ROWS, LANES), lambda i: (i, 0)), row(TOP_K), row(PLE_DIM),
            pl.BlockSpec(memory_space=pl.ANY),
            _const_spec(sh_g.shape), _const_spec(sh_u.shape), _const_spec(sh_d.shape),
            _const_spec(ple_g.shape), _const_spec(ple_p.shape),
            _const_spec((1, D_MODEL)), _const_spec((1, D_MODEL)),
        ],
        out_specs=row(D_MODEL),
        out_shape=jax.ShapeDtypeStruct((t, D_MODEL), F32),
        scratch_shapes=[pltpu.VMEM((TOP_K * tm * TILE_ROWS, LANES), F32),
                        pltpu.SemaphoreType.DMA(())],
        compiler_params=pltpu.CompilerParams(
            dimension_semantics=("arbitrary",), vmem_limit_bytes=VMEM_LIMIT),
        name="combine",
    )(dest_flat, h1t, wtok, p2, ys, sh_g, sh_u, sh_d, ple_g, ple_p, l2g, l2b)


def _block_layout(counts, n_blocks):
    padded = (counts + ROW_BLOCK - 1) // ROW_BLOCK * ROW_BLOCK
    pad_ends = jnp.cumsum(padded)
    pad_starts = pad_ends - padded
    block_row0 = jnp.arange(n_blocks, dtype=I32) * ROW_BLOCK
    block_e = jnp.minimum(jnp.sum(pad_ends[None, :] <= block_row0[:, None], axis=1),
                          N_EXPERTS - 1).astype(I32)
    n_used = (pad_ends[-1:] // ROW_BLOCK).astype(I32)
    has_pad = (counts % ROW_BLOCK) != 0
    zrow = jnp.where(has_pad, pad_ends - ROW_BLOCK, -1).astype(I32)
    return pad_starts.astype(I32), block_e, n_used, zrow


def kernel(x, p, ln_in_g, ln_in_b, w_in, sg_ln_g, sg_ln_b, sg_w_s, sg_b_s, w_branch_a, w_branch_b,
           w_out, ln1_g, ln1_b, router_w, router_bias, moe_w_gate, moe_w_up, moe_w_down,
           sh_w_gate, sh_w_up, sh_w_down, ple_w_proj, ple_w_gate, ln2_g, ln2_b):
    bsz, seq, dm = x.shape
    t = bsz * seq
    vec = lambda a: a.reshape(1, -1).astype(F32)
    assert DEPTH == 1
    h = x.reshape(t, dm)
    for i in range(DEPTH):
        q, k, v, ma, gb = _inproj(
            h, vec(ln_in_g), vec(ln_in_b), w_in[i].astype(BF16), vec(sg_ln_g[i]), vec(sg_ln_b[i]),
            sg_w_s[i], sg_b_s[i].T, w_branch_a[i].astype(BF16), tm=256)
        yb = _attn(q.reshape(bsz, seq, SB_WIDTH), k.reshape(bsz, seq, SB_WIDTH),
                   v.reshape(bsz, seq, SB_WIDTH), tq=256).reshape(t, SB_WIDTH)

        rw_t = router_w[i].T
        rw_hi = rw_t.astype(BF16)
        rw_lo = (rw_t - rw_hi.astype(F32)).astype(BF16)
        h1, eidx, wsel, pos, cnt = _post(
            h, vec(ln_in_g), vec(ln_in_b), ma, gb, yb, w_branch_b[i].astype(BF16),
            w_out[i].astype(BF16), vec(ln1_g[i]), vec(ln1_b[i]), rw_hi, rw_lo,
            router_bias[i].reshape(N_EXPERTS, 1).astype(F32), tm=256)

        n_blocks = t * TOP_K // ROW_BLOCK + N_EXPERTS
        pad_starts, block_e, n_used, zrow = _block_layout(cnt[:, 0], n_blocks)
        dest = pos
        for e in range(N_EXPERTS):
            dest = dest + jnp.where(eidx == e, pad_starts[e], 0)
        dest = dest.T.reshape(t * TOP_K)

        xs = _dispatch(zrow, dest, h1, n_blocks * ROW_BLOCK, tm=256)
        ys = _experts(block_e, n_used, xs, moe_w_gate[i].astype(BF16), moe_w_up[i].astype(BF16),
                      moe_w_down[i].astype(BF16))
        h = _combine(dest, h1, wsel.T, p[i].reshape(t, PLE_DIM), ys,
                     sh_w_gate[i].astype(BF16), sh_w_up[i].astype(BF16), sh_w_down[i].astype(BF16),
                     ple_w_gate[i].astype(BF16), ple_w_proj[i].astype(BF16),
                     vec(ln2_g[i]), vec(ln2_b[i]), tm=256)
    return h.reshape(bsz, seq, dm)
```

```python
import functools
import math

import jax
import jax.numpy as jnp
from jax import lax
from jax.experimental import pallas as pl
from jax.experimental.pallas import tpu as pltpu

F32 = jnp.float32
BF16 = jnp.bfloat16
I32 = jnp.int32

D_MODEL = 1024
DEPTH = 1
CHUNK = 64
PLE_DIM = 256
SG_BLOCK = 128
SG_GROUPS = 4
SG_WIDTH = 512
SB_HEADS = 8
SB_HEAD_DIM = 64
SB_WIDTH = SB_HEADS * SB_HEAD_DIM
N_EXPERTS = 64
N_GROUPS = 8
GROUP_SIZE = N_EXPERTS // N_GROUPS
TOPK_GROUPS = 4
TOP_K = 8
EXPERT_FF = 256
SHARED_FF = 256
ROUTED_SCALE = 2.5
LN_EPS = 1e-5
DEEPNORM_ALPHA = (2.0 * DEPTH) ** 0.25

_OFF_U, _OFF_V, _OFF_Q, _OFF_K, _OFF_VS, _OFF_GA, _OFF_GB, _OFF_END = (
    0, 512, 1024, 1536, 2048, 2560, 3584, 4608)

LANES = 128
TILE_ROWS = 8
assert TILE_ROWS * LANES == D_MODEL
ATTN_LANES = 256
ATTN_HEADS = ATTN_LANES // SB_HEAD_DIM
ROW_BLOCK = 512
VMEM_LIMIT = 56 * 1024 * 1024

_NT = (((1,), (1,)), ((), ()))


def _layer_norm(x, g, b):
    mu = jnp.mean(x, axis=-1, keepdims=True)
    xc = x - mu
    var = jnp.mean(xc * xc, axis=-1, keepdims=True)
    return xc * lax.rsqrt(var + LN_EPS) * g + b


def _tiles_to_rows(ref, n, base=0):
    return jnp.concatenate(
        [ref[pl.ds(base + c, n, stride=TILE_ROWS), :] for c in range(TILE_ROWS)], axis=1)


def _rows_to_tiles(ref, val):
    n = val.shape[0]
    for c in range(TILE_ROWS):
        ref[pl.ds(c, n, stride=TILE_ROWS), :] = val[:, c * LANES:(c + 1) * LANES]


def _const_spec(shape):
    nd = len(shape)
    return pl.BlockSpec(shape, lambda *_: (0,) * nd, pipeline_mode=pl.Buffered(1))


def _inproj_body(x_ref, g_ref, b_ref, w_ref, sgg_ref, sgb_ref, ws_ref, bs_ref, wba_ref,
                 q_ref, k_ref, v_ref, ma_ref, gb_ref, ya_scr):
    tm = x_ref.shape[0]
    hb = _layer_norm(x_ref[...], g_ref[...], b_ref[...]).astype(BF16)

    def proj(lo, hi):
        return jnp.dot(hb, w_ref[:, lo:hi], preferred_element_type=F32)

    q_ref[...] = (proj(_OFF_Q, _OFF_K) * (1.0 / math.sqrt(SB_HEAD_DIM))).astype(BF16)
    k_ref[...] = proj(_OFF_K, _OFF_VS).astype(BF16)
    v_ref[...] = proj(_OFF_VS, _OFF_GA).astype(BF16)
    gb_ref[...] = jax.nn.sigmoid(proj(_OFF_GB, _OFF_END)).astype(BF16)

    yu = jax.nn.gelu(proj(_OFF_U, _OFF_V))
    vn = _layer_norm(jax.nn.gelu(proj(_OFF_V, _OFF_Q)), sgg_ref[...], sgb_ref[...]).astype(BF16)

    pi = lax.broadcasted_iota(I32, (SG_BLOCK, SG_BLOCK), 0) // CHUNK
    pj = lax.broadcasted_iota(I32, (SG_BLOCK, SG_BLOCK), 1) // CHUNK
    chunk_causal = pj <= pi
    gch = SG_WIDTH // SG_GROUPS
    for g in range(SG_GROUPS):
        wm = jnp.where(chunk_causal, ws_ref[g], 0.0).astype(BF16)
        bias = bs_ref[:, g:g + 1]
        for r in range(tm // SG_BLOCK):
            rows = slice(r * SG_BLOCK, (r + 1) * SG_BLOCK)
            cols = slice(g * gch, (g + 1) * gch)
            s = jnp.dot(wm, vn[rows, cols], preferred_element_type=F32) + bias
            ya_scr[rows, cols] = (yu[rows, cols] * s).astype(BF16)

    ba = jnp.dot(ya_scr[...], wba_ref[...], preferred_element_type=F32)
    ma_ref[...] = (jax.nn.sigmoid(proj(_OFF_GA, _OFF_GB)) * ba).astype(BF16)


def _inproj(x2, ln_g, ln_b, w_in, sg_g, sg_b, w_s, b_sT, w_ba, *, tm):
    t = x2.shape[0]
    row = lambda w: pl.BlockSpec((tm, w), lambda i: (i, 0))
    return pl.pallas_call(
        _inproj_body,
        grid=(t // tm,),
        in_specs=[
            row(D_MODEL),
            _const_spec((1, D_MODEL)), _const_spec((1, D_MODEL)),
            _const_spec(w_in.shape),
            _const_spec((1, SG_WIDTH)), _const_spec((1, SG_WIDTH)),
            _const_spec(w_s.shape), _const_spec(b_sT.shape),
            _const_spec(w_ba.shape),
        ],
        out_specs=[row(SB_WIDTH), row(SB_WIDTH), row(SB_WIDTH), row(D_MODEL), row(D_MODEL)],
        out_shape=[
            jax.ShapeDtypeStruct((t, SB_WIDTH), BF16),
            jax.ShapeDtypeStruct((t, SB_WIDTH), BF16),
            jax.ShapeDtypeStruct((t, SB_WIDTH), BF16),
            jax.ShapeDtypeStruct((t, D_MODEL), BF16),
            jax.ShapeDtypeStruct((t, D_MODEL), BF16),
        ],
        scratch_shapes=[pltpu.VMEM((tm, SG_WIDTH), BF16)],
        compiler_params=pltpu.CompilerParams(
            dimension_semantics=("arbitrary",), vmem_limit_bytes=VMEM_LIMIT),
        name="inproj",
    )(x2, ln_g, ln_b, w_in, sg_g, sg_b, w_s, b_sT, w_ba)


def _attn_body(q_ref, k_ref, v_ref, o_ref, acc_scr, car_scr, *, tq):
    qi = pl.program_id(2)
    q2 = q_ref[0]
    lane_head = lax.broadcasted_iota(I32, (tq, ATTN_LANES), 1) // SB_HEAD_DIM
    zero = jnp.zeros_like(q2)
    in_head = [lane_head == h for h in range(ATTN_HEADS)]
    qh = [jnp.where(in_head[h], q2, zero) for h in range(ATTN_HEADS)]
    rr = lax.broadcasted_iota(I32, (tq, tq), 0)
    cc = lax.broadcasted_iota(I32, (tq, tq), 1)
    later = (rr > cc).astype(BF16)
    causal = cc < rr

    def block(kb, vb, diag):
        for h in range(ATTN_HEADS):
            z = lax.dot_general(qh[h], kb, _NT, preferred_element_type=F32)
            lm = -(jnp.maximum(z, 0.0) + jnp.log(1.0 + jnp.exp(-jnp.abs(z))))
            lb = lm + z
            if diag:
                lm = jnp.where(causal, lm, 0.0)
            tail = jnp.dot(lm.astype(BF16), later, preferred_element_type=F32)
            a = jnp.exp(lb + tail + car_scr[h])
            if diag:
                a = jnp.where(causal, a, 0.0)
            acc_scr[h] += jnp.dot(a.astype(BF16), vb, preferred_element_type=F32)
            car_scr[h] += jnp.sum(lm, axis=1, keepdims=True)

    acc_scr[...] = jnp.zeros_like(acc_scr)
    car_scr[...] = jnp.zeros_like(car_scr)
    d0 = pl.multiple_of(qi * tq, tq)
    block(k_ref[0, pl.ds(d0, tq), :], v_ref[0, pl.ds(d0, tq), :], True)

    def earlier(j, c):
        k0 = pl.multiple_of((qi - 1 - j) * tq, tq)
        block(k_ref[0, pl.ds(k0, tq), :], v_ref[0, pl.ds(k0, tq), :], False)
        return c

    lax.fori_loop(0, qi, earlier, 0)
    out = acc_scr[0]
    for h in range(1, ATTN_HEADS):
        out = jnp.where(in_head[h], acc_scr[h], out)
    o_ref[0] = out.astype(BF16)


def _attn(q3, k3, v3, *, tq):
    b, s, _ = q3.shape
    qspec = pl.BlockSpec((1, tq, ATTN_LANES), lambda bi, hp, qi: (bi, qi, hp))
    kvspec = pl.BlockSpec((1, s, ATTN_LANES), lambda bi, hp, qi: (bi, 0, hp))
    return pl.pallas_call(
        functools.partial(_attn_body, tq=tq),
        grid=(b, SB_WIDTH // ATTN_LANES, s // tq),
        in_specs=[qspec, kvspec, kvspec],
        out_specs=qspec,
        out_shape=jax.ShapeDtypeStruct((b, s, SB_WIDTH), BF16),
        scratch_shapes=[pltpu.VMEM((ATTN_HEADS, tq, ATTN_LANES), F32),
                        pltpu.VMEM((ATTN_HEADS, tq, 1), F32)],
        compiler_params=pltpu.CompilerParams(
            dimension_semantics=("arbitrary", "arbitrary", "arbitrary"),
            vmem_limit_bytes=VMEM_LIMIT),
        name="attn",
    )(q3, k3, v3)


def _post_body(x_ref, g_ref, b_ref, ma_ref, gb_ref, yb_ref, wbb_ref, wout_ref, l1g_ref, l1b_ref,
               rwh_ref, rwl_ref, rb_ref,
               h1_ref, eidx_ref, wsel_ref, pos_ref, cnt_ref, cnt_scr):
    tm = x_ref.shape[0]
    step = pl.program_id(0)

    @pl.when(step == 0)
    def _():
        cnt_scr[...] = jnp.zeros_like(cnt_scr)

    h = _layer_norm(x_ref[...], g_ref[...], b_ref[...])
    bb = jnp.dot(yb_ref[...], wbb_ref[...], preferred_element_type=F32)
    merged = ma_ref[...].astype(F32) + gb_ref[...].astype(F32) * bb
    o = jnp.dot(merged.astype(BF16), wout_ref[...], preferred_element_type=F32)
    h1 = _layer_norm(DEEPNORM_ALPHA * h + o, l1g_ref[...], l1b_ref[...])
    _rows_to_tiles(h1_ref, h1)

    hh = h1.astype(BF16)
    hl = (h1 - hh.astype(F32)).astype(BF16)
    logits = (lax.dot_general(rwh_ref[...], hh, _NT, preferred_element_type=F32)
              + lax.dot_general(rwh_ref[...], hl, _NT, preferred_element_type=F32)
              + lax.dot_general(rwl_ref[...], hh, _NT, preferred_element_type=F32))
    scores = jax.nn.sigmoid(logits)
    sel = scores + rb_ref[...]

    sub = lax.broadcasted_iota(I32, (GROUP_SIZE, tm), 0).astype(F32)
    neg = jnp.float32(-jnp.inf)
    selg = [sel[g * GROUP_SIZE:(g + 1) * GROUP_SIZE] for g in range(N_GROUPS)]
    scg = [scores[g * GROUP_SIZE:(g + 1) * GROUP_SIZE] for g in range(N_GROUPS)]
    eid = [sub + float(g * GROUP_SIZE) for g in range(N_GROUPS)]

    grp = []
    for g in range(N_GROUPS):
        m1 = jnp.max(selg[g], axis=0, keepdims=True)
        i1 = jnp.min(jnp.where(selg[g] == m1, sub, float(GROUP_SIZE)), axis=0, keepdims=True)
        m2 = jnp.max(jnp.where(sub == i1, neg, selg[g]), axis=0, keepdims=True)
        grp.append(m1 + m2)
    cur = []
    for g in range(N_GROUPS):
        beaten = jnp.zeros((1, tm), F32)
        for o_g in range(N_GROUPS):
            if o_g == g:
                continue
            wins = (grp[o_g] > grp[g]) if o_g > g else (grp[o_g] >= grp[g])
            beaten = beaten + jnp.where(wins, 1.0, 0.0)
        cur.append(jnp.where(beaten < float(TOPK_GROUPS), selg[g], neg))

    def all_reduce(vals, op):
        acc = vals[0]
        for v in vals[1:]:
            acc = op(acc, v)
        return acc

    idx_rows, w_rows = [], []
    member = [jnp.zeros((GROUP_SIZE, tm), F32) for _ in range(N_GROUPS)]
    for _ in range(TOP_K):
        m = jnp.max(all_reduce(cur, jnp.maximum), axis=0, keepdims=True)
        cand = [jnp.where(cur[g] == m, eid[g], float(N_EXPERTS)) for g in range(N_GROUPS)]
        idx = jnp.min(all_reduce(cand, jnp.minimum), axis=0, keepdims=True)
        hit = [eid[g] == idx for g in range(N_GROUPS)]
        w = jnp.sum(all_reduce([jnp.where(hit[g], scg[g], 0.0) for g in range(N_GROUPS)], jnp.add),
                    axis=0, keepdims=True)
        cur = [jnp.where(hit[g], neg, cur[g]) for g in range(N_GROUPS)]
        member = [jnp.where(hit[g], 1.0, member[g]) for g in range(N_GROUPS)]
        idx_rows.append(idx)
        w_rows.append(w)

    wsum = all_reduce(w_rows, jnp.add)
    for kk in range(TOP_K):
        eidx_ref[kk:kk + 1, :] = idx_rows[kk].astype(I32)
        wsel_ref[kk:kk + 1, :] = w_rows[kk] / wsum * ROUTED_SCALE

    tt = lax.broadcasted_iota(I32, (tm, tm), 0)
    uu = lax.broadcasted_iota(I32, (tm, tm), 1)
    before = (tt < uu).astype(BF16)
    memb = jnp.concatenate(member, axis=0)
    prefix = jnp.dot(memb.astype(BF16), before, preferred_element_type=F32) + cnt_scr[...]
    for kk in range(TOP_K):
        picked = [jnp.where(eid[g] == idx_rows[kk],
                            prefix[g * GROUP_SIZE:(g + 1) * GROUP_SIZE], 0.0) for g in range(N_GROUPS)]
        pos_ref[kk:kk + 1, :] = jnp.sum(all_reduce(picked, jnp.add), axis=0, keepdims=True).astype(I32)
    total = cnt_scr[...] + jnp.sum(memb, axis=1, keepdims=True)
    cnt_scr[...] = total
    cnt_ref[...] = jnp.broadcast_to(total, cnt_ref.shape).astype(I32)


def _post(x2, ln_g, ln_b, ma, gb, yb, w_bb, w_out, l1g, l1b, rw_hi, rw_lo, r_bias, *, tm):
    t = x2.shape[0]
    row = lambda w: pl.BlockSpec((tm, w), lambda i: (i, 0))
    tok = pl.BlockSpec((TOP_K, tm), lambda i: (0, i))
    return pl.pallas_call(
        _post_body,
        grid=(t // tm,),
        in_specs=[
            row(D_MODEL), _const_spec((1, D_MODEL)), _const_spec((1, D_MODEL)),
            row(D_MODEL), row(D_MODEL), row(SB_WIDTH),
            _const_spec(w_bb.shape), _const_spec(w_out.shape),
            _const_spec((1, D_MODEL)), _const_spec((1, D_MODEL)),
            _const_spec(rw_hi.shape), _const_spec(rw_lo.shape), _const_spec(r_bias.shape),
        ],
        out_specs=[pl.BlockSpec((tm * TILE_ROWS, LANES), lambda i: (i, 0)), tok, tok, tok,
                   pl.BlockSpec((N_EXPERTS, LANES), lambda i: (0, 0))],
        out_shape=[
            jax.ShapeDtypeStruct((t * TILE_ROWS, LANES), F32),
            jax.ShapeDtypeStruct((TOP_K, t), I32),
            jax.ShapeDtypeStruct((TOP_K, t), F32),
            jax.ShapeDtypeStruct((TOP_K, t), I32),
            jax.ShapeDtypeStruct((N_EXPERTS, LANES), I32),
        ],
        scratch_shapes=[pltpu.VMEM((N_EXPERTS, 1), F32)],
        compiler_params=pltpu.CompilerParams(
            dimension_semantics=("arbitrary",), vmem_limit_bytes=VMEM_LIMIT),
        name="post",
    )(x2, ln_g, ln_b, ma, gb, yb, w_bb, w_out, l1g, l1b, rw_hi, rw_lo, r_bias)


def _tile_of(r):
    return pl.ds(pl.multiple_of(r * TILE_ROWS, TILE_ROWS), TILE_ROWS)


def _dispatch_body(zrow_ref, dest_ref, h_ref, xs_ref, zero_scr, zsem, sem):
    tm = h_ref.shape[0] // TILE_ROWS
    step = pl.program_id(0)

    def zero_copy(r):
        rows = pl.ds(pl.multiple_of(r * TILE_ROWS, ROW_BLOCK * TILE_ROWS), ROW_BLOCK * TILE_ROWS)
        return pltpu.make_async_copy(zero_scr, xs_ref.at[rows], zsem)

    def row_copy(t, d):
        return pltpu.make_async_copy(h_ref.at[_tile_of(t)], xs_ref.at[_tile_of(d)], sem)

    @pl.when(step == 0)
    def _():
        zero_scr[...] = jnp.zeros_like(zero_scr)

        def start(e, c):
            r = zrow_ref[e]

            @pl.when(r >= 0)
            def _():
                zero_copy(jnp.maximum(r, 0)).start()
            return c

        def wait(e, c):
            @pl.when(zrow_ref[e] >= 0)
            def _():
                zero_copy(0).wait()
            return c

        lax.fori_loop(0, N_EXPERTS, start, 0)
        lax.fori_loop(0, N_EXPERTS, wait, 0)

    def issue(t, c):
        for kk in range(TOP_K):
            row_copy(t, dest_ref[t * TOP_K + kk]).start(priority=kk % 2)
        return c

    def drain(t, c):
        for kk in range(TOP_K):
            row_copy(0, 0).wait()
        return c

    lax.fori_loop(0, tm, issue, 0)
    lax.fori_loop(0, tm, drain, 0)


def _dispatch(zrow, dest_flat, h1t, n_rows, *, tm):
    t = h1t.shape[0] // TILE_ROWS
    return pl.pallas_call(
        _dispatch_body,
        grid_spec=pltpu.PrefetchScalarGridSpec(
            num_scalar_prefetch=1,
            grid=(t // tm,),
            in_specs=[
                pl.BlockSpec((tm * TOP_K,), lambda i, z: (i,), memory_space=pltpu.SMEM),
                pl.BlockSpec((tm * TILE_ROWS, LANES), lambda i, z: (i, 0)),
            ],
            out_specs=pl.BlockSpec(memory_space=pl.ANY),
            scratch_shapes=[pltpu.VMEM((ROW_BLOCK * TILE_ROWS, LANES), F32),
                            pltpu.SemaphoreType.DMA(()), pltpu.SemaphoreType.DMA(())],
        ),
        out_shape=jax.ShapeDtypeStruct((n_rows * TILE_ROWS, LANES), F32),
        compiler_params=pltpu.CompilerParams(
            dimension_semantics=("arbitrary",), vmem_limit_bytes=VMEM_LIMIT),
        name="dispatch",
    )(zrow, dest_flat, h1t)


def _experts_body(be_ref, nu_ref, xs_ref, wg_ref, wu_ref, wd_ref, ys_ref, wg_scr, wu_scr, wd_scr):
    blk = pl.program_id(0)
    prev = jnp.maximum(blk - 1, 0)

    @pl.when((blk < nu_ref[0]) & ((blk == 0) | (be_ref[blk] != be_ref[prev])))
    def _():
        wg_scr[...] = wg_ref[0].astype(BF16)
        wu_scr[...] = wu_ref[0].astype(BF16)
        wd_scr[...] = wd_ref[0].astype(BF16)

    @pl.when(blk < nu_ref[0])
    def _():
        xb = _tiles_to_rows(xs_ref, ROW_BLOCK).astype(BF16)
        gate = jnp.dot(xb, wg_scr[...], preferred_element_type=F32)
        up = jnp.dot(xb, wu_scr[...], preferred_element_type=F32)
        hid = (jax.nn.silu(gate) * up).astype(BF16)
        _rows_to_tiles(ys_ref, jnp.dot(hid, wd_scr[...], preferred_element_type=F32))

    @pl.when(blk >= nu_ref[0])
    def _():
        ys_ref[...] = jnp.zeros_like(ys_ref)


def _experts(block_e, n_used, xs, w_gate, w_up, w_down):
    n_blocks = xs.shape[0] // (ROW_BLOCK * TILE_ROWS)
    last = lambda b, be, nu: jnp.minimum(b, nu[0] - 1)
    blk = (ROW_BLOCK * TILE_ROWS, LANES)
    return pl.pallas_call(
        _experts_body,
        grid_spec=pltpu.PrefetchScalarGridSpec(
            num_scalar_prefetch=2,
            grid=(n_blocks,),
            in_specs=[
                pl.BlockSpec(blk, lambda b, be, nu: (last(b, be, nu), 0)),
                pl.BlockSpec((1, D_MODEL, EXPERT_FF), lambda b, be, nu: (be[last(b, be, nu)], 0, 0)),
                pl.BlockSpec((1, D_MODEL, EXPERT_FF), lambda b, be, nu: (be[last(b, be, nu)], 0, 0)),
                pl.BlockSpec((1, EXPERT_FF, D_MODEL), lambda b, be, nu: (be[last(b, be, nu)], 0, 0)),
            ],
            out_specs=pl.BlockSpec(blk, lambda b, be, nu: (b, 0)),
            scratch_shapes=[pltpu.VMEM((D_MODEL, EXPERT_FF), BF16),
                            pltpu.VMEM((D_MODEL, EXPERT_FF), BF16),
                            pltpu.VMEM((EXPERT_FF, D_MODEL), BF16)],
        ),
        out_shape=jax.ShapeDtypeStruct(xs.shape, F32),
        compiler_params=pltpu.CompilerParams(
            dimension_semantics=("arbitrary",), vmem_limit_bytes=VMEM_LIMIT),
        name="experts",
    )(block_e, n_used, xs, w_gate, w_up, w_down)


def _combine_body(dest_ref, h1_ref, w_ref, p_ref, ys_ref, shg_ref, shu_ref, shd_ref,
                  pg_ref, pp_ref, g_ref, b_ref, o_ref, buf, sem):
    tm = h1_ref.shape[0] // TILE_ROWS

    def row_copy(kk, t, d):
        return pltpu.make_async_copy(ys_ref.at[_tile_of(d)], buf.at[_tile_of(kk * tm + t)], sem)

    def issue(t, c):
        for kk in range(TOP_K):
            row_copy(kk, t, dest_ref[t * TOP_K + kk]).start(priority=kk % 2)
        return c

    def drain(t, c):
        for kk in range(TOP_K):
            row_copy(kk, 0, 0).wait()
        return c

    lax.fori_loop(0, tm, issue, 0)

    h1 = _tiles_to_rows(h1_ref, tm)
    hb = h1.astype(BF16)
    hid = (jax.nn.silu(jnp.dot(hb, shg_ref[...], preferred_element_type=F32))
           * jnp.dot(hb, shu_ref[...], preferred_element_type=F32)).astype(BF16)
    r = DEEPNORM_ALPHA * h1 + jnp.dot(hid, shd_ref[...], preferred_element_type=F32)
    emb = jnp.dot(p_ref[...].astype(BF16), pp_ref[...], preferred_element_type=F32)

    lax.fori_loop(0, tm, drain, 0)
    w = w_ref[...]
    for kk in range(TOP_K):
        r = r + w[:, kk:kk + 1] * _tiles_to_rows(buf, tm, base=kk * tm * TILE_ROWS)
    gate = jax.nn.sigmoid(jnp.dot(r.astype(BF16), pg_ref[...], preferred_element_type=F32))
    o_ref[...] = _layer_norm(r + gate * emb, g_ref[...], b_ref[...])


def _combine(dest_flat, h1t, wtok, p2, ys, sh_g, sh_u, sh_d, ple_g, ple_p, l2g, l2b, *, tm):
    t = h1t.shape[0] // TILE_ROWS
    row = lambda w: pl.BlockSpec((tm, w), lambda i: (i, 0))
    return pl.pallas_call(
        _combine_body,
        grid=(t // tm,),
        in_specs=[
            pl.BlockSpec((tm * TOP_K,), lambda i: (i,), memory_space=pltpu.SMEM),
            pl.BlockSpec((tm * TILE_ROWS, LANES), lambda i: (i, 0)), row(TOP_K), row(PLE_DIM),
            pl.BlockSpec(memory_space=pl.ANY),
            _const_spec(sh_g.shape), _const_spec(sh_u.shape), _const_spec(sh_d.shape),
            _const_spec(ple_g.shape), _const_spec(ple_p.shape),
            _const_spec((1, D_MODEL)), _const_spec((1, D_MODEL)),
        ],
        out_specs=row(D_MODEL),
        out_shape=jax.ShapeDtypeStruct((t, D_MODEL), F32),
        scratch_shapes=[pltpu.VMEM((TOP_K * tm * TILE_ROWS, LANES), F32),
                        pltpu.SemaphoreType.DMA(())],
        compiler_params=pltpu.CompilerParams(
            dimension_semantics=("arbitrary",), vmem_limit_bytes=VMEM_LIMIT),
        name="combine",
    )(dest_flat, h1t, wtok, p2, ys, sh_g, sh_u, sh_d, ple_g, ple_p, l2g, l2b)


def _block_layout(counts, n_blocks):
    padded = (counts + ROW_BLOCK - 1) // ROW_BLOCK * ROW_BLOCK
    pad_ends = jnp.cumsum(padded)
    pad_starts = pad_ends - padded
    block_row0 = jnp.arange(n_blocks, dtype=I32) * ROW_BLOCK
    block_e = jnp.minimum(jnp.sum(pad_ends[None, :] <= block_row0[:, None], axis=1),
                          N_EXPERTS - 1).astype(I32)
    n_used = (pad_ends[-1:] // ROW_BLOCK).astype(I32)
    has_pad = (counts % ROW_BLOCK) != 0
    zrow = jnp.where(has_pad, pad_ends - ROW_BLOCK, -1).astype(I32)
    return pad_starts.astype(I32), block_e, n_used, zrow


def kernel(x, p, ln_in_g, ln_in_b, w_in, sg_ln_g, sg_ln_b, sg_w_s, sg_b_s, w_branch_a, w_branch_b,
           w_out, ln1_g, ln1_b, router_w, router_bias, moe_w_gate, moe_w_up, moe_w_down,
           sh_w_gate, sh_w_up, sh_w_down, ple_w_proj, ple_w_gate, ln2_g, ln2_b):
    bsz, seq, dm = x.shape
    t = bsz * seq
    vec = lambda a: a.reshape(1, -1).astype(F32)
    assert DEPTH == 1
    h = x.reshape(t, dm)
    for i in range(DEPTH):
        q, k, v, ma, gb = _inproj(
            h, vec(ln_in_g), vec(ln_in_b), w_in[i].astype(BF16), vec(sg_ln_g[i]), vec(sg_ln_b[i]),
            sg_w_s[i], sg_b_s[i].T, w_branch_a[i].astype(BF16), tm=512)
        yb = _attn(q.reshape(bsz, seq, SB_WIDTH), k.reshape(bsz, seq, SB_WIDTH),
                   v.reshape(bsz, seq, SB_WIDTH), tq=256).reshape(t, SB_WIDTH)

        rw_t = router_w[i].T
        rw_hi = rw_t.astype(BF16)
        rw_lo = (rw_t - rw_hi.astype(F32)).astype(BF16)
        h1, eidx, wsel, pos, cnt = _post(
            h, vec(ln_in_g), vec(ln_in_b), ma, gb, yb, w_branch_b[i].astype(BF16),
            w_out[i].astype(BF16), vec(ln1_g[i]), vec(ln1_b[i]), rw_hi, rw_lo,
            router_bias[i].reshape(N_EXPERTS, 1).astype(F32), tm=512)

        n_blocks = t * TOP_K // ROW_BLOCK + N_EXPERTS
        pad_starts, block_e, n_used, zrow = _block_layout(cnt[:, 0], n_blocks)
        dest = pos
        for e in range(N_EXPERTS):
            dest = dest + jnp.where(eidx == e, pad_starts[e], 0)
        dest = dest.T.reshape(t * TOP_K)

        xs = _dispatch(zrow, dest, h1, n_blocks * ROW_BLOCK, tm=256)
        ys = _experts(block_e, n_used, xs, moe_w_gate[i], moe_w_up[i], moe_w_down[i])
        h = _combine(dest, h1, wsel.T, p[i].reshape(t, PLE_DIM), ys,
                     sh_w_gate[i].astype(BF16), sh_w_up[i].astype(BF16), sh_w_down[i].astype(BF16),
                     ple_w_gate[i].astype(BF16), ple_w_proj[i].astype(BF16),
                     vec(ln2_g[i]), vec(ln2_b[i]), tm=256)
    return h.reshape(bsz, seq, dm)
```

```python
import functools
import math

import jax
import jax.numpy as jnp
from jax import lax
from jax.experimental import pallas as pl
from jax.experimental.pallas import tpu as pltpu

F32 = jnp.float32
BF16 = jnp.bfloat16
I32 = jnp.int32

D_MODEL = 1024
DEPTH = 1
CHUNK = 64
PLE_DIM = 256
SG_BLOCK = 128
SG_GROUPS = 4
SG_WIDTH = 512
SB_HEADS = 8
SB_HEAD_DIM = 64
SB_WIDTH = SB_HEADS * SB_HEAD_DIM
N_EXPERTS = 64
N_GROUPS = 8
GROUP_SIZE = N_EXPERTS // N_GROUPS
TOPK_GROUPS = 4
TOP_K = 8
EXPERT_FF = 256
SHARED_FF = 256
ROUTED_SCALE = 2.5
LN_EPS = 1e-5
DEEPNORM_ALPHA = (2.0 * DEPTH) ** 0.25

_OFF_U, _OFF_V, _OFF_Q, _OFF_K, _OFF_VS, _OFF_GA, _OFF_GB, _OFF_END = (
    0, 512, 1024, 1536, 2048, 2560, 3584, 4608)

LANES = 128
TILE_ROWS = 8
assert TILE_ROWS * LANES == D_MODEL
ATTN_LANES = 256
ATTN_HEADS = ATTN_LANES // SB_HEAD_DIM
ROW_BLOCK = 512
VMEM_LIMIT = 56 * 1024 * 1024

_NT = (((1,), (1,)), ((), ()))


def _layer_norm(x, g, b):
    mu = jnp.mean(x, axis=-1, keepdims=True)
    xc = x - mu
    var = jnp.mean(xc * xc, axis=-1, keepdims=True)
    return xc * lax.rsqrt(var + LN_EPS) * g + b


def _tiles_to_rows(ref, n, base=0):
    return jnp.concatenate(
        [ref[pl.ds(base + c, n, stride=TILE_ROWS), :] for c in range(TILE_ROWS)], axis=1)


def _rows_to_tiles(ref, val):
    n = val.shape[0]
    for c in range(TILE_ROWS):
        ref[pl.ds(c, n, stride=TILE_ROWS), :] = val[:, c * LANES:(c + 1) * LANES]


def _const_spec(shape):
    nd = len(shape)
    return pl.BlockSpec(shape, lambda *_: (0,) * nd, pipeline_mode=pl.Buffered(1))


def _inproj_body(x_ref, g_ref, b_ref, w_ref, sgg_ref, sgb_ref, ws_ref, bs_ref, wba_ref,
                 q_ref, k_ref, v_ref, ma_ref, gb_ref, ya_scr):
    tm = x_ref.shape[0]
    hb = _layer_norm(x_ref[...], g_ref[...], b_ref[...]).astype(BF16)

    def proj(lo, hi):
        return jnp.dot(hb, w_ref[:, lo:hi], preferred_element_type=F32)

    q_ref[...] = (proj(_OFF_Q, _OFF_K) * (1.0 / math.sqrt(SB_HEAD_DIM))).astype(BF16)
    k_ref[...] = proj(_OFF_K, _OFF_VS).astype(BF16)
    v_ref[...] = proj(_OFF_VS, _OFF_GA).astype(BF16)
    gb_ref[...] = jax.nn.sigmoid(proj(_OFF_GB, _OFF_END)).astype(BF16)

    yu = jax.nn.gelu(proj(_OFF_U, _OFF_V))
    vn = _layer_norm(jax.nn.gelu(proj(_OFF_V, _OFF_Q)), sgg_ref[...], sgb_ref[...]).astype(BF16)

    pi = lax.broadcasted_iota(I32, (SG_BLOCK, SG_BLOCK), 0) // CHUNK
    pj = lax.broadcasted_iota(I32, (SG_BLOCK, SG_BLOCK), 1) // CHUNK
    chunk_causal = pj <= pi
    gch = SG_WIDTH // SG_GROUPS
    for g in range(SG_GROUPS):
        wm = jnp.where(chunk_causal, ws_ref[g], 0.0).astype(BF16)
        bias = bs_ref[:, g:g + 1]
        for r in range(tm // SG_BLOCK):
            rows = slice(r * SG_BLOCK, (r + 1) * SG_BLOCK)
            cols = slice(g * gch, (g + 1) * gch)
            s = jnp.dot(wm, vn[rows, cols], preferred_element_type=F32) + bias
            ya_scr[rows, cols] = (yu[rows, cols] * s).astype(BF16)

    ba = jnp.dot(ya_scr[...], wba_ref[...], preferred_element_type=F32)
    ma_ref[...] = (jax.nn.sigmoid(proj(_OFF_GA, _OFF_GB)) * ba).astype(BF16)


def _inproj(x2, ln_g, ln_b, w_in, sg_g, sg_b, w_s, b_sT, w_ba, *, tm):
    t = x2.shape[0]
    row = lambda w: pl.BlockSpec((tm, w), lambda i: (i, 0))
    return pl.pallas_call(
        _inproj_body,
        grid=(t // tm,),
        in_specs=[
            row(D_MODEL),
            _const_spec((1, D_MODEL)), _const_spec((1, D_MODEL)),
            _const_spec(w_in.shape),
            _const_spec((1, SG_WIDTH)), _const_spec((1, SG_WIDTH)),
            _const_spec(w_s.shape), _const_spec(b_sT.shape),
            _const_spec(w_ba.shape),
        ],
        out_specs=[row(SB_WIDTH), row(SB_WIDTH), row(SB_WIDTH), row(D_MODEL), row(D_MODEL)],
        out_shape=[
            jax.ShapeDtypeStruct((t, SB_WIDTH), BF16),
            jax.ShapeDtypeStruct((t, SB_WIDTH), BF16),
            jax.ShapeDtypeStruct((t, SB_WIDTH), BF16),
            jax.ShapeDtypeStruct((t, D_MODEL), BF16),
            jax.ShapeDtypeStruct((t, D_MODEL), BF16),
        ],
        scratch_shapes=[pltpu.VMEM((tm, SG_WIDTH), BF16)],
        compiler_params=pltpu.CompilerParams(
            dimension_semantics=("arbitrary",), vmem_limit_bytes=VMEM_LIMIT),
        name="inproj",
    )(x2, ln_g, ln_b, w_in, sg_g, sg_b, w_s, b_sT, w_ba)


def _attn_body(q_ref, k_ref, v_ref, o_ref, acc_scr, car_scr, *, tq):
    qi = pl.program_id(2)
    q2 = q_ref[0]
    lane_head = lax.broadcasted_iota(I32, (tq, ATTN_LANES), 1) // SB_HEAD_DIM
    zero = jnp.zeros_like(q2)
    in_head = [lane_head == h for h in range(ATTN_HEADS)]
    qn = [jnp.where(in_head[h], -q2, zero) for h in range(ATTN_HEADS)]
    rr = lax.broadcasted_iota(I32, (tq, tq), 0)
    cc = lax.broadcasted_iota(I32, (tq, tq), 1)
    later = (rr > cc).astype(BF16)
    causal = cc < rr
    sign = jnp.uint32(0x80000000)

    def blocks(starts, diag):
        kbs = [k_ref[0, pl.ds(pl.multiple_of(s0, tq), tq), :] for s0 in starts]
        vbs = [v_ref[0, pl.ds(pl.multiple_of(s0, tq), tq), :] for s0 in starts]
        for h in range(ATTN_HEADS):
            car = car_scr[h]
            pv = None
            for n, (kb, vb) in enumerate(zip(kbs, vbs)):
                masked = diag and n == 0
                y = lax.dot_general(qn[h], kb, _NT, preferred_element_type=F32)
                neg_abs = lax.bitcast_convert_type(
                    lax.bitcast_convert_type(y, jnp.uint32) | sign, F32)
                lm = jnp.minimum(y, 0.0) - jnp.log(1.0 + jnp.exp(neg_abs))
                lb = lm - y
                if masked:
                    lm = jnp.where(causal, lm, 0.0)
                tail = jnp.dot(lm.astype(BF16), later, preferred_element_type=F32)
                a = jnp.exp(lb + tail + car)
                if masked:
                    a = jnp.where(causal, a, 0.0)
                d = jnp.dot(a.astype(BF16), vb, preferred_element_type=F32)
                pv = d if pv is None else pv + d
                car = car + jnp.sum(lm, axis=1, keepdims=True)
            acc_scr[h] += pv
            car_scr[h] = car

    acc_scr[...] = jnp.zeros_like(acc_scr)
    car_scr[...] = jnp.zeros_like(car_scr)
    blocks([qi * tq], True)

    @pl.when(qi % 2 == 1)
    def _():
        blocks([(qi - 1) * tq], False)

    def earlier(j, c):
        s0 = (qi - qi % 2 - 1 - 2 * j) * tq
        blocks([s0, s0 - tq], False)
        return c

    lax.fori_loop(0, qi // 2, earlier, 0)
    out = acc_scr[0]
    for h in range(1, ATTN_HEADS):
        out = jnp.where(in_head[h], acc_scr[h], out)
    o_ref[0] = out.astype(BF16)


def _attn(q3, k3, v3, *, tq):
    b, s, _ = q3.shape
    qspec = pl.BlockSpec((1, tq, ATTN_LANES), lambda bi, hp, qi: (bi, qi, hp))
    kvspec = pl.BlockSpec((1, s, ATTN_LANES), lambda bi, hp, qi: (bi, 0, hp))
    return pl.pallas_call(
        functools.partial(_attn_body, tq=tq),
        grid=(b, SB_WIDTH // ATTN_LANES, s // tq),
        in_specs=[qspec, kvspec, kvspec],
        out_specs=qspec,
        out_shape=jax.ShapeDtypeStruct((b, s, SB_WIDTH), BF16),
        scratch_shapes=[pltpu.VMEM((ATTN_HEADS, tq, ATTN_LANES), F32),
                        pltpu.VMEM((ATTN_HEADS, tq, 1), F32)],
        compiler_params=pltpu.CompilerParams(
            dimension_semantics=("arbitrary", "arbitrary", "arbitrary"),
            vmem_limit_bytes=VMEM_LIMIT),
        name="attn",
    )(q3, k3, v3)


def _post_body(x_ref, g_ref, b_ref, ma_ref, gb_ref, yb_ref, wbb_ref, wout_ref, l1g_ref, l1b_ref,
               rwh_ref, rwl_ref, rb_ref,
               h1_ref, eidx_ref, wsel_ref, pos_ref, cnt_ref, cnt_scr):
    tm = x_ref.shape[0]
    step = pl.program_id(0)

    @pl.when(step == 0)
    def _():
        cnt_scr[...] = jnp.zeros_like(cnt_scr)

    h = _layer_norm(x_ref[...], g_ref[...], b_ref[...])
    bb = jnp.dot(yb_ref[...], wbb_ref[...], preferred_element_type=F32)
    merged = ma_ref[...].astype(F32) + gb_ref[...].astype(F32) * bb
    o = jnp.dot(merged.astype(BF16), wout_ref[...], preferred_element_type=F32)
    h1 = _layer_norm(DEEPNORM_ALPHA * h + o, l1g_ref[...], l1b_ref[...])
    _rows_to_tiles(h1_ref, h1)

    hh = h1.astype(BF16)
    hl = (h1 - hh.astype(F32)).astype(BF16)
    logits = (lax.dot_general(rwh_ref[...], hh, _NT, preferred_element_type=F32)
              + lax.dot_general(rwh_ref[...], hl, _NT, preferred_element_type=F32)
              + lax.dot_general(rwl_ref[...], hh, _NT, preferred_element_type=F32))
    scores = jax.nn.sigmoid(logits)
    sel = scores + rb_ref[...]

    sub = lax.broadcasted_iota(I32, (GROUP_SIZE, tm), 0).astype(F32)
    neg = jnp.float32(-jnp.inf)
    selg = [sel[g * GROUP_SIZE:(g + 1) * GROUP_SIZE] for g in range(N_GROUPS)]
    scg = [scores[g * GROUP_SIZE:(g + 1) * GROUP_SIZE] for g in range(N_GROUPS)]
    eid = [sub + float(g * GROUP_SIZE) for g in range(N_GROUPS)]

    grp = []
    for g in range(N_GROUPS):
        m1 = jnp.max(selg[g], axis=0, keepdims=True)
        i1 = jnp.min(jnp.where(selg[g] == m1, sub, float(GROUP_SIZE)), axis=0, keepdims=True)
        m2 = jnp.max(jnp.where(sub == i1, neg, selg[g]), axis=0, keepdims=True)
        grp.append(m1 + m2)
    cur = []
    for g in range(N_GROUPS):
        beaten = jnp.zeros((1, tm), F32)
        for o_g in range(N_GROUPS):
            if o_g == g:
                continue
            wins = (grp[o_g] > grp[g]) if o_g > g else (grp[o_g] >= grp[g])
            beaten = beaten + jnp.where(wins, 1.0, 0.0)
        cur.append(jnp.where(beaten < float(TOPK_GROUPS), selg[g], neg))

    def all_reduce(vals, op):
        acc = vals[0]
        for v in vals[1:]:
            acc = op(acc, v)
        return acc

    idx_rows, w_rows = [], []
    member = [jnp.zeros((GROUP_SIZE, tm), F32) for _ in range(N_GROUPS)]
    for _ in range(TOP_K):
        m = jnp.max(all_reduce(cur, jnp.maximum), axis=0, keepdims=True)
        cand = [jnp.where(cur[g] == m, eid[g], float(N_EXPERTS)) for g in range(N_GROUPS)]
        idx = jnp.min(all_reduce(cand, jnp.minimum), axis=0, keepdims=True)
        hit = [eid[g] == idx for g in range(N_GROUPS)]
        w = jnp.sum(all_reduce([jnp.where(hit[g], scg[g], 0.0) for g in range(N_GROUPS)], jnp.add),
                    axis=0, keepdims=True)
        cur = [jnp.where(hit[g], neg, cur[g]) for g in range(N_GROUPS)]
        member = [jnp.where(hit[g], 1.0, member[g]) for g in range(N_GROUPS)]
        idx_rows.append(idx)
        w_rows.append(w)

    wsum = all_reduce(w_rows, jnp.add)
    for kk in range(TOP_K):
        eidx_ref[kk:kk + 1, :] = idx_rows[kk].astype(I32)
        wsel_ref[kk:kk + 1, :] = w_rows[kk] / wsum * ROUTED_SCALE

    tt = lax.broadcasted_iota(I32, (tm, tm), 0)
    uu = lax.broadcasted_iota(I32, (tm, tm), 1)
    before = (tt < uu).astype(BF16)
    memb = jnp.concatenate(member, axis=0)
    prefix = jnp.dot(memb.astype(BF16), before, preferred_element_type=F32) + cnt_scr[...]
    for kk in range(TOP_K):
        picked = [jnp.where(eid[g] == idx_rows[kk],
                            prefix[g * GROUP_SIZE:(g + 1) * GROUP_SIZE], 0.0) for g in range(N_GROUPS)]
        pos_ref[kk:kk + 1, :] = jnp.sum(all_reduce(picked, jnp.add), axis=0, keepdims=True).astype(I32)
    total = cnt_scr[...] + jnp.sum(memb, axis=1, keepdims=True)
    cnt_scr[...] = total
    cnt_ref[...] = jnp.broadcast_to(total, cnt_ref.shape).astype(I32)


def _post(x2, ln_g, ln_b, ma, gb, yb, w_bb, w_out, l1g, l1b, rw_hi, rw_lo, r_bias, *, tm):
    t = x2.shape[0]
    row = lambda w: pl.BlockSpec((tm, w), lambda i: (i, 0))
    tok = pl.BlockSpec((TOP_K, tm), lambda i: (0, i))
    return pl.pallas_call(
        _post_body,
        grid=(t // tm,),
        in_specs=[
            row(D_MODEL), _const_spec((1, D_MODEL)), _const_spec((1, D_MODEL)),
            row(D_MODEL), row(D_MODEL), row(SB_WIDTH),
            _const_spec(w_bb.shape), _const_spec(w_out.shape),
            _const_spec((1, D_MODEL)), _const_spec((1, D_MODEL)),
            _const_spec(rw_hi.shape), _const_spec(rw_lo.shape), _const_spec(r_bias.shape),
        ],
        out_specs=[pl.BlockSpec((tm * TILE_ROWS, LANES), lambda i: (i, 0)), tok, tok, tok,
                   pl.BlockSpec((N_EXPERTS, LANES), lambda i: (0, 0))],
        out_shape=[
            jax.ShapeDtypeStruct((t * TILE_ROWS, LANES), F32),
            jax.ShapeDtypeStruct((TOP_K, t), I32),
            jax.ShapeDtypeStruct((TOP_K, t), F32),
            jax.ShapeDtypeStruct((TOP_K, t), I32),
            jax.ShapeDtypeStruct((N_EXPERTS, LANES), I32),
        ],
        scratch_shapes=[pltpu.VMEM((N_EXPERTS, 1), F32)],
        compiler_params=pltpu.CompilerParams(
            dimension_semantics=("arbitrary",), vmem_limit_bytes=VMEM_LIMIT),
        name="post",
    )(x2, ln_g, ln_b, ma, gb, yb, w_bb, w_out, l1g, l1b, rw_hi, rw_lo, r_bias)


def _tile_of(r):
    return pl.ds(pl.multiple_of(r * TILE_ROWS, TILE_ROWS), TILE_ROWS)


def _dispatch_body(zrow_ref, dest_ref, h_ref, xs_ref, zero_scr, zsem, sem):
    tm = h_ref.shape[0] // TILE_ROWS
    step = pl.program_id(0)

    def zero_copy(r):
        rows = pl.ds(pl.multiple_of(r * TILE_ROWS, ROW_BLOCK * TILE_ROWS), ROW_BLOCK * TILE_ROWS)
        return pltpu.make_async_copy(zero_scr, xs_ref.at[rows], zsem)

    def row_copy(t, d):
        return pltpu.make_async_copy(h_ref.at[_tile_of(t)], xs_ref.at[_tile_of(d)], sem)

    @pl.when(step == 0)
    def _():
        zero_scr[...] = jnp.zeros_like(zero_scr)

        def start(e, c):
            r = zrow_ref[e]

            @pl.when(r >= 0)
            def _():
                zero_copy(jnp.maximum(r, 0)).start()
            return c

        def wait(e, c):
            @pl.when(zrow_ref[e] >= 0)
            def _():
                zero_copy(0).wait()
            return c

        lax.fori_loop(0, N_EXPERTS, start, 0)
        lax.fori_loop(0, N_EXPERTS, wait, 0)

    def issue(t, c):
        for kk in range(TOP_K):
            row_copy(t, dest_ref[t * TOP_K + kk]).start(priority=kk % 2)
        return c

    def drain(t, c):
        for kk in range(TOP_K):
            row_copy(0, 0).wait()
        return c

    lax.fori_loop(0, tm, issue, 0)
    lax.fori_loop(0, tm, drain, 0)


def _dispatch(zrow, dest_flat, h1t, n_rows, *, tm):
    t = h1t.shape[0] // TILE_ROWS
    return pl.pallas_call(
        _dispatch_body,
        grid_spec=pltpu.PrefetchScalarGridSpec(
            num_scalar_prefetch=1,
            grid=(t // tm,),
            in_specs=[
                pl.BlockSpec((tm * TOP_K,), lambda i, z: (i,), memory_space=pltpu.SMEM),
                pl.BlockSpec((tm * TILE_ROWS, LANES), lambda i, z: (i, 0)),
            ],
            out_specs=pl.BlockSpec(memory_space=pl.ANY),
            scratch_shapes=[pltpu.VMEM((ROW_BLOCK * TILE_ROWS, LANES), F32),
                            pltpu.SemaphoreType.DMA(()), pltpu.SemaphoreType.DMA(())],
        ),
        out_shape=jax.ShapeDtypeStruct((n_rows * TILE_ROWS, LANES), F32),
        compiler_params=pltpu.CompilerParams(
            dimension_semantics=("arbitrary",), vmem_limit_bytes=VMEM_LIMIT),
        name="dispatch",
    )(zrow, dest_flat, h1t)


def _experts_body(be_ref, nu_ref, xs_ref, wg_ref, wu_ref, wd_ref, ys_ref, wg_scr, wu_scr, wd_scr):
    blk = pl.program_id(0)
    prev = jnp.maximum(blk - 1, 0)

    @pl.when((blk < nu_ref[0]) & ((blk == 0) | (be_ref[blk] != be_ref[prev])))
    def _():
        wg_scr[...] = wg_ref[0].astype(BF16)
        wu_scr[...] = wu_ref[0].astype(BF16)
        wd_scr[...] = wd_ref[0].astype(BF16)

    @pl.when(blk < nu_ref[0])
    def _():
        xb = _tiles_to_rows(xs_ref, ROW_BLOCK).astype(BF16)
        gate = jnp.dot(xb, wg_scr[...], preferred_element_type=F32)
        up = jnp.dot(xb, wu_scr[...], preferred_element_type=F32)
        hid = (jax.nn.silu(gate) * up).astype(BF16)
        _rows_to_tiles(ys_ref, jnp.dot(hid, wd_scr[...], preferred_element_type=F32))

    @pl.when(blk >= nu_ref[0])
    def _():
        ys_ref[...] = jnp.zeros_like(ys_ref)


def _experts(block_e, n_used, xs, w_gate, w_up, w_down):
    n_blocks = xs.shape[0] // (ROW_BLOCK * TILE_ROWS)
    last = lambda b, be, nu: jnp.minimum(b, nu[0] - 1)
    blk = (ROW_BLOCK * TILE_ROWS, LANES)
    return pl.pallas_call(
        _experts_body,
        grid_spec=pltpu.PrefetchScalarGridSpec(
            num_scalar_prefetch=2,
            grid=(n_blocks,),
            in_specs=[
                pl.BlockSpec(blk, lambda b, be, nu: (last(b, be, nu), 0)),
                pl.BlockSpec((1, D_MODEL, EXPERT_FF), lambda b, be, nu: (be[last(b, be, nu)], 0, 0)),
                pl.BlockSpec((1, D_MODEL, EXPERT_FF), lambda b, be, nu: (be[last(b, be, nu)], 0, 0)),
                pl.BlockSpec((1, EXPERT_FF, D_MODEL), lambda b, be, nu: (be[last(b, be, nu)], 0, 0)),
            ],
            out_specs=pl.BlockSpec(blk, lambda b, be, nu: (b, 0)),
            scratch_shapes=[pltpu.VMEM((D_MODEL, EXPERT_FF), BF16),
                            pltpu.VMEM((D_MODEL, EXPERT_FF), BF16),
                            pltpu.VMEM((EXPERT_FF, D_MODEL), BF16)],
        ),
        out_shape=jax.ShapeDtypeStruct(xs.shape, F32),
        compiler_params=pltpu.CompilerParams(
            dimension_semantics=("arbitrary",), vmem_limit_bytes=VMEM_LIMIT),
        name="experts",
    )(block_e, n_used, xs, w_gate, w_up, w_down)


def _combine_body(dcur_ref, dnxt_ref, h1_ref, w_ref, p_ref, ys_ref, shg_ref, shu_ref, shd_ref,
                  pg_ref, pp_ref, g_ref, b_ref, o_ref, buf, sems):
    tm = h1_ref.shape[0] // TILE_ROWS
    step = pl.program_id(0)
    slab = TOP_K * tm

    def row_copy(slot, kk, t, d):
        dst = buf.at[_tile_of(slot * slab + kk * tm + t)]
        return pltpu.make_async_copy(ys_ref.at[_tile_of(d)], dst, sems.at[slot])

    def issue(dref, slot):
        def body(t, c):
            for kk in range(TOP_K):
                row_copy(slot, kk, t, dref[t * TOP_K + kk]).start(priority=kk % 2)
            return c
        lax.fori_loop(0, tm, body, 0)

    def drain(slot):
        def body(t, c):
            for kk in range(TOP_K):
                row_copy(slot, kk, 0, 0).wait()
            return c
        lax.fori_loop(0, tm, body, 0)

    slot = lax.rem(step, 2)

    @pl.when(step == 0)
    def _():
        issue(dcur_ref, 0)

    for nxt in range(2):
        @pl.when((step + 1 < pl.num_programs(0)) & (slot == 1 - nxt))
        def _():
            issue(dnxt_ref, nxt)

    h1 = _tiles_to_rows(h1_ref, tm)
    hb = h1.astype(BF16)
    hid = (jax.nn.silu(jnp.dot(hb, shg_ref[...], preferred_element_type=F32))
           * jnp.dot(hb, shu_ref[...], preferred_element_type=F32)).astype(BF16)
    r = DEEPNORM_ALPHA * h1 + jnp.dot(hid, shd_ref[...], preferred_element_type=F32)
    emb = jnp.dot(p_ref[...].astype(BF16), pp_ref[...], preferred_element_type=F32)

    drain(slot)
    w = w_ref[...]
    wk = [jnp.broadcast_to(w[:, kk:kk + 1], (tm, LANES)) for kk in range(TOP_K)]
    base = slot * (slab * TILE_ROWS)
    cols = []
    for c in range(TILE_ROWS):
        acc = r[:, c * LANES:(c + 1) * LANES]
        for kk in range(TOP_K):
            rows = pl.ds(base + kk * tm * TILE_ROWS + c, tm, stride=TILE_ROWS)
            acc = acc + wk[kk] * buf[rows, :]
        cols.append(acc)
    r = jnp.concatenate(cols, axis=1)
    gate = jax.nn.sigmoid(jnp.dot(r.astype(BF16), pg_ref[...], preferred_element_type=F32))
    o_ref[...] = _layer_norm(r + gate * emb, g_ref[...], b_ref[...])


def _combine(dest_flat, h1t, wtok, p2, ys, sh_g, sh_u, sh_d, ple_g, ple_p, l2g, l2b, *, tm):
    t = h1t.shape[0] // TILE_ROWS
    n = t // tm
    row = lambda w: pl.BlockSpec((tm, w), lambda i: (i, 0))
    return pl.pallas_call(
        _combine_body,
        grid=(n,),
        in_specs=[
            pl.BlockSpec((tm * TOP_K,), lambda i: (i,), memory_space=pltpu.SMEM),
            pl.BlockSpec((tm * TOP_K,), lambda i: (jnp.minimum(i + 1, n - 1),),
                         memory_space=pltpu.SMEM),
            pl.BlockSpec((tm * TILE_ROWS, LANES), lambda i: (i, 0)), row(TOP_K), row(PLE_DIM),
            pl.BlockSpec(memory_space=pl.ANY),
            _const_spec(sh_g.shape), _const_spec(sh_u.shape), _const_spec(sh_d.shape),
            _const_spec(ple_g.shape), _const_spec(ple_p.shape),
            _const_spec((1, D_MODEL)), _const_spec((1, D_MODEL)),
        ],
        out_specs=row(D_MODEL),
        out_shape=jax.ShapeDtypeStruct((t, D_MODEL), F32),
        scratch_shapes=[pltpu.VMEM((2 * TOP_K * tm * TILE_ROWS, LANES), F32),
                        pltpu.SemaphoreType.DMA((2,))],
        compiler_params=pltpu.CompilerParams(
            dimension_semantics=("arbitrary",), vmem_limit_bytes=VMEM_LIMIT),
        name="combine",
    )(dest_flat, dest_flat, h1t, wtok, p2, ys, sh_g, sh_u, sh_d, ple_g, ple_p, l2g, l2b)


def _block_layout(counts, n_blocks):
    padded = (counts + ROW_BLOCK - 1) // ROW_BLOCK * ROW_BLOCK
    pad_ends = jnp.cumsum(padded)
    pad_starts = pad_ends - padded
    block_row0 = jnp.arange(n_blocks, dtype=I32) * ROW_BLOCK
    block_e = jnp.minimum(jnp.sum(pad_ends[None, :] <= block_row0[:, None], axis=1),
                          N_EXPERTS - 1).astype(I32)
    n_used = (pad_ends[-1:] // ROW_BLOCK).astype(I32)
    has_pad = (counts % ROW_BLOCK) != 0
    zrow = jnp.where(has_pad, pad_ends - ROW_BLOCK, -1).astype(I32)
    return pad_starts.astype(I32), block_e, n_used, zrow


def kernel(x, p, ln_in_g, ln_in_b, w_in, sg_ln_g, sg_ln_b, sg_w_s, sg_b_s, w_branch_a, w_branch_b,
           w_out, ln1_g, ln1_b, router_w, router_bias, moe_w_gate, moe_w_up, moe_w_down,
           sh_w_gate, sh_w_up, sh_w_down, ple_w_proj, ple_w_gate, ln2_g, ln2_b):
    bsz, seq, dm = x.shape
    t = bsz * seq
    vec = lambda a: a.reshape(1, -1).astype(F32)
    assert DEPTH == 1
    h = x.reshape(t, dm)
    for i in range(DEPTH):
        q, k, v, ma, gb = _inproj(
            h, vec(ln_in_g), vec(ln_in_b), w_in[i].astype(BF16), vec(sg_ln_g[i]), vec(sg_ln_b[i]),
            sg_w_s[i], sg_b_s[i].T, w_branch_a[i].astype(BF16), tm=512)
        yb = _attn(q.reshape(bsz, seq, SB_WIDTH), k.reshape(bsz, seq, SB_WIDTH),
                   v.reshape(bsz, seq, SB_WIDTH), tq=256).reshape(t, SB_WIDTH)

        rw_t = router_w[i].T
        rw_hi = rw_t.astype(BF16)
        rw_lo = (rw_t - rw_hi.astype(F32)).astype(BF16)
        h1, eidx, wsel, pos, cnt = _post(
            h, vec(ln_in_g), vec(ln_in_b), ma, gb, yb, w_branch_b[i].astype(BF16),
            w_out[i].astype(BF16), vec(ln1_g[i]), vec(ln1_b[i]), rw_hi, rw_lo,
            router_bias[i].reshape(N_EXPERTS, 1).astype(F32), tm=512)

        n_blocks = t * TOP_K // ROW_BLOCK + N_EXPERTS
        pad_starts, block_e, n_used, zrow = _block_layout(cnt[:, 0], n_blocks)
        dest = pos
        for e in range(N_EXPERTS):
            dest = dest + jnp.where(eidx == e, pad_starts[e], 0)
        dest = dest.T.reshape(t * TOP_K)

        xs = _dispatch(zrow, dest, h1, n_blocks * ROW_BLOCK, tm=256)
        ys = _experts(block_e, n_used, xs, moe_w_gate[i], moe_w_up[i], moe_w_down[i])
        h = _combine(dest, h1, wsel.T, p[i].reshape(t, PLE_DIM), ys,
                     sh_w_gate[i].astype(BF16), sh_w_up[i].astype(BF16), sh_w_down[i].astype(BF16),
                     ple_w_gate[i].astype(BF16), ple_w_proj[i].astype(BF16),
                     vec(ln2_g[i]), vec(ln2_b[i]), tm=256)
    return h.reshape(bsz, seq, dm)
```

```python
import functools
import math

import jax
import jax.numpy as jnp
from jax import lax
from jax.experimental import pallas as pl
from jax.experimental.pallas import tpu as pltpu

F32 = jnp.float32
BF16 = jnp.bfloat16
I32 = jnp.int32

D_MODEL = 1024
DEPTH = 1
CHUNK = 64
PLE_DIM = 256
SG_BLOCK = 128
SG_GROUPS = 4
SG_WIDTH = 512
SB_HEADS = 8
SB_HEAD_DIM = 64
SB_WIDTH = SB_HEADS * SB_HEAD_DIM
N_EXPERTS = 64
N_GROUPS = 8
GROUP_SIZE = N_EXPERTS // N_GROUPS
TOPK_GROUPS = 4
TOP_K = 8
EXPERT_FF = 256
SHARED_FF = 256
ROUTED_SCALE = 2.5
LN_EPS = 1e-5
DEEPNORM_ALPHA = (2.0 * DEPTH) ** 0.25

_OFF_U, _OFF_V, _OFF_Q, _OFF_K, _OFF_VS, _OFF_GA, _OFF_GB, _OFF_END = (
    0, 512, 1024, 1536, 2048, 2560, 3584, 4608)

LANES = 128
ROW_SUB = 4
assert 2 * ROW_SUB * LANES == D_MODEL
U32 = jnp.uint32
ATTN_LANES = 256
ATTN_HEADS = ATTN_LANES // SB_HEAD_DIM
ROW_BLOCK = 512
VMEM_LIMIT = 56 * 1024 * 1024

_NT = (((1,), (1,)), ((), ()))


def _layer_norm(x, g, b):
    mu = jnp.mean(x, axis=-1, keepdims=True)
    xc = x - mu
    var = jnp.mean(xc * xc, axis=-1, keepdims=True)
    return xc * lax.rsqrt(var + LN_EPS) * g + b


def _unpack_rows(ref, n, base=0):
    lo, hi = [], []
    for s in range(ROW_SUB):
        w = ref[pl.ds(base * ROW_SUB + s, n, stride=ROW_SUB), :]
        lo.append(lax.bitcast_convert_type(w << 16, F32))
        hi.append(lax.bitcast_convert_type(w & jnp.uint32(0xFFFF0000), F32))
    return lo + hi


def _pack_rows(ref, val):
    n = val.shape[0]
    bits = lambda c: lax.bitcast_convert_type(
        val[:, c * LANES:(c + 1) * LANES].astype(BF16).astype(F32), jnp.uint32)
    for s in range(ROW_SUB):
        ref[pl.ds(s, n, stride=ROW_SUB), :] = bits(s + ROW_SUB) | (bits(s) >> 16)


def _const_spec(shape):
    nd = len(shape)
    return pl.BlockSpec(shape, lambda *_: (0,) * nd, pipeline_mode=pl.Buffered(1))


def _inproj_body(x_ref, g_ref, b_ref, w_ref, sgg_ref, sgb_ref, ws_ref, bs_ref, wba_ref,
                 q_ref, k_ref, v_ref, ma_ref, gb_ref, ya_scr):
    tm = x_ref.shape[0]
    hb = _layer_norm(x_ref[...], g_ref[...], b_ref[...]).astype(BF16)

    def proj(lo, hi):
        return jnp.dot(hb, w_ref[:, lo:hi], preferred_element_type=F32)

    q_ref[...] = (proj(_OFF_Q, _OFF_K) * (1.0 / math.sqrt(SB_HEAD_DIM))).astype(BF16)
    k_ref[...] = proj(_OFF_K, _OFF_VS).astype(BF16)
    v_ref[...] = proj(_OFF_VS, _OFF_GA).astype(BF16)
    gb_ref[...] = jax.nn.sigmoid(proj(_OFF_GB, _OFF_END)).astype(BF16)

    yu = jax.nn.gelu(proj(_OFF_U, _OFF_V))
    vn = _layer_norm(jax.nn.gelu(proj(_OFF_V, _OFF_Q)), sgg_ref[...], sgb_ref[...]).astype(BF16)

    pi = lax.broadcasted_iota(I32, (SG_BLOCK, SG_BLOCK), 0) // CHUNK
    pj = lax.broadcasted_iota(I32, (SG_BLOCK, SG_BLOCK), 1) // CHUNK
    chunk_causal = pj <= pi
    gch = SG_WIDTH // SG_GROUPS
    for g in range(SG_GROUPS):
        wm = jnp.where(chunk_causal, ws_ref[g], 0.0).astype(BF16)
        bias = bs_ref[:, g:g + 1]
        for r in range(tm // SG_BLOCK):
            rows = slice(r * SG_BLOCK, (r + 1) * SG_BLOCK)
            cols = slice(g * gch, (g + 1) * gch)
            s = jnp.dot(wm, vn[rows, cols], preferred_element_type=F32) + bias
            ya_scr[rows, cols] = (yu[rows, cols] * s).astype(BF16)

    ba = jnp.dot(ya_scr[...], wba_ref[...], preferred_element_type=F32)
    ma_ref[...] = (jax.nn.sigmoid(proj(_OFF_GA, _OFF_GB)) * ba).astype(BF16)


def _inproj(x2, ln_g, ln_b, w_in, sg_g, sg_b, w_s, b_sT, w_ba, *, tm):
    t = x2.shape[0]
    row = lambda w: pl.BlockSpec((tm, w), lambda i: (i, 0))
    return pl.pallas_call(
        _inproj_body,
        grid=(t // tm,),
        in_specs=[
            row(D_MODEL),
            _const_spec((1, D_MODEL)), _const_spec((1, D_MODEL)),
            _const_spec(w_in.shape),
            _const_spec((1, SG_WIDTH)), _const_spec((1, SG_WIDTH)),
            _const_spec(w_s.shape), _const_spec(b_sT.shape),
            _const_spec(w_ba.shape),
        ],
        out_specs=[row(SB_WIDTH), row(SB_WIDTH), row(SB_WIDTH), row(D_MODEL), row(D_MODEL)],
        out_shape=[
            jax.ShapeDtypeStruct((t, SB_WIDTH), BF16),
            jax.ShapeDtypeStruct((t, SB_WIDTH), BF16),
            jax.ShapeDtypeStruct((t, SB_WIDTH), BF16),
            jax.ShapeDtypeStruct((t, D_MODEL), BF16),
            jax.ShapeDtypeStruct((t, D_MODEL), BF16),
        ],
        scratch_shapes=[pltpu.VMEM((tm, SG_WIDTH), BF16)],
        compiler_params=pltpu.CompilerParams(
            dimension_semantics=("arbitrary",), vmem_limit_bytes=VMEM_LIMIT),
        name="inproj",
    )(x2, ln_g, ln_b, w_in, sg_g, sg_b, w_s, b_sT, w_ba)


def _attn_body(q_ref, k_ref, v_ref, o_ref, acc_scr, car_scr, *, tq):
    qi = pl.program_id(2)
    q2 = q_ref[0]
    lane_head = lax.broadcasted_iota(I32, (tq, ATTN_LANES), 1) // SB_HEAD_DIM
    zero = jnp.zeros_like(q2)
    in_head = [lane_head == h for h in range(ATTN_HEADS)]
    qn = [jnp.where(in_head[h], -q2, zero) for h in range(ATTN_HEADS)]
    rr = lax.broadcasted_iota(I32, (tq, tq), 0)
    cc = lax.broadcasted_iota(I32, (tq, tq), 1)
    later = (rr > cc).astype(BF16)
    causal = cc < rr
    sign = jnp.uint32(0x80000000)

    def blocks(starts, diag):
        kbs = [k_ref[0, pl.ds(pl.multiple_of(s0, tq), tq), :] for s0 in starts]
        vbs = [v_ref[0, pl.ds(pl.multiple_of(s0, tq), tq), :] for s0 in starts]
        for h in range(ATTN_HEADS):
            car = car_scr[h]
            pv = None
            for n, (kb, vb) in enumerate(zip(kbs, vbs)):
                masked = diag and n == 0
                y = lax.dot_general(qn[h], kb, _NT, preferred_element_type=F32)
                neg_abs = lax.bitcast_convert_type(
                    lax.bitcast_convert_type(y, jnp.uint32) | sign, F32)
                lm = jnp.minimum(y, 0.0) - jnp.log(1.0 + jnp.exp(neg_abs))
                lb = lm - y
                if masked:
                    lm = jnp.where(causal, lm, 0.0)
                tail = jnp.dot(lm.astype(BF16), later, preferred_element_type=F32)
                a = jnp.exp(lb + tail + car)
                if masked:
                    a = jnp.where(causal, a, 0.0)
                d = jnp.dot(a.astype(BF16), vb, preferred_element_type=F32)
                pv = d if pv is None else pv + d
                car = car + jnp.sum(lm, axis=1, keepdims=True)
            acc_scr[h] += pv
            car_scr[h] = car

    acc_scr[...] = jnp.zeros_like(acc_scr)
    car_scr[...] = jnp.zeros_like(car_scr)
    blocks([qi * tq], True)

    @pl.when(qi % 2 == 1)
    def _():
        blocks([(qi - 1) * tq], False)

    def earlier(j, c):
        s0 = (qi - qi % 2 - 1 - 2 * j) * tq
        blocks([s0, s0 - tq], False)
        return c

    lax.fori_loop(0, qi // 2, earlier, 0)
    out = acc_scr[0]
    for h in range(1, ATTN_HEADS):
        out = jnp.where(in_head[h], acc_scr[h], out)
    o_ref[0] = out.astype(BF16)


def _attn(q3, k3, v3, *, tq):
    b, s, _ = q3.shape
    qspec = pl.BlockSpec((1, tq, ATTN_LANES), lambda bi, hp, qi: (bi, qi, hp))
    kvspec = pl.BlockSpec((1, s, ATTN_LANES), lambda bi, hp, qi: (bi, 0, hp))
    return pl.pallas_call(
        functools.partial(_attn_body, tq=tq),
        grid=(b, SB_WIDTH // ATTN_LANES, s // tq),
        in_specs=[qspec, kvspec, kvspec],
        out_specs=qspec,
        out_shape=jax.ShapeDtypeStruct((b, s, SB_WIDTH), BF16),
        scratch_shapes=[pltpu.VMEM((ATTN_HEADS, tq, ATTN_LANES), F32),
                        pltpu.VMEM((ATTN_HEADS, tq, 1), F32)],
        compiler_params=pltpu.CompilerParams(
            dimension_semantics=("arbitrary", "arbitrary", "arbitrary"),
            vmem_limit_bytes=VMEM_LIMIT),
        name="attn",
    )(q3, k3, v3)


def _post_body(x_ref, g_ref, b_ref, ma_ref, gb_ref, yb_ref, wbb_ref, wout_ref, l1g_ref, l1b_ref,
               rwh_ref, rwl_ref, rb_ref,
               h1_ref, h1p_ref, eidx_ref, wsel_ref, pos_ref, cnt_ref, cnt_scr):
    tm = x_ref.shape[0]
    step = pl.program_id(0)

    @pl.when(step == 0)
    def _():
        cnt_scr[...] = jnp.zeros_like(cnt_scr)

    h = _layer_norm(x_ref[...], g_ref[...], b_ref[...])
    bb = jnp.dot(yb_ref[...], wbb_ref[...], preferred_element_type=F32)
    merged = ma_ref[...].astype(F32) + gb_ref[...].astype(F32) * bb
    o = jnp.dot(merged.astype(BF16), wout_ref[...], preferred_element_type=F32)
    h1 = _layer_norm(DEEPNORM_ALPHA * h + o, l1g_ref[...], l1b_ref[...])
    h1_ref[...] = h1
    _pack_rows(h1p_ref, h1)

    hh = h1.astype(BF16)
    hl = (h1 - hh.astype(F32)).astype(BF16)
    logits = (lax.dot_general(rwh_ref[...], hh, _NT, preferred_element_type=F32)
              + lax.dot_general(rwh_ref[...], hl, _NT, preferred_element_type=F32)
              + lax.dot_general(rwl_ref[...], hh, _NT, preferred_element_type=F32))
    scores = jax.nn.sigmoid(logits)
    sel = scores + rb_ref[...]

    sub = lax.broadcasted_iota(I32, (GROUP_SIZE, tm), 0).astype(F32)
    neg = jnp.float32(-jnp.inf)
    selg = [sel[g * GROUP_SIZE:(g + 1) * GROUP_SIZE] for g in range(N_GROUPS)]
    scg = [scores[g * GROUP_SIZE:(g + 1) * GROUP_SIZE] for g in range(N_GROUPS)]
    eid = [sub + float(g * GROUP_SIZE) for g in range(N_GROUPS)]

    grp = []
    for g in range(N_GROUPS):
        m1 = jnp.max(selg[g], axis=0, keepdims=True)
        i1 = jnp.min(jnp.where(selg[g] == m1, sub, float(GROUP_SIZE)), axis=0, keepdims=True)
        m2 = jnp.max(jnp.where(sub == i1, neg, selg[g]), axis=0, keepdims=True)
        grp.append(m1 + m2)
    cur = []
    for g in range(N_GROUPS):
        beaten = jnp.zeros((1, tm), F32)
        for o_g in range(N_GROUPS):
            if o_g == g:
                continue
            wins = (grp[o_g] > grp[g]) if o_g > g else (grp[o_g] >= grp[g])
            beaten = beaten + jnp.where(wins, 1.0, 0.0)
        cur.append(jnp.where(beaten < float(TOPK_GROUPS), selg[g], neg))

    def all_reduce(vals, op):
        acc = vals[0]
        for v in vals[1:]:
            acc = op(acc, v)
        return acc

    idx_rows, w_rows = [], []
    member = [jnp.zeros((GROUP_SIZE, tm), F32) for _ in range(N_GROUPS)]
    for _ in range(TOP_K):
        m = jnp.max(all_reduce(cur, jnp.maximum), axis=0, keepdims=True)
        cand = [jnp.where(cur[g] == m, eid[g], float(N_EXPERTS)) for g in range(N_GROUPS)]
        idx = jnp.min(all_reduce(cand, jnp.minimum), axis=0, keepdims=True)
        hit = [eid[g] == idx for g in range(N_GROUPS)]
        w = jnp.sum(all_reduce([jnp.where(hit[g], scg[g], 0.0) for g in range(N_GROUPS)], jnp.add),
                    axis=0, keepdims=True)
        cur = [jnp.where(hit[g], neg, cur[g]) for g in range(N_GROUPS)]
        member = [jnp.where(hit[g], 1.0, member[g]) for g in range(N_GROUPS)]
        idx_rows.append(idx)
        w_rows.append(w)

    wsum = all_reduce(w_rows, jnp.add)
    for kk in range(TOP_K):
        eidx_ref[kk:kk + 1, :] = idx_rows[kk].astype(I32)
        wsel_ref[kk:kk + 1, :] = w_rows[kk] / wsum * ROUTED_SCALE

    tt = lax.broadcasted_iota(I32, (tm, tm), 0)
    uu = lax.broadcasted_iota(I32, (tm, tm), 1)
    before = (tt < uu).astype(BF16)
    memb = jnp.concatenate(member, axis=0)
    prefix = jnp.dot(memb.astype(BF16), before, preferred_element_type=F32) + cnt_scr[...]
    for kk in range(TOP_K):
        picked = [jnp.where(eid[g] == idx_rows[kk],
                            prefix[g * GROUP_SIZE:(g + 1) * GROUP_SIZE], 0.0) for g in range(N_GROUPS)]
        pos_ref[kk:kk + 1, :] = jnp.sum(all_reduce(picked, jnp.add), axis=0, keepdims=True).astype(I32)
    total = cnt_scr[...] + jnp.sum(memb, axis=1, keepdims=True)
    cnt_scr[...] = total
    cnt_ref[...] = jnp.broadcast_to(total, cnt_ref.shape).astype(I32)


def _post(x2, ln_g, ln_b, ma, gb, yb, w_bb, w_out, l1g, l1b, rw_hi, rw_lo, r_bias, *, tm):
    t = x2.shape[0]
    row = lambda w: pl.BlockSpec((tm, w), lambda i: (i, 0))
    tok = pl.BlockSpec((TOP_K, tm), lambda i: (0, i))
    return pl.pallas_call(
        _post_body,
        grid=(t // tm,),
        in_specs=[
            row(D_MODEL), _const_spec((1, D_MODEL)), _const_spec((1, D_MODEL)),
            row(D_MODEL), row(D_MODEL), row(SB_WIDTH),
            _const_spec(w_bb.shape), _const_spec(w_out.shape),
            _const_spec((1, D_MODEL)), _const_spec((1, D_MODEL)),
            _const_spec(rw_hi.shape), _const_spec(rw_lo.shape), _const_spec(r_bias.shape),
        ],
        out_specs=[row(D_MODEL), pl.BlockSpec((tm * ROW_SUB, LANES), lambda i: (i, 0)),
                   tok, tok, tok, pl.BlockSpec((N_EXPERTS, LANES), lambda i: (0, 0))],
        out_shape=[
            jax.ShapeDtypeStruct((t, D_MODEL), F32),
            jax.ShapeDtypeStruct((t * ROW_SUB, LANES), U32),
            jax.ShapeDtypeStruct((TOP_K, t), I32),
            jax.ShapeDtypeStruct((TOP_K, t), F32),
            jax.ShapeDtypeStruct((TOP_K, t), I32),
            jax.ShapeDtypeStruct((N_EXPERTS, LANES), I32),
        ],
        scratch_shapes=[pltpu.VMEM((N_EXPERTS, 1), F32)],
        compiler_params=pltpu.CompilerParams(
            dimension_semantics=("arbitrary",), vmem_limit_bytes=VMEM_LIMIT),
        name="post",
    )(x2, ln_g, ln_b, ma, gb, yb, w_bb, w_out, l1g, l1b, rw_hi, rw_lo, r_bias)


def _slots_body(ps_ref, eidx_ref, pos_ref, dest_ref):
    eidx = eidx_ref[...]
    dest = pos_ref[...]
    for e in range(N_EXPERTS):
        dest = dest + jnp.where(eidx == e, ps_ref[e], 0)
    dest_ref[...] = dest


def _slots(pad_starts, eidx, pos):
    spec = pl.BlockSpec(eidx.shape, lambda i, ps: (0, 0))
    return pl.pallas_call(
        _slots_body,
        grid_spec=pltpu.PrefetchScalarGridSpec(
            num_scalar_prefetch=1, grid=(1,), in_specs=[spec, spec], out_specs=spec),
        out_shape=jax.ShapeDtypeStruct(eidx.shape, I32),
        name="slots",
    )(pad_starts, eidx, pos)


def _tile_of(r):
    return pl.ds(pl.multiple_of(r * ROW_SUB, ROW_SUB), ROW_SUB)


def _dispatch_body(zrow_ref, dest_ref, h_ref, xs_ref, zero_scr, zsem, sem):
    tm = h_ref.shape[0] // ROW_SUB
    step = pl.program_id(0)

    def zero_copy(r):
        rows = pl.ds(pl.multiple_of(r * ROW_SUB, ROW_BLOCK * ROW_SUB), ROW_BLOCK * ROW_SUB)
        return pltpu.make_async_copy(zero_scr, xs_ref.at[rows], zsem)

    def row_copy(t, d):
        return pltpu.make_async_copy(h_ref.at[_tile_of(t)], xs_ref.at[_tile_of(d)], sem)

    @pl.when(step == 0)
    def _():
        zero_scr[...] = jnp.zeros_like(zero_scr)

        def start(e, c):
            r = zrow_ref[e]

            @pl.when(r >= 0)
            def _():
                zero_copy(jnp.maximum(r, 0)).start()
            return c

        def wait(e, c):
            @pl.when(zrow_ref[e] >= 0)
            def _():
                zero_copy(0).wait()
            return c

        lax.fori_loop(0, N_EXPERTS, start, 0)
        lax.fori_loop(0, N_EXPERTS, wait, 0)

    def issue(t, c):
        for kk in range(TOP_K):
            row_copy(t, dest_ref[t * TOP_K + kk]).start(priority=kk % 2)
        return c

    def drain(t, c):
        for kk in range(TOP_K):
            row_copy(0, 0).wait()
        return c

    lax.fori_loop(0, tm, issue, 0)
    lax.fori_loop(0, tm, drain, 0)


def _dispatch(zrow, dest_flat, h1t, n_rows, *, tm):
    t = h1t.shape[0] // ROW_SUB
    return pl.pallas_call(
        _dispatch_body,
        grid_spec=pltpu.PrefetchScalarGridSpec(
            num_scalar_prefetch=1,
            grid=(t // tm,),
            in_specs=[
                pl.BlockSpec((tm * TOP_K,), lambda i, z: (i,), memory_space=pltpu.SMEM),
                pl.BlockSpec((tm * ROW_SUB, LANES), lambda i, z: (i, 0)),
            ],
            out_specs=pl.BlockSpec(memory_space=pl.ANY),
            scratch_shapes=[pltpu.VMEM((ROW_BLOCK * ROW_SUB, LANES), U32),
                            pltpu.SemaphoreType.DMA(()), pltpu.SemaphoreType.DMA(())],
        ),
        out_shape=jax.ShapeDtypeStruct((n_rows * ROW_SUB, LANES), U32),
        compiler_params=pltpu.CompilerParams(
            dimension_semantics=("arbitrary",), vmem_limit_bytes=VMEM_LIMIT),
        name="dispatch",
    )(zrow, dest_flat, h1t)


def _experts_body(be_ref, nu_ref, xs_ref, wg_ref, wu_ref, wd_ref, ys_ref, wg_scr, wu_scr, wd_scr):
    blk = pl.program_id(0)
    prev = jnp.maximum(blk - 1, 0)

    @pl.when((blk < nu_ref[0]) & ((blk == 0) | (be_ref[blk] != be_ref[prev])))
    def _():
        wg_scr[...] = wg_ref[0].astype(BF16)
        wu_scr[...] = wu_ref[0].astype(BF16)
        wd_scr[...] = wd_ref[0].astype(BF16)

    @pl.when(blk < nu_ref[0])
    def _():
        xb = jnp.concatenate(_unpack_rows(xs_ref, ROW_BLOCK), axis=1).astype(BF16)
        gate = jnp.dot(xb, wg_scr[...], preferred_element_type=F32)
        up = jnp.dot(xb, wu_scr[...], preferred_element_type=F32)
        hid = (jax.nn.silu(gate) * up).astype(BF16)
        _pack_rows(ys_ref, jnp.dot(hid, wd_scr[...], preferred_element_type=F32))

    @pl.when(blk >= nu_ref[0])
    def _():
        ys_ref[...] = jnp.zeros_like(ys_ref)


def _experts(block_e, n_used, xs, w_gate, w_up, w_down):
    n_blocks = xs.shape[0] // (ROW_BLOCK * ROW_SUB)
    last = lambda b, be, nu: jnp.minimum(b, nu[0] - 1)
    blk = (ROW_BLOCK * ROW_SUB, LANES)
    return pl.pallas_call(
        _experts_body,
        grid_spec=pltpu.PrefetchScalarGridSpec(
            num_scalar_prefetch=2,
            grid=(n_blocks,),
            in_specs=[
                pl.BlockSpec(blk, lambda b, be, nu: (last(b, be, nu), 0)),
                pl.BlockSpec((1, D_MODEL, EXPERT_FF), lambda b, be, nu: (be[last(b, be, nu)], 0, 0)),
                pl.BlockSpec((1, D_MODEL, EXPERT_FF), lambda b, be, nu: (be[last(b, be, nu)], 0, 0)),
                pl.BlockSpec((1, EXPERT_FF, D_MODEL), lambda b, be, nu: (be[last(b, be, nu)], 0, 0)),
            ],
            out_specs=pl.BlockSpec(blk, lambda b, be, nu: (b, 0)),
            scratch_shapes=[pltpu.VMEM((D_MODEL, EXPERT_FF), BF16),
                            pltpu.VMEM((D_MODEL, EXPERT_FF), BF16),
                            pltpu.VMEM((EXPERT_FF, D_MODEL), BF16)],
        ),
        out_shape=jax.ShapeDtypeStruct(xs.shape, U32),
        compiler_params=pltpu.CompilerParams(
            dimension_semantics=("arbitrary",), vmem_limit_bytes=VMEM_LIMIT),
        name="experts",
    )(block_e, n_used, xs, w_gate, w_up, w_down)


def _combine_body(dest_ref, h1_ref, w_ref, p_ref, ys_ref, shg_ref, shu_ref, shd_ref,
                  pg_ref, pp_ref, g_ref, b_ref, o_ref, buf, sem):
    tm = h1_ref.shape[0]

    def row_copy(kk, t, d):
        return pltpu.make_async_copy(ys_ref.at[_tile_of(d)], buf.at[_tile_of(kk * tm + t)], sem)

    def issue(t, c):
        for kk in range(TOP_K):
            row_copy(kk, t, dest_ref[t * TOP_K + kk]).start(priority=kk % 2)
        return c

    def drain(t, c):
        for kk in range(TOP_K):
            row_copy(kk, 0, 0).wait()
        return c

    lax.fori_loop(0, tm, issue, 0)

    h1 = h1_ref[...]
    hb = h1.astype(BF16)
    hid = (jax.nn.silu(jnp.dot(hb, shg_ref[...], preferred_element_type=F32))
           * jnp.dot(hb, shu_ref[...], preferred_element_type=F32)).astype(BF16)
    r = DEEPNORM_ALPHA * h1 + jnp.dot(hid, shd_ref[...], preferred_element_type=F32)
    emb = jnp.dot(p_ref[...].astype(BF16), pp_ref[...], preferred_element_type=F32)

    lax.fori_loop(0, tm, drain, 0)
    w = w_ref[...]
    cols = [r[:, c * LANES:(c + 1) * LANES] for c in range(2 * ROW_SUB)]
    for kk in range(TOP_K):
        wk = jnp.broadcast_to(w[:, kk:kk + 1], (tm, LANES))
        for c, chunk in enumerate(_unpack_rows(buf, tm, base=kk * tm)):
            cols[c] = cols[c] + wk * chunk
    r = jnp.concatenate(cols, axis=1)
    gate = jax.nn.sigmoid(jnp.dot(r.astype(BF16), pg_ref[...], preferred_element_type=F32))
    o_ref[...] = _layer_norm(r + gate * emb, g_ref[...], b_ref[...])


def _combine(dest_flat, h1, wtok, p2, ys, sh_g, sh_u, sh_d, ple_g, ple_p, l2g, l2b, *, tm):
    t = h1.shape[0]
    row = lambda w: pl.BlockSpec((tm, w), lambda i: (i, 0))
    return pl.pallas_call(
        _combine_body,
        grid=(t // tm,),
        in_specs=[
            pl.BlockSpec((tm * TOP_K,), lambda i: (i,), memory_space=pltpu.SMEM),
            row(D_MODEL), row(TOP_K), row(PLE_DIM),
            pl.BlockSpec(memory_space=pl.ANY),
            _const_spec(sh_g.shape), _const_spec(sh_u.shape), _const_spec(sh_d.shape),
            _const_spec(ple_g.shape), _const_spec(ple_p.shape),
            _const_spec((1, D_MODEL)), _const_spec((1, D_MODEL)),
        ],
        out_specs=row(D_MODEL),
        out_shape=jax.ShapeDtypeStruct((t, D_MODEL), F32),
        scratch_shapes=[pltpu.VMEM((TOP_K * tm * ROW_SUB, LANES), U32),
                        pltpu.SemaphoreType.DMA(())],
        compiler_params=pltpu.CompilerParams(
            dimension_semantics=("arbitrary",), vmem_limit_bytes=VMEM_LIMIT),
        name="combine",
    )(dest_flat, h1, wtok, p2, ys, sh_g, sh_u, sh_d, ple_g, ple_p, l2g, l2b)


def _block_layout(counts, n_blocks):
    padded = (counts + ROW_BLOCK - 1) // ROW_BLOCK * ROW_BLOCK
    pad_ends = jnp.cumsum(padded)
    pad_starts = pad_ends - padded
    block_row0 = jnp.arange(n_blocks, dtype=I32) * ROW_BLOCK
    block_e = jnp.minimum(jnp.sum(pad_ends[None, :] <= block_row0[:, None], axis=1),
                          N_EXPERTS - 1).astype(I32)
    n_used = (pad_ends[-1:] // ROW_BLOCK).astype(I32)
    has_pad = (counts % ROW_BLOCK) != 0
    zrow = jnp.where(has_pad, pad_ends - ROW_BLOCK, -1).astype(I32)
    return pad_starts.astype(I32), block_e, n_used, zrow


def kernel(x, p, ln_in_g, ln_in_b, w_in, sg_ln_g, sg_ln_b, sg_w_s, sg_b_s, w_branch_a, w_branch_b,
           w_out, ln1_g, ln1_b, router_w, router_bias, moe_w_gate, moe_w_up, moe_w_down,
           sh_w_gate, sh_w_up, sh_w_down, ple_w_proj, ple_w_gate, ln2_g, ln2_b):
    bsz, seq, dm = x.shape
    t = bsz * seq
    vec = lambda a: a.reshape(1, -1).astype(F32)
    assert DEPTH == 1
    h = x.reshape(t, dm)
    for i in range(DEPTH):
        q, k, v, ma, gb = _inproj(
            h, vec(ln_in_g), vec(ln_in_b), w_in[i].astype(BF16), vec(sg_ln_g[i]), vec(sg_ln_b[i]),
            sg_w_s[i], sg_b_s[i].T, w_branch_a[i].astype(BF16), tm=512)
        yb = _attn(q.reshape(bsz, seq, SB_WIDTH), k.reshape(bsz, seq, SB_WIDTH),
                   v.reshape(bsz, seq, SB_WIDTH), tq=256).reshape(t, SB_WIDTH)

        rw_t = router_w[i].T
        rw_hi = rw_t.astype(BF16)
        rw_lo = (rw_t - rw_hi.astype(F32)).astype(BF16)
        h1, h1p, eidx, wsel, pos, cnt = _post(
            h, vec(ln_in_g), vec(ln_in_b), ma, gb, yb, w_branch_b[i].astype(BF16),
            w_out[i].astype(BF16), vec(ln1_g[i]), vec(ln1_b[i]), rw_hi, rw_lo,
            router_bias[i].reshape(N_EXPERTS, 1).astype(F32), tm=512)

        n_blocks = t * TOP_K // ROW_BLOCK + N_EXPERTS
        pad_starts, block_e, n_used, zrow = _block_layout(cnt[:, 0], n_blocks)
        dest = _slots(pad_starts, eidx, pos).T.reshape(t * TOP_K)

        xs = _dispatch(zrow, dest, h1p, n_blocks * ROW_BLOCK, tm=256)
        ys = _experts(block_e, n_used, xs, moe_w_gate[i], moe_w_up[i], moe_w_down[i])
        h = _combine(dest, h1, wsel.T, p[i].reshape(t, PLE_DIM), ys,
                     sh_w_gate[i].astype(BF16), sh_w_up[i].astype(BF16), sh_w_down[i].astype(BF16),
                     ple_w_gate[i].astype(BF16), ple_w_proj[i].astype(BF16),
                     vec(ln2_g[i]), vec(ln2_b[i]), tm=256)
    return h.reshape(bsz, seq, dm)
```

```python
import functools
import math

import jax
import jax.numpy as jnp
from jax import lax
from jax.experimental import pallas as pl
from jax.experimental.pallas import tpu as pltpu

F32 = jnp.float32
BF16 = jnp.bfloat16
I32 = jnp.int32

D_MODEL = 1024
DEPTH = 1
CHUNK = 64
PLE_DIM = 256
SG_BLOCK = 128
SG_GROUPS = 4
SG_WIDTH = 512
SB_HEADS = 8
SB_HEAD_DIM = 64
SB_WIDTH = SB_HEADS * SB_HEAD_DIM
N_EXPERTS = 64
N_GROUPS = 8
GROUP_SIZE = N_EXPERTS // N_GROUPS
TOPK_GROUPS = 4
TOP_K = 8
EXPERT_FF = 256
SHARED_FF = 256
ROUTED_SCALE = 2.5
LN_EPS = 1e-5
DEEPNORM_ALPHA = (2.0 * DEPTH) ** 0.25

_OFF_U, _OFF_V, _OFF_Q, _OFF_K, _OFF_VS, _OFF_GA, _OFF_GB, _OFF_END = (
    0, 512, 1024, 1536, 2048, 2560, 3584, 4608)

LANES = 128
ROW_SUB = 4
assert 2 * ROW_SUB * LANES == D_MODEL
U32 = jnp.uint32
ATTN_LANES = 256
ATTN_HEADS = ATTN_LANES // SB_HEAD_DIM
EXP_UNDERFLOW = -110.0
ROW_BLOCK = 512
VMEM_LIMIT = 56 * 1024 * 1024

_NT = (((1,), (1,)), ((), ()))


def _layer_norm(x, g, b):
    mu = jnp.mean(x, axis=-1, keepdims=True)
    xc = x - mu
    var = jnp.mean(xc * xc, axis=-1, keepdims=True)
    return xc * lax.rsqrt(var + LN_EPS) * g + b


def _unpack_rows(ref, n, base=0):
    lo, hi = [], []
    for s in range(ROW_SUB):
        w = ref[pl.ds(base * ROW_SUB + s, n, stride=ROW_SUB), :]
        lo.append(lax.bitcast_convert_type(w << 16, F32))
        hi.append(lax.bitcast_convert_type(w & jnp.uint32(0xFFFF0000), F32))
    return lo + hi


def _pack_rows(ref, val):
    n = val.shape[0]
    bits = lambda c: lax.bitcast_convert_type(
        val[:, c * LANES:(c + 1) * LANES].astype(BF16).astype(F32), jnp.uint32)
    for s in range(ROW_SUB):
        ref[pl.ds(s, n, stride=ROW_SUB), :] = bits(s + ROW_SUB) | (bits(s) >> 16)


def _const_spec(shape):
    nd = len(shape)
    return pl.BlockSpec(shape, lambda *_: (0,) * nd, pipeline_mode=pl.Buffered(1))


def _inproj_body(x_ref, g_ref, b_ref, w_ref, sgg_ref, sgb_ref, ws_ref, bs_ref, wba_ref,
                 q_ref, k_ref, v_ref, ma_ref, gb_ref, ya_scr):
    tm = x_ref.shape[0]
    hb = _layer_norm(x_ref[...], g_ref[...], b_ref[...]).astype(BF16)

    def proj(lo, hi):
        return jnp.dot(hb, w_ref[:, lo:hi], preferred_element_type=F32)

    q_ref[...] = (proj(_OFF_Q, _OFF_K) * (1.0 / math.sqrt(SB_HEAD_DIM))).astype(BF16)
    k_ref[...] = proj(_OFF_K, _OFF_VS).astype(BF16)
    v_ref[...] = proj(_OFF_VS, _OFF_GA).astype(BF16)
    gb_ref[...] = jax.nn.sigmoid(proj(_OFF_GB, _OFF_END)).astype(BF16)

    yu = jax.nn.gelu(proj(_OFF_U, _OFF_V))
    vn = _layer_norm(jax.nn.gelu(proj(_OFF_V, _OFF_Q)), sgg_ref[...], sgb_ref[...]).astype(BF16)

    pi = lax.broadcasted_iota(I32, (SG_BLOCK, SG_BLOCK), 0) // CHUNK
    pj = lax.broadcasted_iota(I32, (SG_BLOCK, SG_BLOCK), 1) // CHUNK
    chunk_causal = pj <= pi
    gch = SG_WIDTH // SG_GROUPS
    for g in range(SG_GROUPS):
        wm = jnp.where(chunk_causal, ws_ref[g], 0.0).astype(BF16)
        bias = bs_ref[:, g:g + 1]
        for r in range(tm // SG_BLOCK):
            rows = slice(r * SG_BLOCK, (r + 1) * SG_BLOCK)
            cols = slice(g * gch, (g + 1) * gch)
            s = jnp.dot(wm, vn[rows, cols], preferred_element_type=F32) + bias
            ya_scr[rows, cols] = (yu[rows, cols] * s).astype(BF16)

    ba = jnp.dot(ya_scr[...], wba_ref[...], preferred_element_type=F32)
    ma_ref[...] = (jax.nn.sigmoid(proj(_OFF_GA, _OFF_GB)) * ba).astype(BF16)


def _inproj(x2, ln_g, ln_b, w_in, sg_g, sg_b, w_s, b_sT, w_ba, *, tm):
    t = x2.shape[0]
    row = lambda w: pl.BlockSpec((tm, w), lambda i: (i, 0))
    return pl.pallas_call(
        _inproj_body,
        grid=(t // tm,),
        in_specs=[
            row(D_MODEL),
            _const_spec((1, D_MODEL)), _const_spec((1, D_MODEL)),
            _const_spec(w_in.shape),
            _const_spec((1, SG_WIDTH)), _const_spec((1, SG_WIDTH)),
            _const_spec(w_s.shape), _const_spec(b_sT.shape),
            _const_spec(w_ba.shape),
        ],
        out_specs=[row(SB_WIDTH), row(SB_WIDTH), row(SB_WIDTH), row(D_MODEL), row(D_MODEL)],
        out_shape=[
            jax.ShapeDtypeStruct((t, SB_WIDTH), BF16),
            jax.ShapeDtypeStruct((t, SB_WIDTH), BF16),
            jax.ShapeDtypeStruct((t, SB_WIDTH), BF16),
            jax.ShapeDtypeStruct((t, D_MODEL), BF16),
            jax.ShapeDtypeStruct((t, D_MODEL), BF16),
        ],
        scratch_shapes=[pltpu.VMEM((tm, SG_WIDTH), BF16)],
        compiler_params=pltpu.CompilerParams(
            dimension_semantics=("arbitrary",), vmem_limit_bytes=VMEM_LIMIT),
        name="inproj",
    )(x2, ln_g, ln_b, w_in, sg_g, sg_b, w_s, b_sT, w_ba)


def _attn_body(q_ref, k_ref, v_ref, o_ref, acc_scr, car_scr, *, tq):
    qi = pl.program_id(2)
    q2 = q_ref[0]
    lane_head = lax.broadcasted_iota(I32, (tq, ATTN_LANES), 1) // SB_HEAD_DIM
    zero = jnp.zeros_like(q2)
    in_head = [lane_head == h for h in range(ATTN_HEADS)]
    qn = [jnp.where(in_head[h], -q2, zero) for h in range(ATTN_HEADS)]
    rr = lax.broadcasted_iota(I32, (tq, tq), 0)
    cc = lax.broadcasted_iota(I32, (tq, tq), 1)
    later = (rr > cc).astype(BF16)
    causal = cc < rr
    sign = jnp.uint32(0x80000000)

    def blocks(starts, diag):
        kbs = [k_ref[0, pl.ds(pl.multiple_of(s0, tq), tq), :] for s0 in starts]
        vbs = [v_ref[0, pl.ds(pl.multiple_of(s0, tq), tq), :] for s0 in starts]
        for h in range(ATTN_HEADS):
            car = car_scr[h]
            pv = None
            for n, (kb, vb) in enumerate(zip(kbs, vbs)):
                masked = diag and n == 0
                y = lax.dot_general(qn[h], kb, _NT, preferred_element_type=F32)
                neg_abs = lax.bitcast_convert_type(
                    lax.bitcast_convert_type(y, jnp.uint32) | sign, F32)
                lm = jnp.minimum(y, 0.0) - jnp.log(1.0 + jnp.exp(neg_abs))
                lb = lm - y
                if masked:
                    lm = jnp.where(causal, lm, 0.0)
                tail = jnp.dot(lm.astype(BF16), later, preferred_element_type=F32)
                a = jnp.exp(lb + tail + car)
                if masked:
                    a = jnp.where(causal, a, 0.0)
                d = jnp.dot(a.astype(BF16), vb, preferred_element_type=F32)
                pv = d if pv is None else pv + d
                car = car + jnp.sum(lm, axis=1, keepdims=True)
            acc_scr[h] += pv
            car_scr[h] = car

    acc_scr[...] = jnp.zeros_like(acc_scr)
    car_scr[...] = jnp.zeros_like(car_scr)
    blocks([qi * tq], True)

    def live():
        top = car_scr[0]
        for h in range(1, ATTN_HEADS):
            top = jnp.maximum(top, car_scr[h])
        return (jnp.max(top) > EXP_UNDERFLOW).astype(I32)

    def earlier(state):
        j, _ = state
        blocks([(qi - 1 - j) * tq], False)
        return j + 1, live()

    lax.while_loop(lambda s: (s[0] < qi) & (s[1] > 0), earlier, (jnp.int32(0), live()))
    out = acc_scr[0]
    for h in range(1, ATTN_HEADS):
        out = jnp.where(in_head[h], acc_scr[h], out)
    o_ref[0] = out.astype(BF16)


def _attn(q3, k3, v3, *, tq):
    b, s, _ = q3.shape
    qspec = pl.BlockSpec((1, tq, ATTN_LANES), lambda bi, hp, qi: (bi, qi, hp))
    kvspec = pl.BlockSpec((1, s, ATTN_LANES), lambda bi, hp, qi: (bi, 0, hp))
    return pl.pallas_call(
        functools.partial(_attn_body, tq=tq),
        grid=(b, SB_WIDTH // ATTN_LANES, s // tq),
        in_specs=[qspec, kvspec, kvspec],
        out_specs=qspec,
        out_shape=jax.ShapeDtypeStruct((b, s, SB_WIDTH), BF16),
        scratch_shapes=[pltpu.VMEM((ATTN_HEADS, tq, ATTN_LANES), F32),
                        pltpu.VMEM((ATTN_HEADS, tq, 1), F32)],
        compiler_params=pltpu.CompilerParams(
            dimension_semantics=("arbitrary", "arbitrary", "arbitrary"),
            vmem_limit_bytes=VMEM_LIMIT),
        name="attn",
    )(q3, k3, v3)


def _post_body(x_ref, g_ref, b_ref, ma_ref, gb_ref, yb_ref, wbb_ref, wout_ref, l1g_ref, l1b_ref,
               rwh_ref, rwl_ref, rb_ref,
               h1_ref, h1p_ref, eidx_ref, wsel_ref, pos_ref, cnt_ref, cnt_scr):
    tm = x_ref.shape[0]
    step = pl.program_id(0)

    @pl.when(step == 0)
    def _():
        cnt_scr[...] = jnp.zeros_like(cnt_scr)

    h = _layer_norm(x_ref[...], g_ref[...], b_ref[...])
    bb = jnp.dot(yb_ref[...], wbb_ref[...], preferred_element_type=F32)
    merged = ma_ref[...].astype(F32) + gb_ref[...].astype(F32) * bb
    o = jnp.dot(merged.astype(BF16), wout_ref[...], preferred_element_type=F32)
    h1 = _layer_norm(DEEPNORM_ALPHA * h + o, l1g_ref[...], l1b_ref[...])
    h1_ref[...] = h1
    _pack_rows(h1p_ref, h1)

    hh = h1.astype(BF16)
    hl = (h1 - hh.astype(F32)).astype(BF16)
    logits = (lax.dot_general(rwh_ref[...], hh, _NT, preferred_element_type=F32)
              + lax.dot_general(rwh_ref[...], hl, _NT, preferred_element_type=F32)
              + lax.dot_general(rwl_ref[...], hh, _NT, preferred_element_type=F32))
    scores = jax.nn.sigmoid(logits)
    sel = scores + rb_ref[...]

    sub = lax.broadcasted_iota(I32, (GROUP_SIZE, tm), 0).astype(F32)
    neg = jnp.float32(-jnp.inf)
    selg = [sel[g * GROUP_SIZE:(g + 1) * GROUP_SIZE] for g in range(N_GROUPS)]
    scg = [scores[g * GROUP_SIZE:(g + 1) * GROUP_SIZE] for g in range(N_GROUPS)]
    eid = [sub + float(g * GROUP_SIZE) for g in range(N_GROUPS)]

    grp = []
    for g in range(N_GROUPS):
        m1 = jnp.max(selg[g], axis=0, keepdims=True)
        i1 = jnp.min(jnp.where(selg[g] == m1, sub, float(GROUP_SIZE)), axis=0, keepdims=True)
        m2 = jnp.max(jnp.where(sub == i1, neg, selg[g]), axis=0, keepdims=True)
        grp.append(m1 + m2)
    cur = []
    for g in range(N_GROUPS):
        beaten = jnp.zeros((1, tm), F32)
        for o_g in range(N_GROUPS):
            if o_g == g:
                continue
            wins = (grp[o_g] > grp[g]) if o_g > g else (grp[o_g] >= grp[g])
            beaten = beaten + jnp.where(wins, 1.0, 0.0)
        cur.append(jnp.where(beaten < float(TOPK_GROUPS), selg[g], neg))

    def all_reduce(vals, op):
        acc = vals[0]
        for v in vals[1:]:
            acc = op(acc, v)
        return acc

    idx_rows, w_rows = [], []
    member = [jnp.zeros((GROUP_SIZE, tm), F32) for _ in range(N_GROUPS)]
    for _ in range(TOP_K):
        m = jnp.max(all_reduce(cur, jnp.maximum), axis=0, keepdims=True)
        cand = [jnp.where(cur[g] == m, eid[g], float(N_EXPERTS)) for g in range(N_GROUPS)]
        idx = jnp.min(all_reduce(cand, jnp.minimum), axis=0, keepdims=True)
        hit = [eid[g] == idx for g in range(N_GROUPS)]
        w = jnp.sum(all_reduce([jnp.where(hit[g], scg[g], 0.0) for g in range(N_GROUPS)], jnp.add),
                    axis=0, keepdims=True)
        cur = [jnp.where(hit[g], neg, cur[g]) for g in range(N_GROUPS)]
        member = [jnp.where(hit[g], 1.0, member[g]) for g in range(N_GROUPS)]
        idx_rows.append(idx)
        w_rows.append(w)

    wsum = all_reduce(w_rows, jnp.add)
    for kk in range(TOP_K):
        eidx_ref[kk:kk + 1, :] = idx_rows[kk].astype(I32)
        wsel_ref[kk:kk + 1, :] = w_rows[kk] / wsum * ROUTED_SCALE

    tt = lax.broadcasted_iota(I32, (tm, tm), 0)
    uu = lax.broadcasted_iota(I32, (tm, tm), 1)
    before = (tt < uu).astype(BF16)
    memb = jnp.concatenate(member, axis=0)
    prefix = jnp.dot(memb.astype(BF16), before, preferred_element_type=F32) + cnt_scr[...]
    for kk in range(TOP_K):
        picked = [jnp.where(eid[g] == idx_rows[kk],
                            prefix[g * GROUP_SIZE:(g + 1) * GROUP_SIZE], 0.0) for g in range(N_GROUPS)]
        pos_ref[kk:kk + 1, :] = jnp.sum(all_reduce(picked, jnp.add), axis=0, keepdims=True).astype(I32)
    total = cnt_scr[...] + jnp.sum(memb, axis=1, keepdims=True)
    cnt_scr[...] = total
    cnt_ref[...] = jnp.broadcast_to(total, cnt_ref.shape).astype(I32)


def _post(x2, ln_g, ln_b, ma, gb, yb, w_bb, w_out, l1g, l1b, rw_hi, rw_lo, r_bias, *, tm):
    t = x2.shape[0]
    row = lambda w: pl.BlockSpec((tm, w), lambda i: (i, 0))
    tok = pl.BlockSpec((TOP_K, tm), lambda i: (0, i))
    return pl.pallas_call(
        _post_body,
        grid=(t // tm,),
        in_specs=[
            row(D_MODEL), _const_spec((1, D_MODEL)), _const_spec((1, D_MODEL)),
            row(D_MODEL), row(D_MODEL), row(SB_WIDTH),
            _const_spec(w_bb.shape), _const_spec(w_out.shape),
            _const_spec((1, D_MODEL)), _const_spec((1, D_MODEL)),
            _const_spec(rw_hi.shape), _const_spec(rw_lo.shape), _const_spec(r_bias.shape),
        ],
        out_specs=[row(D_MODEL), pl.BlockSpec((tm * ROW_SUB, LANES), lambda i: (i, 0)),
                   tok, tok, tok, pl.BlockSpec((N_EXPERTS, LANES), lambda i: (0, 0))],
        out_shape=[
            jax.ShapeDtypeStruct((t, D_MODEL), F32),
            jax.ShapeDtypeStruct((t * ROW_SUB, LANES), U32),
            jax.ShapeDtypeStruct((TOP_K, t), I32),
            jax.ShapeDtypeStruct((TOP_K, t), F32),
            jax.ShapeDtypeStruct((TOP_K, t), I32),
            jax.ShapeDtypeStruct((N_EXPERTS, LANES), I32),
        ],
        scratch_shapes=[pltpu.VMEM((N_EXPERTS, 1), F32)],
        compiler_params=pltpu.CompilerParams(
            dimension_semantics=("arbitrary",), vmem_limit_bytes=VMEM_LIMIT),
        name="post",
    )(x2, ln_g, ln_b, ma, gb, yb, w_bb, w_out, l1g, l1b, rw_hi, rw_lo, r_bias)


def _slots_body(ps_ref, eidx_ref, pos_ref, dest_ref):
    eidx = eidx_ref[...]
    dest = pos_ref[...]
    for e in range(N_EXPERTS):
        dest = dest + jnp.where(eidx == e, ps_ref[e], 0)
    dest_ref[...] = dest


def _slots(pad_starts, eidx, pos):
    spec = pl.BlockSpec(eidx.shape, lambda i, ps: (0, 0))
    return pl.pallas_call(
        _slots_body,
        grid_spec=pltpu.PrefetchScalarGridSpec(
            num_scalar_prefetch=1, grid=(1,), in_specs=[spec, spec], out_specs=spec),
        out_shape=jax.ShapeDtypeStruct(eidx.shape, I32),
        name="slots",
    )(pad_starts, eidx, pos)


def _tile_of(r):
    return pl.ds(pl.multiple_of(r * ROW_SUB, ROW_SUB), ROW_SUB)


def _dispatch_body(zrow_ref, dest_ref, h_ref, xs_ref, zero_scr, zsem, sem):
    tm = h_ref.shape[0] // ROW_SUB
    step = pl.program_id(0)

    def zero_copy(r):
        rows = pl.ds(pl.multiple_of(r * ROW_SUB, ROW_BLOCK * ROW_SUB), ROW_BLOCK * ROW_SUB)
        return pltpu.make_async_copy(zero_scr, xs_ref.at[rows], zsem)

    def row_copy(t, d):
        return pltpu.make_async_copy(h_ref.at[_tile_of(t)], xs_ref.at[_tile_of(d)], sem)

    @pl.when(step == 0)
    def _():
        zero_scr[...] = jnp.zeros_like(zero_scr)

        def start(e, c):
            r = zrow_ref[e]

            @pl.when(r >= 0)
            def _():
                zero_copy(jnp.maximum(r, 0)).start()
            return c

        def wait(e, c):
            @pl.when(zrow_ref[e] >= 0)
            def _():
                zero_copy(0).wait()
            return c

        lax.fori_loop(0, N_EXPERTS, start, 0)
        lax.fori_loop(0, N_EXPERTS, wait, 0)

    def issue(t, c):
        for kk in range(TOP_K):
            row_copy(t, dest_ref[t * TOP_K + kk]).start(priority=kk % 2)
        return c

    def drain(t, c):
        for kk in range(TOP_K):
            row_copy(0, 0).wait()
        return c

    lax.fori_loop(0, tm, issue, 0)
    lax.fori_loop(0, tm, drain, 0)


def _dispatch(zrow, dest_flat, h1t, n_rows, *, tm):
    t = h1t.shape[0] // ROW_SUB
    return pl.pallas_call(
        _dispatch_body,
        grid_spec=pltpu.PrefetchScalarGridSpec(
            num_scalar_prefetch=1,
            grid=(t // tm,),
            in_specs=[
                pl.BlockSpec((tm * TOP_K,), lambda i, z: (i,), memory_space=pltpu.SMEM),
                pl.BlockSpec((tm * ROW_SUB, LANES), lambda i, z: (i, 0)),
            ],
            out_specs=pl.BlockSpec(memory_space=pl.ANY),
            scratch_shapes=[pltpu.VMEM((ROW_BLOCK * ROW_SUB, LANES), U32),
                            pltpu.SemaphoreType.DMA(()), pltpu.SemaphoreType.DMA(())],
        ),
        out_shape=jax.ShapeDtypeStruct((n_rows * ROW_SUB, LANES), U32),
        compiler_params=pltpu.CompilerParams(
            dimension_semantics=("arbitrary",), vmem_limit_bytes=VMEM_LIMIT),
        name="dispatch",
    )(zrow, dest_flat, h1t)


def _experts_body(be_ref, nu_ref, xs_ref, wg_ref, wu_ref, wd_ref, ys_ref, wg_scr, wu_scr, wd_scr):
    blk = pl.program_id(0)
    prev = jnp.maximum(blk - 1, 0)

    @pl.when((blk < nu_ref[0]) & ((blk == 0) | (be_ref[blk] != be_ref[prev])))
    def _():
        wg_scr[...] = wg_ref[0].astype(BF16)
        wu_scr[...] = wu_ref[0].astype(BF16)
        wd_scr[...] = wd_ref[0].astype(BF16)

    @pl.when(blk < nu_ref[0])
    def _():
        xb = jnp.concatenate(_unpack_rows(xs_ref, ROW_BLOCK), axis=1).astype(BF16)
        gate = jnp.dot(xb, wg_scr[...], preferred_element_type=F32)
        up = jnp.dot(xb, wu_scr[...], preferred_element_type=F32)
        hid = (jax.nn.silu(gate) * up).astype(BF16)
        _pack_rows(ys_ref, jnp.dot(hid, wd_scr[...], preferred_element_type=F32))

    @pl.when(blk >= nu_ref[0])
    def _():
        ys_ref[...] = jnp.zeros_like(ys_ref)


def _experts(block_e, n_used, xs, w_gate, w_up, w_down):
    n_blocks = xs.shape[0] // (ROW_BLOCK * ROW_SUB)
    last = lambda b, be, nu: jnp.minimum(b, nu[0] - 1)
    blk = (ROW_BLOCK * ROW_SUB, LANES)
    return pl.pallas_call(
        _experts_body,
        grid_spec=pltpu.PrefetchScalarGridSpec(
            num_scalar_prefetch=2,
            grid=(n_blocks,),
            in_specs=[
                pl.BlockSpec(blk, lambda b, be, nu: (last(b, be, nu), 0)),
                pl.BlockSpec((1, D_MODEL, EXPERT_FF), lambda b, be, nu: (be[last(b, be, nu)], 0, 0)),
                pl.BlockSpec((1, D_MODEL, EXPERT_FF), lambda b, be, nu: (be[last(b, be, nu)], 0, 0)),
                pl.BlockSpec((1, EXPERT_FF, D_MODEL), lambda b, be, nu: (be[last(b, be, nu)], 0, 0)),
            ],
            out_specs=pl.BlockSpec(blk, lambda b, be, nu: (b, 0)),
            scratch_shapes=[pltpu.VMEM((D_MODEL, EXPERT_FF), BF16),
                            pltpu.VMEM((D_MODEL, EXPERT_FF), BF16),
                            pltpu.VMEM((EXPERT_FF, D_MODEL), BF16)],
        ),
        out_shape=jax.ShapeDtypeStruct(xs.shape, U32),
        compiler_params=pltpu.CompilerParams(
            dimension_semantics=("arbitrary",), vmem_limit_bytes=VMEM_LIMIT),
        name="experts",
    )(block_e, n_used, xs, w_gate, w_up, w_down)


def _combine_body(dest_ref, h1_ref, w_ref, p_ref, ys_ref, shg_ref, shu_ref, shd_ref,
                  pg_ref, pp_ref, g_ref, b_ref, o_ref, buf, sem):
    tm = h1_ref.shape[0]

    def row_copy(kk, t, d):
        return pltpu.make_async_copy(ys_ref.at[_tile_of(d)], buf.at[_tile_of(kk * tm + t)], sem)

    def issue(t, c):
        for kk in range(TOP_K):
            row_copy(kk, t, dest_ref[t * TOP_K + kk]).start(priority=kk % 2)
        return c

    def drain(t, c):
        for kk in range(TOP_K):
            row_copy(kk, 0, 0).wait()
        return c

    lax.fori_loop(0, tm, issue, 0)

    h1 = h1_ref[...]
    hb = h1.astype(BF16)
    hid = (jax.nn.silu(jnp.dot(hb, shg_ref[...], preferred_element_type=F32))
           * jnp.dot(hb, shu_ref[...], preferred_element_type=F32)).astype(BF16)
    r = DEEPNORM_ALPHA * h1 + jnp.dot(hid, shd_ref[...], preferred_element_type=F32)
    emb = jnp.dot(p_ref[...].astype(BF16), pp_ref[...], preferred_element_type=F32)

    lax.fori_loop(0, tm, drain, 0)
    w = w_ref[...]
    cols = [r[:, c * LANES:(c + 1) * LANES] for c in range(2 * ROW_SUB)]
    for kk in range(TOP_K):
        wk = jnp.broadcast_to(w[:, kk:kk + 1], (tm, LANES))
        for c, chunk in enumerate(_unpack_rows(buf, tm, base=kk * tm)):
            cols[c] = cols[c] + wk * chunk
    r = jnp.concatenate(cols, axis=1)
    gate = jax.nn.sigmoid(jnp.dot(r.astype(BF16), pg_ref[...], preferred_element_type=F32))
    o_ref[...] = _layer_norm(r + gate * emb, g_ref[...], b_ref[...])


def _combine(dest_flat, h1, wtok, p2, ys, sh_g, sh_u, sh_d, ple_g, ple_p, l2g, l2b, *, tm):
    t = h1.shape[0]
    row = lambda w: pl.BlockSpec((tm, w), lambda i: (i, 0))
    return pl.pallas_call(
        _combine_body,
        grid=(t // tm,),
        in_specs=[
            pl.BlockSpec((tm * TOP_K,), lambda i: (i,), memory_space=pltpu.SMEM),
            row(D_MODEL), row(TOP_K), row(PLE_DIM),
            pl.BlockSpec(memory_space=pl.ANY),
            _const_spec(sh_g.shape), _const_spec(sh_u.shape), _const_spec(sh_d.shape),
            _const_spec(ple_g.shape), _const_spec(ple_p.shape),
            _const_spec((1, D_MODEL)), _const_spec((1, D_MODEL)),
        ],
        out_specs=row(D_MODEL),
        out_shape=jax.ShapeDtypeStruct((t, D_MODEL), F32),
        scratch_shapes=[pltpu.VMEM((TOP_K * tm * ROW_SUB, LANES), U32),
                        pltpu.SemaphoreType.DMA(())],
        compiler_params=pltpu.CompilerParams(
            dimension_semantics=("arbitrary",), vmem_limit_bytes=VMEM_LIMIT),
        name="combine",
    )(dest_flat, h1, wtok, p2, ys, sh_g, sh_u, sh_d, ple_g, ple_p, l2g, l2b)


def _block_layout(counts, n_blocks):
    padded = (counts + ROW_BLOCK - 1) // ROW_BLOCK * ROW_BLOCK
    pad_ends = jnp.cumsum(padded)
    pad_starts = pad_ends - padded
    block_row0 = jnp.arange(n_blocks, dtype=I32) * ROW_BLOCK
    block_e = jnp.minimum(jnp.sum(pad_ends[None, :] <= block_row0[:, None], axis=1),
                          N_EXPERTS - 1).astype(I32)
    n_used = (pad_ends[-1:] // ROW_BLOCK).astype(I32)
    has_pad = (counts % ROW_BLOCK) != 0
    zrow = jnp.where(has_pad, pad_ends - ROW_BLOCK, -1).astype(I32)
    return pad_starts.astype(I32), block_e, n_used, zrow


def kernel(x, p, ln_in_g, ln_in_b, w_in, sg_ln_g, sg_ln_b, sg_w_s, sg_b_s, w_branch_a, w_branch_b,
           w_out, ln1_g, ln1_b, router_w, router_bias, moe_w_gate, moe_w_up, moe_w_down,
           sh_w_gate, sh_w_up, sh_w_down, ple_w_proj, ple_w_gate, ln2_g, ln2_b):
    bsz, seq, dm = x.shape
    t = bsz * seq
    vec = lambda a: a.reshape(1, -1).astype(F32)
    assert DEPTH == 1
    h = x.reshape(t, dm)
    for i in range(DEPTH):
        q, k, v, ma, gb = _inproj(
            h, vec(ln_in_g), vec(ln_in_b), w_in[i].astype(BF16), vec(sg_ln_g[i]), vec(sg_ln_b[i]),
            sg_w_s[i], sg_b_s[i].T, w_branch_a[i].astype(BF16), tm=512)
        yb = _attn(q.reshape(bsz, seq, SB_WIDTH), k.reshape(bsz, seq, SB_WIDTH),
                   v.reshape(bsz, seq, SB_WIDTH), tq=256).reshape(t, SB_WIDTH)

        rw_t = router_w[i].T
        rw_hi = rw_t.astype(BF16)
        rw_lo = (rw_t - rw_hi.astype(F32)).astype(BF16)
        h1, h1p, eidx, wsel, pos, cnt = _post(
            h, vec(ln_in_g), vec(ln_in_b), ma, gb, yb, w_branch_b[i].astype(BF16),
            w_out[i].astype(BF16), vec(ln1_g[i]), vec(ln1_b[i]), rw_hi, rw_lo,
            router_bias[i].reshape(N_EXPERTS, 1).astype(F32), tm=512)

        n_blocks = t * TOP_K // ROW_BLOCK + N_EXPERTS
        pad_starts, block_e, n_used, zrow = _block_layout(cnt[:, 0], n_blocks)
        dest = _slots(pad_starts, eidx, pos).T.reshape(t * TOP_K)

        xs = _dispatch(zrow, dest, h1p, n_blocks * ROW_BLOCK, tm=256)
        ys = _experts(block_e, n_used, xs, moe_w_gate[i], moe_w_up[i], moe_w_down[i])
        h = _combine(dest, h1, wsel.T, p[i].reshape(t, PLE_DIM), ys,
                     sh_w_gate[i].astype(BF16), sh_w_up[i].astype(BF16), sh_w_down[i].astype(BF16),
                     ple_w_gate[i].astype(BF16), ple_w_proj[i].astype(BF16),
                     vec(ln2_g[i]), vec(ln2_b[i]), tm=256)
    return h.reshape(bsz, seq, dm)
```

```python
import functools
import math

import jax
import jax.numpy as jnp
from jax import lax
from jax.experimental import pallas as pl
from jax.experimental.pallas import tpu as pltpu

F32 = jnp.float32
BF16 = jnp.bfloat16
I32 = jnp.int32

D_MODEL = 1024
DEPTH = 1
CHUNK = 64
PLE_DIM = 256
SG_BLOCK = 128
SG_GROUPS = 4
SG_WIDTH = 512
SB_HEADS = 8
SB_HEAD_DIM = 64
SB_WIDTH = SB_HEADS * SB_HEAD_DIM
N_EXPERTS = 64
N_GROUPS = 8
GROUP_SIZE = N_EXPERTS // N_GROUPS
TOPK_GROUPS = 4
TOP_K = 8
EXPERT_FF = 256
SHARED_FF = 256
ROUTED_SCALE = 2.5
LN_EPS = 1e-5
DEEPNORM_ALPHA = (2.0 * DEPTH) ** 0.25

_OFF_U, _OFF_V, _OFF_Q, _OFF_K, _OFF_VS, _OFF_GA, _OFF_GB, _OFF_END = (
    0, 512, 1024, 1536, 2048, 2560, 3584, 4608)

LANES = 128
ROW_SUB = 4
assert 2 * ROW_SUB * LANES == D_MODEL
U32 = jnp.uint32
ATTN_LANES = 256
ATTN_HEADS = ATTN_LANES // SB_HEAD_DIM
EXP_UNDERFLOW = -110.0
ROW_BLOCK = 512
INPROJ_ROWS, POST_ROWS, ATTN_ROWS, DISPATCH_ROWS, COMBINE_ROWS = 512, 512, 256, 256, 256
VMEM_LIMIT = 56 * 1024 * 1024

_NT = (((1,), (1,)), ((), ()))


def _layer_norm(x, g, b):
    mu = jnp.mean(x, axis=-1, keepdims=True)
    xc = x - mu
    var = jnp.mean(xc * xc, axis=-1, keepdims=True)
    return xc * lax.rsqrt(var + LN_EPS) * g + b


def _unpack_rows(ref, n, base=0):
    lo, hi = [], []
    for s in range(ROW_SUB):
        w = ref[pl.ds(base * ROW_SUB + s, n, stride=ROW_SUB), :]
        lo.append(lax.bitcast_convert_type(w << 16, F32))
        hi.append(lax.bitcast_convert_type(w & jnp.uint32(0xFFFF0000), F32))
    return lo + hi


def _pack_rows(ref, val):
    n = val.shape[0]
    bits = lambda c: lax.bitcast_convert_type(
        val[:, c * LANES:(c + 1) * LANES].astype(BF16).astype(F32), jnp.uint32)
    for s in range(ROW_SUB):
        ref[pl.ds(s, n, stride=ROW_SUB), :] = bits(s + ROW_SUB) | (bits(s) >> 16)


def _const_spec(shape):
    nd = len(shape)
    return pl.BlockSpec(shape, lambda *_: (0,) * nd, pipeline_mode=pl.Buffered(1))


def _inproj_body(x_ref, g_ref, b_ref, w_ref, sgg_ref, sgb_ref, ws_ref, bs_ref, wba_ref,
                 q_ref, k_ref, v_ref, ma_ref, gb_ref, ya_scr):
    tm = x_ref.shape[0]
    hb = _layer_norm(x_ref[...], g_ref[...], b_ref[...]).astype(BF16)

    def proj(lo, hi):
        return jnp.dot(hb, w_ref[:, lo:hi], preferred_element_type=F32)

    q_ref[...] = (proj(_OFF_Q, _OFF_K) * (1.0 / math.sqrt(SB_HEAD_DIM))).astype(BF16)
    k_ref[...] = proj(_OFF_K, _OFF_VS).astype(BF16)
    v_ref[...] = proj(_OFF_VS, _OFF_GA).astype(BF16)
    gb_ref[...] = jax.nn.sigmoid(proj(_OFF_GB, _OFF_END)).astype(BF16)

    yu = jax.nn.gelu(proj(_OFF_U, _OFF_V))
    vn = _layer_norm(jax.nn.gelu(proj(_OFF_V, _OFF_Q)), sgg_ref[...], sgb_ref[...]).astype(BF16)

    pi = lax.broadcasted_iota(I32, (SG_BLOCK, SG_BLOCK), 0) // CHUNK
    pj = lax.broadcasted_iota(I32, (SG_BLOCK, SG_BLOCK), 1) // CHUNK
    chunk_causal = pj <= pi
    gch = SG_WIDTH // SG_GROUPS
    for g in range(SG_GROUPS):
        wm = jnp.where(chunk_causal, ws_ref[g], 0.0).astype(BF16)
        bias = bs_ref[:, g:g + 1]
        for r in range(tm // SG_BLOCK):
            rows = slice(r * SG_BLOCK, (r + 1) * SG_BLOCK)
            cols = slice(g * gch, (g + 1) * gch)
            s = jnp.dot(wm, vn[rows, cols], preferred_element_type=F32) + bias
            ya_scr[rows, cols] = (yu[rows, cols] * s).astype(BF16)

    ba = jnp.dot(ya_scr[...], wba_ref[...], preferred_element_type=F32)
    ma_ref[...] = (jax.nn.sigmoid(proj(_OFF_GA, _OFF_GB)) * ba).astype(BF16)


def _inproj(x2, ln_g, ln_b, w_in, sg_g, sg_b, w_s, b_sT, w_ba, *, tm):
    t = x2.shape[0]
    row = lambda w: pl.BlockSpec((tm, w), lambda i: (i, 0))
    return pl.pallas_call(
        _inproj_body,
        grid=(t // tm,),
        in_specs=[
            row(D_MODEL),
            _const_spec((1, D_MODEL)), _const_spec((1, D_MODEL)),
            _const_spec(w_in.shape),
            _const_spec((1, SG_WIDTH)), _const_spec((1, SG_WIDTH)),
            _const_spec(w_s.shape), _const_spec(b_sT.shape),
            _const_spec(w_ba.shape),
        ],
        out_specs=[row(SB_WIDTH), row(SB_WIDTH), row(SB_WIDTH), row(D_MODEL), row(D_MODEL)],
        out_shape=[
            jax.ShapeDtypeStruct((t, SB_WIDTH), BF16),
            jax.ShapeDtypeStruct((t, SB_WIDTH), BF16),
            jax.ShapeDtypeStruct((t, SB_WIDTH), BF16),
            jax.ShapeDtypeStruct((t, D_MODEL), BF16),
            jax.ShapeDtypeStruct((t, D_MODEL), BF16),
        ],
        scratch_shapes=[pltpu.VMEM((tm, SG_WIDTH), BF16)],
        compiler_params=pltpu.CompilerParams(
            dimension_semantics=("arbitrary",), vmem_limit_bytes=VMEM_LIMIT),
        name="inproj",
    )(x2, ln_g, ln_b, w_in, sg_g, sg_b, w_s, b_sT, w_ba)


def _attn_body(q_ref, k_ref, v_ref, o_ref, acc_scr, car_scr, *, tq):
    qi = pl.program_id(2)
    q2 = q_ref[0]
    lane_head = lax.broadcasted_iota(I32, (tq, ATTN_LANES), 1) // SB_HEAD_DIM
    zero = jnp.zeros_like(q2)
    in_head = [lane_head == h for h in range(ATTN_HEADS)]
    qn = [jnp.where(in_head[h], -q2, zero) for h in range(ATTN_HEADS)]
    rr = lax.broadcasted_iota(I32, (tq, tq), 0)
    cc = lax.broadcasted_iota(I32, (tq, tq), 1)
    later = (rr > cc).astype(BF16)
    causal = cc < rr
    sign = jnp.uint32(0x80000000)

    def blocks(starts, diag):
        kbs = [k_ref[0, pl.ds(pl.multiple_of(s0, tq), tq), :] for s0 in starts]
        vbs = [v_ref[0, pl.ds(pl.multiple_of(s0, tq), tq), :] for s0 in starts]
        for h in range(ATTN_HEADS):
            car = car_scr[h]
            pv = None
            for n, (kb, vb) in enumerate(zip(kbs, vbs)):
                masked = diag and n == 0
                y = lax.dot_general(qn[h], kb, _NT, preferred_element_type=F32)
                neg_abs = lax.bitcast_convert_type(
                    lax.bitcast_convert_type(y, jnp.uint32) | sign, F32)
                lm = jnp.minimum(y, 0.0) - jnp.log(1.0 + jnp.exp(neg_abs))
                lb = lm - y
                if masked:
                    lm = jnp.where(causal, lm, 0.0)
                tail = jnp.dot(lm.astype(BF16), later, preferred_element_type=F32)
                a = jnp.exp(lb + tail + car)
                if masked:
                    a = jnp.where(causal, a, 0.0)
                d = jnp.dot(a.astype(BF16), vb, preferred_element_type=F32)
                pv = d if pv is None else pv + d
                car = car + jnp.sum(lm, axis=1, keepdims=True)
            acc_scr[h] += pv
            car_scr[h] = car

    acc_scr[...] = jnp.zeros_like(acc_scr)
    car_scr[...] = jnp.zeros_like(car_scr)
    @pl.when(qi == 0)
    def _():
        blocks([0], True)

    @pl.when(qi > 0)
    def _():
        blocks([qi * tq, (qi - 1) * tq], True)

    def live():
        top = car_scr[0]
        for h in range(1, ATTN_HEADS):
            top = jnp.maximum(top, car_scr[h])
        return (jnp.max(top) > EXP_UNDERFLOW).astype(I32)

    def earlier(state):
        j, _ = state
        blocks([(qi - 1 - j) * tq], False)
        return j + 1, live()

    lax.while_loop(lambda s: (s[0] < qi) & (s[1] > 0), earlier, (jnp.int32(1), live()))
    out = acc_scr[0]
    for h in range(1, ATTN_HEADS):
        out = jnp.where(in_head[h], acc_scr[h], out)
    o_ref[0] = out.astype(BF16)


def _attn(q3, k3, v3, *, tq):
    b, s, _ = q3.shape
    qspec = pl.BlockSpec((1, tq, ATTN_LANES), lambda bi, hp, qi: (bi, qi, hp))
    kvspec = pl.BlockSpec((1, s, ATTN_LANES), lambda bi, hp, qi: (bi, 0, hp))
    return pl.pallas_call(
        functools.partial(_attn_body, tq=tq),
        grid=(b, SB_WIDTH // ATTN_LANES, s // tq),
        in_specs=[qspec, kvspec, kvspec],
        out_specs=qspec,
        out_shape=jax.ShapeDtypeStruct((b, s, SB_WIDTH), BF16),
        scratch_shapes=[pltpu.VMEM((ATTN_HEADS, tq, ATTN_LANES), F32),
                        pltpu.VMEM((ATTN_HEADS, tq, 1), F32)],
        compiler_params=pltpu.CompilerParams(
            dimension_semantics=("arbitrary", "arbitrary", "arbitrary"),
            vmem_limit_bytes=VMEM_LIMIT),
        name="attn",
    )(q3, k3, v3)


def _post_body(x_ref, g_ref, b_ref, ma_ref, gb_ref, yb_ref, wbb_ref, wout_ref, l1g_ref, l1b_ref,
               rwh_ref, rwl_ref, rb_ref,
               h1_ref, h1p_ref, eidx_ref, wsel_ref, pos_ref, cnt_ref, cnt_scr):
    tm = x_ref.shape[0]
    step = pl.program_id(0)

    @pl.when(step == 0)
    def _():
        cnt_scr[...] = jnp.zeros_like(cnt_scr)

    h = _layer_norm(x_ref[...], g_ref[...], b_ref[...])
    bb = jnp.dot(yb_ref[...], wbb_ref[...], preferred_element_type=F32)
    merged = ma_ref[...].astype(F32) + gb_ref[...].astype(F32) * bb
    o = jnp.dot(merged.astype(BF16), wout_ref[...], preferred_element_type=F32)
    h1 = _layer_norm(DEEPNORM_ALPHA * h + o, l1g_ref[...], l1b_ref[...])
    h1_ref[...] = h1
    _pack_rows(h1p_ref, h1)

    hh = h1.astype(BF16)
    hl = (h1 - hh.astype(F32)).astype(BF16)
    logits = (lax.dot_general(rwh_ref[...], hh, _NT, preferred_element_type=F32)
              + lax.dot_general(rwh_ref[...], hl, _NT, preferred_element_type=F32)
              + lax.dot_general(rwl_ref[...], hh, _NT, preferred_element_type=F32))
    scores = jax.nn.sigmoid(logits)
    sel = scores + rb_ref[...]

    sub = lax.broadcasted_iota(I32, (GROUP_SIZE, tm), 0).astype(F32)
    neg = jnp.float32(-jnp.inf)
    selg = [sel[g * GROUP_SIZE:(g + 1) * GROUP_SIZE] for g in range(N_GROUPS)]
    scg = [scores[g * GROUP_SIZE:(g + 1) * GROUP_SIZE] for g in range(N_GROUPS)]
    eid = [sub + float(g * GROUP_SIZE) for g in range(N_GROUPS)]

    grp = []
    for g in range(N_GROUPS):
        m1 = jnp.max(selg[g], axis=0, keepdims=True)
        i1 = jnp.min(jnp.where(selg[g] == m1, sub, float(GROUP_SIZE)), axis=0, keepdims=True)
        m2 = jnp.max(jnp.where(sub == i1, neg, selg[g]), axis=0, keepdims=True)
        grp.append(m1 + m2)
    cur = []
    for g in range(N_GROUPS):
        beaten = jnp.zeros((1, tm), F32)
        for o_g in range(N_GROUPS):
            if o_g == g:
                continue
            wins = (grp[o_g] > grp[g]) if o_g > g else (grp[o_g] >= grp[g])
            beaten = beaten + jnp.where(wins, 1.0, 0.0)
        cur.append(jnp.where(beaten < float(TOPK_GROUPS), selg[g], neg))

    def all_reduce(vals, op):
        acc = vals[0]
        for v in vals[1:]:
            acc = op(acc, v)
        return acc

    idx_rows, w_rows = [], []
    member = [jnp.zeros((GROUP_SIZE, tm), F32) for _ in range(N_GROUPS)]
    for _ in range(TOP_K):
        m = jnp.max(all_reduce(cur, jnp.maximum), axis=0, keepdims=True)
        cand = [jnp.where(cur[g] == m, eid[g], float(N_EXPERTS)) for g in range(N_GROUPS)]
        idx = jnp.min(all_reduce(cand, jnp.minimum), axis=0, keepdims=True)
        hit = [eid[g] == idx for g in range(N_GROUPS)]
        w = jnp.sum(all_reduce([jnp.where(hit[g], scg[g], 0.0) for g in range(N_GROUPS)], jnp.add),
                    axis=0, keepdims=True)
        cur = [jnp.where(hit[g], neg, cur[g]) for g in range(N_GROUPS)]
        member = [jnp.where(hit[g], 1.0, member[g]) for g in range(N_GROUPS)]
        idx_rows.append(idx)
        w_rows.append(w)

    wsum = all_reduce(w_rows, jnp.add)
    for kk in range(TOP_K):
        eidx_ref[kk:kk + 1, :] = idx_rows[kk].astype(I32)
        wsel_ref[kk:kk + 1, :] = w_rows[kk] / wsum * ROUTED_SCALE

    tt = lax.broadcasted_iota(I32, (tm, tm), 0)
    uu = lax.broadcasted_iota(I32, (tm, tm), 1)
    before = (tt < uu).astype(BF16)
    memb = jnp.concatenate(member, axis=0)
    prefix = jnp.dot(memb.astype(BF16), before, preferred_element_type=F32) + cnt_scr[...]
    for kk in range(TOP_K):
        picked = [jnp.where(eid[g] == idx_rows[kk],
                            prefix[g * GROUP_SIZE:(g + 1) * GROUP_SIZE], 0.0) for g in range(N_GROUPS)]
        pos_ref[kk:kk + 1, :] = jnp.sum(all_reduce(picked, jnp.add), axis=0, keepdims=True).astype(I32)
    total = cnt_scr[...] + jnp.sum(memb, axis=1, keepdims=True)
    cnt_scr[...] = total
    cnt_ref[...] = jnp.broadcast_to(total, cnt_ref.shape).astype(I32)


def _post(x2, ln_g, ln_b, ma, gb, yb, w_bb, w_out, l1g, l1b, rw_hi, rw_lo, r_bias, *, tm):
    t = x2.shape[0]
    row = lambda w: pl.BlockSpec((tm, w), lambda i: (i, 0))
    tok = pl.BlockSpec((TOP_K, tm), lambda i: (0, i))
    return pl.pallas_call(
        _post_body,
        grid=(t // tm,),
        in_specs=[
            row(D_MODEL), _const_spec((1, D_MODEL)), _const_spec((1, D_MODEL)),
            row(D_MODEL), row(D_MODEL), row(SB_WIDTH),
            _const_spec(w_bb.shape), _const_spec(w_out.shape),
            _const_spec((1, D_MODEL)), _const_spec((1, D_MODEL)),
            _const_spec(rw_hi.shape), _const_spec(rw_lo.shape), _const_spec(r_bias.shape),
        ],
        out_specs=[row(D_MODEL), pl.BlockSpec((tm * ROW_SUB, LANES), lambda i: (i, 0)),
                   tok, tok, tok, pl.BlockSpec((N_EXPERTS, LANES), lambda i: (0, 0))],
        out_shape=[
            jax.ShapeDtypeStruct((t, D_MODEL), F32),
            jax.ShapeDtypeStruct((t * ROW_SUB, LANES), U32),
            jax.ShapeDtypeStruct((TOP_K, t), I32),
            jax.ShapeDtypeStruct((TOP_K, t), F32),
            jax.ShapeDtypeStruct((TOP_K, t), I32),
            jax.ShapeDtypeStruct((N_EXPERTS, LANES), I32),
        ],
        scratch_shapes=[pltpu.VMEM((N_EXPERTS, 1), F32)],
        compiler_params=pltpu.CompilerParams(
            dimension_semantics=("arbitrary",), vmem_limit_bytes=VMEM_LIMIT),
        name="post",
    )(x2, ln_g, ln_b, ma, gb, yb, w_bb, w_out, l1g, l1b, rw_hi, rw_lo, r_bias)


def _slots_body(ps_ref, eidx_ref, pos_ref, dest_ref):
    eidx = eidx_ref[...]
    dest = pos_ref[...]
    for e in range(N_EXPERTS):
        dest = dest + jnp.where(eidx == e, ps_ref[e], 0)
    dest_ref[...] = dest


def _slots(pad_starts, eidx, pos):
    spec = pl.BlockSpec(eidx.shape, lambda i, ps: (0, 0))
    return pl.pallas_call(
        _slots_body,
        grid_spec=pltpu.PrefetchScalarGridSpec(
            num_scalar_prefetch=1, grid=(1,), in_specs=[spec, spec], out_specs=spec),
        out_shape=jax.ShapeDtypeStruct(eidx.shape, I32),
        name="slots",
    )(pad_starts, eidx, pos)


def _tile_of(r):
    return pl.ds(pl.multiple_of(r * ROW_SUB, ROW_SUB), ROW_SUB)


def _dispatch_body(zrow_ref, dest_ref, h_ref, xs_ref, zero_scr, zsem, sem):
    tm = h_ref.shape[0] // ROW_SUB
    step = pl.program_id(0)

    def zero_copy(r):
        rows = pl.ds(pl.multiple_of(r * ROW_SUB, ROW_BLOCK * ROW_SUB), ROW_BLOCK * ROW_SUB)
        return pltpu.make_async_copy(zero_scr, xs_ref.at[rows], zsem)

    def row_copy(t, d):
        return pltpu.make_async_copy(h_ref.at[_tile_of(t)], xs_ref.at[_tile_of(d)], sem)

    @pl.when(step == 0)
    def _():
        zero_scr[...] = jnp.zeros_like(zero_scr)

        def start(e, c):
            r = zrow_ref[e]

            @pl.when(r >= 0)
            def _():
                zero_copy(jnp.maximum(r, 0)).start()
            return c

        def wait(e, c):
            @pl.when(zrow_ref[e] >= 0)
            def _():
                zero_copy(0).wait()
            return c

        lax.fori_loop(0, N_EXPERTS, start, 0)
        lax.fori_loop(0, N_EXPERTS, wait, 0)

    def issue(t, c):
        for kk in range(TOP_K):
            row_copy(t, dest_ref[t * TOP_K + kk]).start(priority=kk % 2)
        return c

    def drain(t, c):
        for kk in range(TOP_K):
            row_copy(0, 0).wait()
        return c

    lax.fori_loop(0, tm, issue, 0)
    lax.fori_loop(0, tm, drain, 0)


def _dispatch(zrow, dest_flat, h1t, n_rows, *, tm):
    t = h1t.shape[0] // ROW_SUB
    return pl.pallas_call(
        _dispatch_body,
        grid_spec=pltpu.PrefetchScalarGridSpec(
            num_scalar_prefetch=1,
            grid=(t // tm,),
            in_specs=[
                pl.BlockSpec((tm * TOP_K,), lambda i, z: (i,), memory_space=pltpu.SMEM),
                pl.BlockSpec((tm * ROW_SUB, LANES), lambda i, z: (i, 0)),
            ],
            out_specs=pl.BlockSpec(memory_space=pl.ANY),
            scratch_shapes=[pltpu.VMEM((ROW_BLOCK * ROW_SUB, LANES), U32),
                            pltpu.SemaphoreType.DMA(()), pltpu.SemaphoreType.DMA(())],
        ),
        out_shape=jax.ShapeDtypeStruct((n_rows * ROW_SUB, LANES), U32),
        compiler_params=pltpu.CompilerParams(
            dimension_semantics=("arbitrary",), vmem_limit_bytes=VMEM_LIMIT),
        name="dispatch",
    )(zrow, dest_flat, h1t)


def _experts_body(be_ref, nu_ref, xs_ref, wg_ref, wu_ref, wd_ref, ys_ref, wg_scr, wu_scr, wd_scr):
    blk = pl.program_id(0)
    prev = jnp.maximum(blk - 1, 0)

    @pl.when((blk < nu_ref[0]) & ((blk == 0) | (be_ref[blk] != be_ref[prev])))
    def _():
        wg_scr[...] = wg_ref[0].astype(BF16)
        wu_scr[...] = wu_ref[0].astype(BF16)
        wd_scr[...] = wd_ref[0].astype(BF16)

    @pl.when(blk < nu_ref[0])
    def _():
        xb = jnp.concatenate(_unpack_rows(xs_ref, ROW_BLOCK), axis=1).astype(BF16)
        gate = jnp.dot(xb, wg_scr[...], preferred_element_type=F32)
        up = jnp.dot(xb, wu_scr[...], preferred_element_type=F32)
        hid = (jax.nn.silu(gate) * up).astype(BF16)
        _pack_rows(ys_ref, jnp.dot(hid, wd_scr[...], preferred_element_type=F32))

    @pl.when(blk >= nu_ref[0])
    def _():
        ys_ref[...] = jnp.zeros_like(ys_ref)


def _experts(block_e, n_used, xs, w_gate, w_up, w_down):
    n_blocks = xs.shape[0] // (ROW_BLOCK * ROW_SUB)
    last = lambda b, be, nu: jnp.minimum(b, nu[0] - 1)
    blk = (ROW_BLOCK * ROW_SUB, LANES)
    return pl.pallas_call(
        _experts_body,
        grid_spec=pltpu.PrefetchScalarGridSpec(
            num_scalar_prefetch=2,
            grid=(n_blocks,),
            in_specs=[
                pl.BlockSpec(blk, lambda b, be, nu: (last(b, be, nu), 0)),
                pl.BlockSpec((1, D_MODEL, EXPERT_FF), lambda b, be, nu: (be[last(b, be, nu)], 0, 0)),
                pl.BlockSpec((1, D_MODEL, EXPERT_FF), lambda b, be, nu: (be[last(b, be, nu)], 0, 0)),
                pl.BlockSpec((1, EXPERT_FF, D_MODEL), lambda b, be, nu: (be[last(b, be, nu)], 0, 0)),
            ],
            out_specs=pl.BlockSpec(blk, lambda b, be, nu: (b, 0)),
            scratch_shapes=[pltpu.VMEM((D_MODEL, EXPERT_FF), BF16),
                            pltpu.VMEM((D_MODEL, EXPERT_FF), BF16),
                            pltpu.VMEM((EXPERT_FF, D_MODEL), BF16)],
        ),
        out_shape=jax.ShapeDtypeStruct(xs.shape, U32),
        compiler_params=pltpu.CompilerParams(
            dimension_semantics=("arbitrary",), vmem_limit_bytes=VMEM_LIMIT),
        name="experts",
    )(block_e, n_used, xs, w_gate, w_up, w_down)


def _combine_body(dest_ref, h1_ref, w_ref, p_ref, ys_ref, shg_ref, shu_ref, shd_ref,
                  pg_ref, pp_ref, g_ref, b_ref, o_ref, buf, sem):
    tm = h1_ref.shape[0]

    def row_copy(kk, t, d):
        return pltpu.make_async_copy(ys_ref.at[_tile_of(d)], buf.at[_tile_of(kk * tm + t)], sem)

    def issue(t, c):
        for kk in range(TOP_K):
            row_copy(kk, t, dest_ref[t * TOP_K + kk]).start(priority=kk % 2)
        return c

    def drain(t, c):
        for kk in range(TOP_K):
            row_copy(kk, 0, 0).wait()
        return c

    lax.fori_loop(0, tm, issue, 0)

    h1 = h1_ref[...]
    hb = h1.astype(BF16)
    hid = (jax.nn.silu(jnp.dot(hb, shg_ref[...], preferred_element_type=F32))
           * jnp.dot(hb, shu_ref[...], preferred_element_type=F32)).astype(BF16)
    r = DEEPNORM_ALPHA * h1 + jnp.dot(hid, shd_ref[...], preferred_element_type=F32)
    emb = jnp.dot(p_ref[...].astype(BF16), pp_ref[...], preferred_element_type=F32)

    lax.fori_loop(0, tm, drain, 0)
    w = w_ref[...]
    cols = [r[:, c * LANES:(c + 1) * LANES] for c in range(2 * ROW_SUB)]
    for kk in range(TOP_K):
        wk = jnp.broadcast_to(w[:, kk:kk + 1], (tm, LANES))
        for c, chunk in enumerate(_unpack_rows(buf, tm, base=kk * tm)):
            cols[c] = cols[c] + wk * chunk
    r = jnp.concatenate(cols, axis=1)
    gate = jax.nn.sigmoid(jnp.dot(r.astype(BF16), pg_ref[...], preferred_element_type=F32))
    o_ref[...] = _layer_norm(r + gate * emb, g_ref[...], b_ref[...])


def _combine(dest_flat, h1, wtok, p2, ys, sh_g, sh_u, sh_d, ple_g, ple_p, l2g, l2b, *, tm):
    t = h1.shape[0]
    row = lambda w: pl.BlockSpec((tm, w), lambda i: (i, 0))
    return pl.pallas_call(
        _combine_body,
        grid=(t // tm,),
        in_specs=[
            pl.BlockSpec((tm * TOP_K,), lambda i: (i,), memory_space=pltpu.SMEM),
            row(D_MODEL), row(TOP_K), row(PLE_DIM),
            pl.BlockSpec(memory_space=pl.ANY),
            _const_spec(sh_g.shape), _const_spec(sh_u.shape), _const_spec(sh_d.shape),
            _const_spec(ple_g.shape), _const_spec(ple_p.shape),
            _const_spec((1, D_MODEL)), _const_spec((1, D_MODEL)),
        ],
        out_specs=row(D_MODEL),
        out_shape=jax.ShapeDtypeStruct((t, D_MODEL), F32),
        scratch_shapes=[pltpu.VMEM((TOP_K * tm * ROW_SUB, LANES), U32),
                        pltpu.SemaphoreType.DMA(())],
        compiler_params=pltpu.CompilerParams(
            dimension_semantics=("arbitrary",), vmem_limit_bytes=VMEM_LIMIT),
        name="combine",
    )(dest_flat, h1, wtok, p2, ys, sh_g, sh_u, sh_d, ple_g, ple_p, l2g, l2b)


def _block_layout(counts, n_blocks):
    padded = (counts + ROW_BLOCK - 1) // ROW_BLOCK * ROW_BLOCK
    pad_ends = jnp.cumsum(padded)
    pad_starts = pad_ends - padded
    block_row0 = jnp.arange(n_blocks, dtype=I32) * ROW_BLOCK
    block_e = jnp.minimum(jnp.sum(pad_ends[None, :] <= block_row0[:, None], axis=1),
                          N_EXPERTS - 1).astype(I32)
    n_used = (pad_ends[-1:] // ROW_BLOCK).astype(I32)
    has_pad = (counts % ROW_BLOCK) != 0
    zrow = jnp.where(has_pad, pad_ends - ROW_BLOCK, -1).astype(I32)
    return pad_starts.astype(I32), block_e, n_used, zrow


def kernel(x, p, ln_in_g, ln_in_b, w_in, sg_ln_g, sg_ln_b, sg_w_s, sg_b_s, w_branch_a, w_branch_b,
           w_out, ln1_g, ln1_b, router_w, router_bias, moe_w_gate, moe_w_up, moe_w_down,
           sh_w_gate, sh_w_up, sh_w_down, ple_w_proj, ple_w_gate, ln2_g, ln2_b):
    bsz, seq, dm = x.shape
    t = bsz * seq
    vec = lambda a: a.reshape(1, -1).astype(F32)
    assert DEPTH == 1
    h = x.reshape(t, dm)
    for i in range(DEPTH):
        q, k, v, ma, gb = _inproj(
            h, vec(ln_in_g), vec(ln_in_b), w_in[i].astype(BF16), vec(sg_ln_g[i]), vec(sg_ln_b[i]),
            sg_w_s[i], sg_b_s[i].T, w_branch_a[i].astype(BF16), tm=INPROJ_ROWS)
        yb = _attn(q.reshape(bsz, seq, SB_WIDTH), k.reshape(bsz, seq, SB_WIDTH),
                   v.reshape(bsz, seq, SB_WIDTH), tq=ATTN_ROWS).reshape(t, SB_WIDTH)

        rw_t = router_w[i].T
        rw_hi = rw_t.astype(BF16)
        rw_lo = (rw_t - rw_hi.astype(F32)).astype(BF16)
        h1, h1p, eidx, wsel, pos, cnt = _post(
            h, vec(ln_in_g), vec(ln_in_b), ma, gb, yb, w_branch_b[i].astype(BF16),
            w_out[i].astype(BF16), vec(ln1_g[i]), vec(ln1_b[i]), rw_hi, rw_lo,
            router_bias[i].reshape(N_EXPERTS, 1).astype(F32), tm=POST_ROWS)

        n_blocks = t * TOP_K // ROW_BLOCK + N_EXPERTS
        pad_starts, block_e, n_used, zrow = _block_layout(cnt[:, 0], n_blocks)
        dest = _slots(pad_starts, eidx, pos).T.reshape(t * TOP_K)

        xs = _dispatch(zrow, dest, h1p, n_blocks * ROW_BLOCK, tm=DISPATCH_ROWS)
        ys = _experts(block_e, n_used, xs, moe_w_gate[i], moe_w_up[i], moe_w_down[i])
        h = _combine(dest, h1, wsel.T, p[i].reshape(t, PLE_DIM), ys,
                     sh_w_gate[i].astype(BF16), sh_w_up[i].astype(BF16), sh_w_down[i].astype(BF16),
                     ple_w_gate[i].astype(BF16), ple_w_proj[i].astype(BF16),
                     vec(ln2_g[i]), vec(ln2_b[i]), tm=COMBINE_ROWS)
    return h.reshape(bsz, seq, dm)
```

```python
import functools
import math

import jax
import jax.numpy as jnp
from jax import lax
from jax.experimental import pallas as pl
from jax.experimental.pallas import tpu as pltpu

F32 = jnp.float32
BF16 = jnp.bfloat16
I32 = jnp.int32

D_MODEL = 1024
DEPTH = 1
CHUNK = 64
PLE_DIM = 256
SG_BLOCK = 128
SG_GROUPS = 4
SG_WIDTH = 512
SB_HEADS = 8
SB_HEAD_DIM = 64
SB_WIDTH = SB_HEADS * SB_HEAD_DIM
N_EXPERTS = 64
N_GROUPS = 8
GROUP_SIZE = N_EXPERTS // N_GROUPS
TOPK_GROUPS = 4
TOP_K = 8
EXPERT_FF = 256
SHARED_FF = 256
ROUTED_SCALE = 2.5
LN_EPS = 1e-5
DEEPNORM_ALPHA = (2.0 * DEPTH) ** 0.25

_OFF_U, _OFF_V, _OFF_Q, _OFF_K, _OFF_VS, _OFF_GA, _OFF_GB, _OFF_END = (
    0, 512, 1024, 1536, 2048, 2560, 3584, 4608)

LANES = 128
ROW_SUB = 4
assert 2 * ROW_SUB * LANES == D_MODEL
U32 = jnp.uint32
ATTN_LANES = 256
ATTN_HEADS = ATTN_LANES // SB_HEAD_DIM
EXP_UNDERFLOW = -110.0
ROW_BLOCK = 512
INPROJ_ROWS, POST_ROWS, ATTN_ROWS, DISPATCH_ROWS, COMBINE_ROWS = 512, 512, 256, 256, 256
VMEM_LIMIT = 56 * 1024 * 1024

_NT = (((1,), (1,)), ((), ()))


def _layer_norm(x, g, b):
    mu = jnp.mean(x, axis=-1, keepdims=True)
    xc = x - mu
    var = jnp.mean(xc * xc, axis=-1, keepdims=True)
    return xc * lax.rsqrt(var + LN_EPS) * g + b


def _unpack_rows(ref, n, base=0):
    lo, hi = [], []
    for s in range(ROW_SUB):
        w = ref[pl.ds(base * ROW_SUB + s, n, stride=ROW_SUB), :]
        lo.append(lax.bitcast_convert_type(w << 16, F32))
        hi.append(lax.bitcast_convert_type(w & jnp.uint32(0xFFFF0000), F32))
    return lo + hi


def _pack_rows(ref, val):
    n = val.shape[0]
    bits = lambda c: lax.bitcast_convert_type(
        val[:, c * LANES:(c + 1) * LANES].astype(BF16).astype(F32), jnp.uint32)
    for s in range(ROW_SUB):
        ref[pl.ds(s, n, stride=ROW_SUB), :] = bits(s + ROW_SUB) | (bits(s) >> 16)


def _const_spec(shape):
    nd = len(shape)
    return pl.BlockSpec(shape, lambda *_: (0,) * nd, pipeline_mode=pl.Buffered(1))


def _inproj_body(x_ref, g_ref, b_ref, w_ref, sgg_ref, sgb_ref, ws_ref, bs_ref, wba_ref,
                 q_ref, k_ref, v_ref, ma_ref, gb_ref, ya_scr):
    tm = x_ref.shape[0]
    hb = _layer_norm(x_ref[...], g_ref[...], b_ref[...]).astype(BF16)

    def proj(lo, hi):
        return jnp.dot(hb, w_ref[:, lo:hi], preferred_element_type=F32)

    q_ref[...] = (proj(_OFF_Q, _OFF_K) * (1.0 / math.sqrt(SB_HEAD_DIM))).astype(BF16)
    k_ref[...] = proj(_OFF_K, _OFF_VS).astype(BF16)
    v_ref[...] = proj(_OFF_VS, _OFF_GA).astype(BF16)
    gb_ref[...] = jax.nn.sigmoid(proj(_OFF_GB, _OFF_END)).astype(BF16)

    yu = jax.nn.gelu(proj(_OFF_U, _OFF_V))
    vn = _layer_norm(jax.nn.gelu(proj(_OFF_V, _OFF_Q)), sgg_ref[...], sgb_ref[...]).astype(BF16)

    pi = lax.broadcasted_iota(I32, (SG_BLOCK, SG_BLOCK), 0) // CHUNK
    pj = lax.broadcasted_iota(I32, (SG_BLOCK, SG_BLOCK), 1) // CHUNK
    chunk_causal = pj <= pi
    gch = SG_WIDTH // SG_GROUPS
    for g in range(SG_GROUPS):
        wm = jnp.where(chunk_causal, ws_ref[g], 0.0).astype(BF16)
        bias = bs_ref[:, g:g + 1]
        for r in range(tm // SG_BLOCK):
            rows = slice(r * SG_BLOCK, (r + 1) * SG_BLOCK)
            cols = slice(g * gch, (g + 1) * gch)
            s = jnp.dot(wm, vn[rows, cols], preferred_element_type=F32) + bias
            ya_scr[rows, cols] = (yu[rows, cols] * s).astype(BF16)

    ba = jnp.dot(ya_scr[...], wba_ref[...], preferred_element_type=F32)
    ma_ref[...] = (jax.nn.sigmoid(proj(_OFF_GA, _OFF_GB)) * ba).astype(BF16)


def _inproj(x2, ln_g, ln_b, w_in, sg_g, sg_b, w_s, b_sT, w_ba, *, tm):
    t = x2.shape[0]
    row = lambda w: pl.BlockSpec((tm, w), lambda i: (i, 0))
    return pl.pallas_call(
        _inproj_body,
        grid=(t // tm,),
        in_specs=[
            row(D_MODEL),
            _const_spec((1, D_MODEL)), _const_spec((1, D_MODEL)),
            _const_spec(w_in.shape),
            _const_spec((1, SG_WIDTH)), _const_spec((1, SG_WIDTH)),
            _const_spec(w_s.shape), _const_spec(b_sT.shape),
            _const_spec(w_ba.shape),
        ],
        out_specs=[row(SB_WIDTH), row(SB_WIDTH), row(SB_WIDTH), row(D_MODEL), row(D_MODEL)],
        out_shape=[
            jax.ShapeDtypeStruct((t, SB_WIDTH), BF16),
            jax.ShapeDtypeStruct((t, SB_WIDTH), BF16),
            jax.ShapeDtypeStruct((t, SB_WIDTH), BF16),
            jax.ShapeDtypeStruct((t, D_MODEL), BF16),
            jax.ShapeDtypeStruct((t, D_MODEL), BF16),
        ],
        scratch_shapes=[pltpu.VMEM((tm, SG_WIDTH), BF16)],
        compiler_params=pltpu.CompilerParams(
            dimension_semantics=("arbitrary",), vmem_limit_bytes=VMEM_LIMIT),
        name="inproj",
    )(x2, ln_g, ln_b, w_in, sg_g, sg_b, w_s, b_sT, w_ba)


def _attn_body(q_ref, k_ref, v_ref, o_ref, acc_scr, car_scr, *, tq):
    qi = pl.program_id(2)
    q2 = q_ref[0]
    lane_head = lax.broadcasted_iota(I32, (tq, ATTN_LANES), 1) // SB_HEAD_DIM
    zero = jnp.zeros_like(q2)
    in_head = [lane_head == h for h in range(ATTN_HEADS)]
    qn = [jnp.where(in_head[h], -q2, zero) for h in range(ATTN_HEADS)]
    rr = lax.broadcasted_iota(I32, (tq, tq), 0)
    cc = lax.broadcasted_iota(I32, (tq, tq), 1)
    later = (rr > cc).astype(BF16)
    causal = cc < rr
    sign = jnp.uint32(0x80000000)

    def blocks(starts, diag, rows=tq):
        kbs = [k_ref[0, pl.ds(pl.multiple_of(s0, tq), tq), :] for s0 in starts]
        vbs = [v_ref[0, pl.ds(pl.multiple_of(s0, tq), tq), :] for s0 in starts]
        for h in range(ATTN_HEADS):
            car = car_scr[h, :rows]
            pv = None
            for n, (kb, vb) in enumerate(zip(kbs, vbs)):
                masked = diag and n == 0
                y = lax.dot_general(qn[h][:rows], kb, _NT, preferred_element_type=F32)
                neg_abs = lax.bitcast_convert_type(
                    lax.bitcast_convert_type(y, jnp.uint32) | sign, F32)
                lm = jnp.minimum(y, 0.0) - jnp.log(1.0 + jnp.exp(neg_abs))
                lb = lm - y
                if masked:
                    lm = jnp.where(causal, lm, 0.0)
                tail = jnp.dot(lm.astype(BF16), later, preferred_element_type=F32)
                a = jnp.exp(lb + tail + car)
                if masked:
                    a = jnp.where(causal, a, 0.0)
                d = jnp.dot(a.astype(BF16), vb, preferred_element_type=F32)
                pv = d if pv is None else pv + d
                car = car + jnp.sum(lm, axis=1, keepdims=True)
            acc_scr[h, :rows] += pv
            car_scr[h, :rows] = car

    acc_scr[...] = jnp.zeros_like(acc_scr)
    car_scr[...] = jnp.zeros_like(car_scr)
    blocks([qi * tq], True)

    half = tq // 2

    def live(lo, hi):
        top = car_scr[0, lo:hi]
        for h in range(1, ATTN_HEADS):
            top = jnp.maximum(top, car_scr[h, lo:hi])
        return (jnp.max(top) > EXP_UNDERFLOW).astype(I32)

    def earlier(state):
        j, _ = state
        s0 = (qi - 1 - j) * tq
        lower = live(half, tq)

        @pl.when(lower > 0)
        def _():
            blocks([s0], False)

        @pl.when(lower == 0)
        def _():
            blocks([s0], False, rows=half)

        return j + 1, live(0, tq)

    lax.while_loop(lambda s: (s[0] < qi) & (s[1] > 0), earlier, (jnp.int32(0), live(0, tq)))
    out = acc_scr[0]
    for h in range(1, ATTN_HEADS):
        out = jnp.where(in_head[h], acc_scr[h], out)
    o_ref[0] = out.astype(BF16)


def _attn(q3, k3, v3, *, tq):
    b, s, _ = q3.shape
    qspec = pl.BlockSpec((1, tq, ATTN_LANES), lambda bi, hp, qi: (bi, qi, hp))
    kvspec = pl.BlockSpec((1, s, ATTN_LANES), lambda bi, hp, qi: (bi, 0, hp))
    return pl.pallas_call(
        functools.partial(_attn_body, tq=tq),
        grid=(b, SB_WIDTH // ATTN_LANES, s // tq),
        in_specs=[qspec, kvspec, kvspec],
        out_specs=qspec,
        out_shape=jax.ShapeDtypeStruct((b, s, SB_WIDTH), BF16),
        scratch_shapes=[pltpu.VMEM((ATTN_HEADS, tq, ATTN_LANES), F32),
                        pltpu.VMEM((ATTN_HEADS, tq, 1), F32)],
        compiler_params=pltpu.CompilerParams(
            dimension_semantics=("arbitrary", "arbitrary", "arbitrary"),
            vmem_limit_bytes=VMEM_LIMIT),
        name="attn",
    )(q3, k3, v3)


def _post_body(x_ref, g_ref, b_ref, ma_ref, gb_ref, yb_ref, wbb_ref, wout_ref, l1g_ref, l1b_ref,
               rwh_ref, rwl_ref, rb_ref,
               h1_ref, h1p_ref, eidx_ref, wsel_ref, pos_ref, cnt_ref, cnt_scr):
    tm = x_ref.shape[0]
    step = pl.program_id(0)

    @pl.when(step == 0)
    def _():
        cnt_scr[...] = jnp.zeros_like(cnt_scr)

    h = _layer_norm(x_ref[...], g_ref[...], b_ref[...])
    bb = jnp.dot(yb_ref[...], wbb_ref[...], preferred_element_type=F32)
    merged = ma_ref[...].astype(F32) + gb_ref[...].astype(F32) * bb
    o = jnp.dot(merged.astype(BF16), wout_ref[...], preferred_element_type=F32)
    h1 = _layer_norm(DEEPNORM_ALPHA * h + o, l1g_ref[...], l1b_ref[...])
    h1_ref[...] = h1
    _pack_rows(h1p_ref, h1)

    hh = h1.astype(BF16)
    hl = (h1 - hh.astype(F32)).astype(BF16)
    logits = (lax.dot_general(rwh_ref[...], hh, _NT, preferred_element_type=F32)
              + lax.dot_general(rwh_ref[...], hl, _NT, preferred_element_type=F32)
              + lax.dot_general(rwl_ref[...], hh, _NT, preferred_element_type=F32))
    scores = jax.nn.sigmoid(logits)
    sel = scores + rb_ref[...]

    sub = lax.broadcasted_iota(I32, (GROUP_SIZE, tm), 0).astype(F32)
    neg = jnp.float32(-jnp.inf)
    selg = [sel[g * GROUP_SIZE:(g + 1) * GROUP_SIZE] for g in range(N_GROUPS)]
    scg = [scores[g * GROUP_SIZE:(g + 1) * GROUP_SIZE] for g in range(N_GROUPS)]
    eid = [sub + float(g * GROUP_SIZE) for g in range(N_GROUPS)]

    grp = []
    for g in range(N_GROUPS):
        m1 = jnp.max(selg[g], axis=0, keepdims=True)
        i1 = jnp.min(jnp.where(selg[g] == m1, sub, float(GROUP_SIZE)), axis=0, keepdims=True)
        m2 = jnp.max(jnp.where(sub == i1, neg, selg[g]), axis=0, keepdims=True)
        grp.append(m1 + m2)
    cur = []
    for g in range(N_GROUPS):
        beaten = jnp.zeros((1, tm), F32)
        for o_g in range(N_GROUPS):
            if o_g == g:
                continue
            wins = (grp[o_g] > grp[g]) if o_g > g else (grp[o_g] >= grp[g])
            beaten = beaten + jnp.where(wins, 1.0, 0.0)
        cur.append(jnp.where(beaten < float(TOPK_GROUPS), selg[g], neg))

    def all_reduce(vals, op):
        acc = vals[0]
        for v in vals[1:]:
            acc = op(acc, v)
        return acc

    idx_rows, w_rows = [], []
    member = [jnp.zeros((GROUP_SIZE, tm), F32) for _ in range(N_GROUPS)]
    for _ in range(TOP_K):
        m = jnp.max(all_reduce(cur, jnp.maximum), axis=0, keepdims=True)
        cand = [jnp.where(cur[g] == m, eid[g], float(N_EXPERTS)) for g in range(N_GROUPS)]
        idx = jnp.min(all_reduce(cand, jnp.minimum), axis=0, keepdims=True)
        hit = [eid[g] == idx for g in range(N_GROUPS)]
        w = jnp.sum(all_reduce([jnp.where(hit[g], scg[g], 0.0) for g in range(N_GROUPS)], jnp.add),
                    axis=0, keepdims=True)
        cur = [jnp.where(hit[g], neg, cur[g]) for g in range(N_GROUPS)]
        member = [jnp.where(hit[g], 1.0, member[g]) for g in range(N_GROUPS)]
        idx_rows.append(idx)
        w_rows.append(w)

    wsum = all_reduce(w_rows, jnp.add)
    for kk in range(TOP_K):
        eidx_ref[kk:kk + 1, :] = idx_rows[kk].astype(I32)
        wsel_ref[kk:kk + 1, :] = w_rows[kk] / wsum * ROUTED_SCALE

    tt = lax.broadcasted_iota(I32, (tm, tm), 0)
    uu = lax.broadcasted_iota(I32, (tm, tm), 1)
    before = (tt < uu).astype(BF16)
    memb = jnp.concatenate(member, axis=0)
    prefix = jnp.dot(memb.astype(BF16), before, preferred_element_type=F32) + cnt_scr[...]
    for kk in range(TOP_K):
        picked = [jnp.where(eid[g] == idx_rows[kk],
                            prefix[g * GROUP_SIZE:(g + 1) * GROUP_SIZE], 0.0) for g in range(N_GROUPS)]
        pos_ref[kk:kk + 1, :] = jnp.sum(all_reduce(picked, jnp.add), axis=0, keepdims=True).astype(I32)
    total = cnt_scr[...] + jnp.sum(memb, axis=1, keepdims=True)
    cnt_scr[...] = total
    cnt_ref[...] = jnp.broadcast_to(total, cnt_ref.shape).astype(I32)


def _post(x2, ln_g, ln_b, ma, gb, yb, w_bb, w_out, l1g, l1b, rw_hi, rw_lo, r_bias, *, tm):
    t = x2.shape[0]
    row = lambda w: pl.BlockSpec((tm, w), lambda i: (i, 0))
    tok = pl.BlockSpec((TOP_K, tm), lambda i: (0, i))
    return pl.pallas_call(
        _post_body,
        grid=(t // tm,),
        in_specs=[
            row(D_MODEL), _const_spec((1, D_MODEL)), _const_spec((1, D_MODEL)),
            row(D_MODEL), row(D_MODEL), row(SB_WIDTH),
            _const_spec(w_bb.shape), _const_spec(w_out.shape),
            _const_spec((1, D_MODEL)), _const_spec((1, D_MODEL)),
            _const_spec(rw_hi.shape), _const_spec(rw_lo.shape), _const_spec(r_bias.shape),
        ],
        out_specs=[row(D_MODEL), pl.BlockSpec((tm * ROW_SUB, LANES), lambda i: (i, 0)),
                   tok, tok, tok, pl.BlockSpec((N_EXPERTS, LANES), lambda i: (0, 0))],
        out_shape=[
            jax.ShapeDtypeStruct((t, D_MODEL), F32),
            jax.ShapeDtypeStruct((t * ROW_SUB, LANES), U32),
            jax.ShapeDtypeStruct((TOP_K, t), I32),
            jax.ShapeDtypeStruct((TOP_K, t), F32),
            jax.ShapeDtypeStruct((TOP_K, t), I32),
            jax.ShapeDtypeStruct((N_EXPERTS, LANES), I32),
        ],
        scratch_shapes=[pltpu.VMEM((N_EXPERTS, 1), F32)],
        compiler_params=pltpu.CompilerParams(
            dimension_semantics=("arbitrary",), vmem_limit_bytes=VMEM_LIMIT),
        name="post",
    )(x2, ln_g, ln_b, ma, gb, yb, w_bb, w_out, l1g, l1b, rw_hi, rw_lo, r_bias)


def _slots_body(ps_ref, eidx_ref, pos_ref, dest_ref):
    eidx = eidx_ref[...]
    dest = pos_ref[...]
    for e in range(N_EXPERTS):
        dest = dest + jnp.where(eidx == e, ps_ref[e], 0)
    dest_ref[...] = dest


def _slots(pad_starts, eidx, pos):
    spec = pl.BlockSpec(eidx.shape, lambda i, ps: (0, 0))
    return pl.pallas_call(
        _slots_body,
        grid_spec=pltpu.PrefetchScalarGridSpec(
            num_scalar_prefetch=1, grid=(1,), in_specs=[spec, spec], out_specs=spec),
        out_shape=jax.ShapeDtypeStruct(eidx.shape, I32),
        name="slots",
    )(pad_starts, eidx, pos)


def _tile_of(r):
    return pl.ds(pl.multiple_of(r * ROW_SUB, ROW_SUB), ROW_SUB)


def _dispatch_body(zrow_ref, dest_ref, h_ref, xs_ref, zero_scr, zsem, sem):
    tm = h_ref.shape[0] // ROW_SUB
    step = pl.program_id(0)

    def zero_copy(r):
        rows = pl.ds(pl.multiple_of(r * ROW_SUB, ROW_BLOCK * ROW_SUB), ROW_BLOCK * ROW_SUB)
        return pltpu.make_async_copy(zero_scr, xs_ref.at[rows], zsem)

    def row_copy(t, d):
        return pltpu.make_async_copy(h_ref.at[_tile_of(t)], xs_ref.at[_tile_of(d)], sem)

    @pl.when(step == 0)
    def _():
        zero_scr[...] = jnp.zeros_like(zero_scr)

        def start(e, c):
            r = zrow_ref[e]

            @pl.when(r >= 0)
            def _():
                zero_copy(jnp.maximum(r, 0)).start()
            return c

        def wait(e, c):
            @pl.when(zrow_ref[e] >= 0)
            def _():
                zero_copy(0).wait()
            return c

        lax.fori_loop(0, N_EXPERTS, start, 0)
        lax.fori_loop(0, N_EXPERTS, wait, 0)

    def issue(t, c):
        for kk in range(TOP_K):
            row_copy(t, dest_ref[t * TOP_K + kk]).start(priority=kk % 2)
        return c

    def drain(t, c):
        for kk in range(TOP_K):
            row_copy(0, 0).wait()
        return c

    lax.fori_loop(0, tm, issue, 0)
    lax.fori_loop(0, tm, drain, 0)


def _dispatch(zrow, dest_flat, h1t, n_rows, *, tm):
    t = h1t.shape[0] // ROW_SUB
    return pl.pallas_call(
        _dispatch_body,
        grid_spec=pltpu.PrefetchScalarGridSpec(
            num_scalar_prefetch=1,
            grid=(t // tm,),
            in_specs=[
                pl.BlockSpec((tm * TOP_K,), lambda i, z: (i,), memory_space=pltpu.SMEM),
                pl.BlockSpec((tm * ROW_SUB, LANES), lambda i, z: (i, 0)),
            ],
            out_specs=pl.BlockSpec(memory_space=pl.ANY),
            scratch_shapes=[pltpu.VMEM((ROW_BLOCK * ROW_SUB, LANES), U32),
                            pltpu.SemaphoreType.DMA(()), pltpu.SemaphoreType.DMA(())],
        ),
        out_shape=jax.ShapeDtypeStruct((n_rows * ROW_SUB, LANES), U32),
        compiler_params=pltpu.CompilerParams(
            dimension_semantics=("arbitrary",), vmem_limit_bytes=VMEM_LIMIT),
        name="dispatch",
    )(zrow, dest_flat, h1t)


def _experts_body(be_ref, nu_ref, xs_ref, wg_ref, wu_ref, wd_ref, ys_ref, wg_scr, wu_scr, wd_scr):
    blk = pl.program_id(0)
    prev = jnp.maximum(blk - 1, 0)

    @pl.when((blk < nu_ref[0]) & ((blk == 0) | (be_ref[blk] != be_ref[prev])))
    def _():
        wg_scr[...] = wg_ref[0].astype(BF16)
        wu_scr[...] = wu_ref[0].astype(BF16)
        wd_scr[...] = wd_ref[0].astype(BF16)

    @pl.when(blk < nu_ref[0])
    def _():
        xb = jnp.concatenate(_unpack_rows(xs_ref, ROW_BLOCK), axis=1).astype(BF16)
        gate = jnp.dot(xb, wg_scr[...], preferred_element_type=F32)
        up = jnp.dot(xb, wu_scr[...], preferred_element_type=F32)
        hid = (jax.nn.silu(gate) * up).astype(BF16)
        _pack_rows(ys_ref, jnp.dot(hid, wd_scr[...], preferred_element_type=F32))

    @pl.when(blk >= nu_ref[0])
    def _():
        ys_ref[...] = jnp.zeros_like(ys_ref)


def _experts(block_e, n_used, xs, w_gate, w_up, w_down):
    n_blocks = xs.shape[0] // (ROW_BLOCK * ROW_SUB)
    last = lambda b, be, nu: jnp.minimum(b, nu[0] - 1)
    blk = (ROW_BLOCK * ROW_SUB, LANES)
    return pl.pallas_call(
        _experts_body,
        grid_spec=pltpu.PrefetchScalarGridSpec(
            num_scalar_prefetch=2,
            grid=(n_blocks,),
            in_specs=[
                pl.BlockSpec(blk, lambda b, be, nu: (last(b, be, nu), 0)),
                pl.BlockSpec((1, D_MODEL, EXPERT_FF), lambda b, be, nu: (be[last(b, be, nu)], 0, 0)),
                pl.BlockSpec((1, D_MODEL, EXPERT_FF), lambda b, be, nu: (be[last(b, be, nu)], 0, 0)),
                pl.BlockSpec((1, EXPERT_FF, D_MODEL), lambda b, be, nu: (be[last(b, be, nu)], 0, 0)),
            ],
            out_specs=pl.BlockSpec(blk, lambda b, be, nu: (b, 0)),
            scratch_shapes=[pltpu.VMEM((D_MODEL, EXPERT_FF), BF16),
                            pltpu.VMEM((D_MODEL, EXPERT_FF), BF16),
                            pltpu.VMEM((EXPERT_FF, D_MODEL), BF16)],
        ),
        out_shape=jax.ShapeDtypeStruct(xs.shape, U32),
        compiler_params=pltpu.CompilerParams(
            dimension_semantics=("arbitrary",), vmem_limit_bytes=VMEM_LIMIT),
        name="experts",
    )(block_e, n_used, xs, w_gate, w_up, w_down)


def _combine_body(dest_ref, h1_ref, w_ref, p_ref, ys_ref, shg_ref, shu_ref, shd_ref,
                  pg_ref, pp_ref, g_ref, b_ref, o_ref, buf, sem):
    tm = h1_ref.shape[0]

    def row_copy(kk, t, d):
        return pltpu.make_async_copy(ys_ref.at[_tile_of(d)], buf.at[_tile_of(kk * tm + t)], sem)

    def issue(t, c):
        for kk in range(TOP_K):
            row_copy(kk, t, dest_ref[t * TOP_K + kk]).start(priority=kk % 2)
        return c

    def drain(t, c):
        for kk in range(TOP_K):
            row_copy(kk, 0, 0).wait()
        return c

    lax.fori_loop(0, tm, issue, 0)

    h1 = h1_ref[...]
    hb = h1.astype(BF16)
    hid = (jax.nn.silu(jnp.dot(hb, shg_ref[...], preferred_element_type=F32))
           * jnp.dot(hb, shu_ref[...], preferred_element_type=F32)).astype(BF16)
    r = DEEPNORM_ALPHA * h1 + jnp.dot(hid, shd_ref[...], preferred_element_type=F32)
    emb = jnp.dot(p_ref[...].astype(BF16), pp_ref[...], preferred_element_type=F32)

    lax.fori_loop(0, tm, drain, 0)
    w = w_ref[...]
    cols = [r[:, c * LANES:(c + 1) * LANES] for c in range(2 * ROW_SUB)]
    for kk in range(TOP_K):
        wk = jnp.broadcast_to(w[:, kk:kk + 1], (tm, LANES))
        for c, chunk in enumerate(_unpack_rows(buf, tm, base=kk * tm)):
            cols[c] = cols[c] + wk * chunk
    r = jnp.concatenate(cols, axis=1)
    gate = jax.nn.sigmoid(jnp.dot(r.astype(BF16), pg_ref[...], preferred_element_type=F32))
    o_ref[...] = _layer_norm(r + gate * emb, g_ref[...], b_ref[...])


def _combine(dest_flat, h1, wtok, p2, ys, sh_g, sh_u, sh_d, ple_g, ple_p, l2g, l2b, *, tm):
    t = h1.shape[0]
    row = lambda w: pl.BlockSpec((tm, w), lambda i: (i, 0))
    return pl.pallas_call(
        _combine_body,
        grid=(t // tm,),
        in_specs=[
            pl.BlockSpec((tm * TOP_K,), lambda i: (i,), memory_space=pltpu.SMEM),
            row(D_MODEL), row(TOP_K), row(PLE_DIM),
            pl.BlockSpec(memory_space=pl.ANY),
            _const_spec(sh_g.shape), _const_spec(sh_u.shape), _const_spec(sh_d.shape),
            _const_spec(ple_g.shape), _const_spec(ple_p.shape),
            _const_spec((1, D_MODEL)), _const_spec((1, D_MODEL)),
        ],
        out_specs=row(D_MODEL),
        out_shape=jax.ShapeDtypeStruct((t, D_MODEL), F32),
        scratch_shapes=[pltpu.VMEM((TOP_K * tm * ROW_SUB, LANES), U32),
                        pltpu.SemaphoreType.DMA(())],
        compiler_params=pltpu.CompilerParams(
            dimension_semantics=("arbitrary",), vmem_limit_bytes=VMEM_LIMIT),
        name="combine",
    )(dest_flat, h1, wtok, p2, ys, sh_g, sh_u, sh_d, ple_g, ple_p, l2g, l2b)


def _block_layout(counts, n_blocks):
    padded = (counts + ROW_BLOCK - 1) // ROW_BLOCK * ROW_BLOCK
    pad_ends = jnp.cumsum(padded)
    pad_starts = pad_ends - padded
    block_row0 = jnp.arange(n_blocks, dtype=I32) * ROW_BLOCK
    block_e = jnp.minimum(jnp.sum(pad_ends[None, :] <= block_row0[:, None], axis=1),
                          N_EXPERTS - 1).astype(I32)
    n_used = (pad_ends[-1:] // ROW_BLOCK).astype(I32)
    has_pad = (counts % ROW_BLOCK) != 0
    zrow = jnp.where(has_pad, pad_ends - ROW_BLOCK, -1).astype(I32)
    return pad_starts.astype(I32), block_e, n_used, zrow


def kernel(x, p, ln_in_g, ln_in_b, w_in, sg_ln_g, sg_ln_b, sg_w_s, sg_b_s, w_branch_a, w_branch_b,
           w_out, ln1_g, ln1_b, router_w, router_bias, moe_w_gate, moe_w_up, moe_w_down,
           sh_w_gate, sh_w_up, sh_w_down, ple_w_proj, ple_w_gate, ln2_g, ln2_b):
    bsz, seq, dm = x.shape
    t = bsz * seq
    vec = lambda a: a.reshape(1, -1).astype(F32)
    assert DEPTH == 1
    h = x.reshape(t, dm)
    for i in range(DEPTH):
        q, k, v, ma, gb = _inproj(
            h, vec(ln_in_g), vec(ln_in_b), w_in[i].astype(BF16), vec(sg_ln_g[i]), vec(sg_ln_b[i]),
            sg_w_s[i], sg_b_s[i].T, w_branch_a[i].astype(BF16), tm=INPROJ_ROWS)
        yb = _attn(q.reshape(bsz, seq, SB_WIDTH), k.reshape(bsz, seq, SB_WIDTH),
                   v.reshape(bsz, seq, SB_WIDTH), tq=ATTN_ROWS).reshape(t, SB_WIDTH)

        rw_t = router_w[i].T
        rw_hi = rw_t.astype(BF16)
        rw_lo = (rw_t - rw_hi.astype(F32)).astype(BF16)
        h1, h1p, eidx, wsel, pos, cnt = _post(
            h, vec(ln_in_g), vec(ln_in_b), ma, gb, yb, w_branch_b[i].astype(BF16),
            w_out[i].astype(BF16), vec(ln1_g[i]), vec(ln1_b[i]), rw_hi, rw_lo,
            router_bias[i].reshape(N_EXPERTS, 1).astype(F32), tm=POST_ROWS)

        n_blocks = t * TOP_K // ROW_BLOCK + N_EXPERTS
        pad_starts, block_e, n_used, zrow = _block_layout(cnt[:, 0], n_blocks)
        dest = _slots(pad_starts, eidx, pos).T.reshape(t * TOP_K)

        xs = _dispatch(zrow, dest, h1p, n_blocks * ROW_BLOCK, tm=DISPATCH_ROWS)
        ys = _experts(block_e, n_used, xs, moe_w_gate[i], moe_w_up[i], moe_w_down[i])
        h = _combine(dest, h1, wsel.T, p[i].reshape(t, PLE_DIM), ys,
                     sh_w_gate[i].astype(BF16), sh_w_up[i].astype(BF16), sh_w_down[i].astype(BF16),
                     ple_w_gate[i].astype(BF16), ple_w_proj[i].astype(BF16),
                     vec(ln2_g[i]), vec(ln2_b[i]), tm=COMBINE_ROWS)
    return h.reshape(bsz, seq, dm)
```

```python
import functools
import math

import jax
import jax.numpy as jnp
from jax import lax
from jax.experimental import pallas as pl
from jax.experimental.pallas import tpu as pltpu

F32 = jnp.float32
BF16 = jnp.bfloat16
I32 = jnp.int32

D_MODEL = 1024
DEPTH = 1
CHUNK = 64
PLE_DIM = 256
SG_BLOCK = 128
SG_GROUPS = 4
SG_WIDTH = 512
SB_HEADS = 8
SB_HEAD_DIM = 64
SB_WIDTH = SB_HEADS * SB_HEAD_DIM
N_EXPERTS = 64
N_GROUPS = 8
GROUP_SIZE = N_EXPERTS // N_GROUPS
TOPK_GROUPS = 4
TOP_K = 8
EXPERT_FF = 256
SHARED_FF = 256
ROUTED_SCALE = 2.5
LN_EPS = 1e-5
DEEPNORM_ALPHA = (2.0 * DEPTH) ** 0.25

_OFF_U, _OFF_V, _OFF_Q, _OFF_K, _OFF_VS, _OFF_GA, _OFF_GB, _OFF_END = (
    0, 512, 1024, 1536, 2048, 2560, 3584, 4608)

LANES = 128
ROW_SUB = 4
assert 2 * ROW_SUB * LANES == D_MODEL
U32 = jnp.uint32
ATTN_LANES = 256
ATTN_HEADS = ATTN_LANES // SB_HEAD_DIM
EXP_UNDERFLOW = -110.0
ROW_BLOCK = 512
INPROJ_ROWS, POST_ROWS, ATTN_ROWS, DISPATCH_ROWS, COMBINE_ROWS = 512, 512, 256, 256, 256
VMEM_LIMIT = 56 * 1024 * 1024

_NT = (((1,), (1,)), ((), ()))


def _layer_norm(x, g, b):
    mu = jnp.mean(x, axis=-1, keepdims=True)
    xc = x - mu
    var = jnp.mean(xc * xc, axis=-1, keepdims=True)
    return xc * lax.rsqrt(var + LN_EPS) * g + b


def _unpack_rows(ref, n, base=0):
    lo, hi = [], []
    for s in range(ROW_SUB):
        w = ref[pl.ds(base * ROW_SUB + s, n, stride=ROW_SUB), :]
        lo.append(lax.bitcast_convert_type(w << 16, F32))
        hi.append(lax.bitcast_convert_type(w & jnp.uint32(0xFFFF0000), F32))
    return lo + hi


def _pack_rows(ref, val):
    n = val.shape[0]
    bits = lambda c: lax.bitcast_convert_type(
        val[:, c * LANES:(c + 1) * LANES].astype(BF16).astype(F32), jnp.uint32)
    for s in range(ROW_SUB):
        ref[pl.ds(s, n, stride=ROW_SUB), :] = bits(s + ROW_SUB) | (bits(s) >> 16)


def _const_spec(shape):
    nd = len(shape)
    return pl.BlockSpec(shape, lambda *_: (0,) * nd, pipeline_mode=pl.Buffered(1))


def _inproj_body(x_ref, g_ref, b_ref, w_ref, sgg_ref, sgb_ref, ws_ref, bs_ref, wba_ref,
                 q_ref, k_ref, v_ref, ma_ref, gb_ref, ya_scr):
    tm = x_ref.shape[0]
    hb = _layer_norm(x_ref[...], g_ref[...], b_ref[...]).astype(BF16)

    def proj(lo, hi):
        return jnp.dot(hb, w_ref[:, lo:hi], preferred_element_type=F32)

    q_ref[...] = (proj(_OFF_Q, _OFF_K) * (1.0 / math.sqrt(SB_HEAD_DIM))).astype(BF16)
    k_ref[...] = proj(_OFF_K, _OFF_VS).astype(BF16)
    v_ref[...] = proj(_OFF_VS, _OFF_GA).astype(BF16)
    gb_ref[...] = jax.nn.sigmoid(proj(_OFF_GB, _OFF_END)).astype(BF16)

    yu = jax.nn.gelu(proj(_OFF_U, _OFF_V))
    vn = _layer_norm(jax.nn.gelu(proj(_OFF_V, _OFF_Q)), sgg_ref[...], sgb_ref[...]).astype(BF16)

    pi = lax.broadcasted_iota(I32, (SG_BLOCK, SG_BLOCK), 0) // CHUNK
    pj = lax.broadcasted_iota(I32, (SG_BLOCK, SG_BLOCK), 1) // CHUNK
    chunk_causal = pj <= pi
    gch = SG_WIDTH // SG_GROUPS
    for g in range(SG_GROUPS):
        wm = jnp.where(chunk_causal, ws_ref[g], 0.0).astype(BF16)
        bias = bs_ref[:, g:g + 1]
        for r in range(tm // SG_BLOCK):
            rows = slice(r * SG_BLOCK, (r + 1) * SG_BLOCK)
            cols = slice(g * gch, (g + 1) * gch)
            s = jnp.dot(wm, vn[rows, cols], preferred_element_type=F32) + bias
            ya_scr[rows, cols] = (yu[rows, cols] * s).astype(BF16)

    ba = jnp.dot(ya_scr[...], wba_ref[...], preferred_element_type=F32)
    ma_ref[...] = (jax.nn.sigmoid(proj(_OFF_GA, _OFF_GB)) * ba).astype(BF16)


def _inproj(x2, ln_g, ln_b, w_in, sg_g, sg_b, w_s, b_sT, w_ba, *, tm):
    t = x2.shape[0]
    row = lambda w: pl.BlockSpec((tm, w), lambda i: (i, 0))
    return pl.pallas_call(
        _inproj_body,
        grid=(t // tm,),
        in_specs=[
            row(D_MODEL),
            _const_spec((1, D_MODEL)), _const_spec((1, D_MODEL)),
            _const_spec(w_in.shape),
            _const_spec((1, SG_WIDTH)), _const_spec((1, SG_WIDTH)),
            _const_spec(w_s.shape), _const_spec(b_sT.shape),
            _const_spec(w_ba.shape),
        ],
        out_specs=[row(SB_WIDTH), row(SB_WIDTH), row(SB_WIDTH), row(D_MODEL), row(D_MODEL)],
        out_shape=[
            jax.ShapeDtypeStruct((t, SB_WIDTH), BF16),
            jax.ShapeDtypeStruct((t, SB_WIDTH), BF16),
            jax.ShapeDtypeStruct((t, SB_WIDTH), BF16),
            jax.ShapeDtypeStruct((t, D_MODEL), BF16),
            jax.ShapeDtypeStruct((t, D_MODEL), BF16),
        ],
        scratch_shapes=[pltpu.VMEM((tm, SG_WIDTH), BF16)],
        compiler_params=pltpu.CompilerParams(
            dimension_semantics=("arbitrary",), vmem_limit_bytes=VMEM_LIMIT),
        name="inproj",
    )(x2, ln_g, ln_b, w_in, sg_g, sg_b, w_s, b_sT, w_ba)


def _attn_body(q_ref, k_ref, v_ref, o_ref, acc_scr, car_scr, *, tq):
    qi = pl.program_id(2)
    q2 = q_ref[0]
    lane_head = lax.broadcasted_iota(I32, (tq, ATTN_LANES), 1) // SB_HEAD_DIM
    zero = jnp.zeros_like(q2)
    in_head = [lane_head == h for h in range(ATTN_HEADS)]
    qn = [jnp.where(in_head[h], -q2, zero) for h in range(ATTN_HEADS)]
    rr = lax.broadcasted_iota(I32, (tq, tq), 0)
    cc = lax.broadcasted_iota(I32, (tq, tq), 1)
    later = (rr > cc).astype(BF16)
    causal = cc < rr
    sign = jnp.uint32(0x80000000)

    def blocks(starts, diag):
        kbs = [k_ref[0, pl.ds(pl.multiple_of(s0, tq), tq), :] for s0 in starts]
        vbs = [v_ref[0, pl.ds(pl.multiple_of(s0, tq), tq), :] for s0 in starts]
        for h in range(ATTN_HEADS):
            car = car_scr[h]
            pv = None
            for n, (kb, vb) in enumerate(zip(kbs, vbs)):
                masked = diag and n == 0
                y = lax.dot_general(qn[h], kb, _NT, preferred_element_type=F32)
                neg_abs = lax.bitcast_convert_type(
                    lax.bitcast_convert_type(y, jnp.uint32) | sign, F32)
                lm = jnp.minimum(y, 0.0) - jnp.log(1.0 + jnp.exp(neg_abs))
                lb = lm - y
                if masked:
                    lm = jnp.where(causal, lm, 0.0)
                tail = jnp.dot(lm.astype(BF16), later, preferred_element_type=F32)
                a = jnp.exp(lb + tail + car)
                if masked:
                    a = jnp.where(causal, a, 0.0)
                d = jnp.dot(a.astype(BF16), vb, preferred_element_type=F32)
                pv = d if pv is None else pv + d
                car = car + jnp.sum(lm, axis=1, keepdims=True)
            acc_scr[h] += pv
            car_scr[h] = car

    acc_scr[...] = jnp.zeros_like(acc_scr)
    car_scr[...] = jnp.zeros_like(car_scr)
    blocks([qi * tq], True)

    def live():
        top = car_scr[0]
        for h in range(1, ATTN_HEADS):
            top = jnp.maximum(top, car_scr[h])
        return (jnp.max(top) > EXP_UNDERFLOW).astype(I32)

    def earlier(state):
        j, _ = state
        blocks([(qi - 1 - j) * tq], False)
        return j + 1, live()

    lax.while_loop(lambda s: (s[0] < qi) & (s[1] > 0), earlier, (jnp.int32(0), live()))
    out = acc_scr[0]
    for h in range(1, ATTN_HEADS):
        out = jnp.where(in_head[h], acc_scr[h], out)
    o_ref[0] = out.astype(BF16)


def _attn(q3, k3, v3, *, tq):
    b, s, _ = q3.shape
    qspec = pl.BlockSpec((1, tq, ATTN_LANES), lambda bi, hp, qi: (bi, qi, hp))
    kvspec = pl.BlockSpec((1, s, ATTN_LANES), lambda bi, hp, qi: (bi, 0, hp))
    return pl.pallas_call(
        functools.partial(_attn_body, tq=tq),
        grid=(b, SB_WIDTH // ATTN_LANES, s // tq),
        in_specs=[qspec, kvspec, kvspec],
        out_specs=qspec,
        out_shape=jax.ShapeDtypeStruct((b, s, SB_WIDTH), BF16),
        scratch_shapes=[pltpu.VMEM((ATTN_HEADS, tq, ATTN_LANES), F32),
                        pltpu.VMEM((ATTN_HEADS, tq, 1), F32)],
        compiler_params=pltpu.CompilerParams(
            dimension_semantics=("arbitrary", "arbitrary", "arbitrary"),
            vmem_limit_bytes=VMEM_LIMIT),
        name="attn",
    )(q3, k3, v3)


def _post_body(x_ref, g_ref, b_ref, ma_ref, gb_ref, yb_ref, wbb_ref, wout_ref, l1g_ref, l1b_ref,
               rwh_ref, rwl_ref, rb_ref,
               h1_ref, h1p_ref, eidx_ref, wsel_ref, pos_ref, cnt_ref, cnt_scr):
    tm = x_ref.shape[0]
    step = pl.program_id(0)

    @pl.when(step == 0)
    def _():
        cnt_scr[...] = jnp.zeros_like(cnt_scr)

    h = _layer_norm(x_ref[...], g_ref[...], b_ref[...])
    bb = jnp.dot(yb_ref[...], wbb_ref[...], preferred_element_type=F32)
    merged = ma_ref[...].astype(F32) + gb_ref[...].astype(F32) * bb
    o = jnp.dot(merged.astype(BF16), wout_ref[...], preferred_element_type=F32)
    h1 = _layer_norm(DEEPNORM_ALPHA * h + o, l1g_ref[...], l1b_ref[...])
    h1_ref[...] = h1
    _pack_rows(h1p_ref, h1)

    hh = h1.astype(BF16)
    hl = (h1 - hh.astype(F32)).astype(BF16)
    logits = (lax.dot_general(rwh_ref[...], hh, _NT, preferred_element_type=F32)
              + lax.dot_general(rwh_ref[...], hl, _NT, preferred_element_type=F32)
              + lax.dot_general(rwl_ref[...], hh, _NT, preferred_element_type=F32))
    scores = jax.nn.sigmoid(logits)
    sel = scores + rb_ref[...]

    sub = lax.broadcasted_iota(I32, (GROUP_SIZE, tm), 0).astype(F32)
    neg = jnp.float32(-jnp.inf)
    selg = [sel[g * GROUP_SIZE:(g + 1) * GROUP_SIZE] for g in range(N_GROUPS)]
    scg = [scores[g * GROUP_SIZE:(g + 1) * GROUP_SIZE] for g in range(N_GROUPS)]
    eid = [sub + float(g * GROUP_SIZE) for g in range(N_GROUPS)]

    grp = []
    for g in range(N_GROUPS):
        m1 = jnp.max(selg[g], axis=0, keepdims=True)
        i1 = jnp.min(jnp.where(selg[g] == m1, sub, float(GROUP_SIZE)), axis=0, keepdims=True)
        m2 = jnp.max(jnp.where(sub == i1, neg, selg[g]), axis=0, keepdims=True)
        grp.append(m1 + m2)
    cur = []
    for g in range(N_GROUPS):
        beaten = jnp.zeros((1, tm), F32)
        for o_g in range(N_GROUPS):
            if o_g == g:
                continue
            wins = (grp[o_g] > grp[g]) if o_g > g else (grp[o_g] >= grp[g])
            beaten = beaten + jnp.where(wins, 1.0, 0.0)
        cur.append(jnp.where(beaten < float(TOPK_GROUPS), selg[g], neg))

    def all_reduce(vals, op):
        acc = vals[0]
        for v in vals[1:]:
            acc = op(acc, v)
        return acc

    idx_rows, w_rows = [], []
    member = [jnp.zeros((GROUP_SIZE, tm), F32) for _ in range(N_GROUPS)]
    for _ in range(TOP_K):
        m = jnp.max(all_reduce(cur, jnp.maximum), axis=0, keepdims=True)
        cand = [jnp.where(cur[g] == m, eid[g], float(N_EXPERTS)) for g in range(N_GROUPS)]
        idx = jnp.min(all_reduce(cand, jnp.minimum), axis=0, keepdims=True)
        hit = [eid[g] == idx for g in range(N_GROUPS)]
        w = jnp.sum(all_reduce([jnp.where(hit[g], scg[g], 0.0) for g in range(N_GROUPS)], jnp.add),
                    axis=0, keepdims=True)
        cur = [jnp.where(hit[g], neg, cur[g]) for g in range(N_GROUPS)]
        member = [jnp.where(hit[g], 1.0, member[g]) for g in range(N_GROUPS)]
        idx_rows.append(idx)
        w_rows.append(w)

    wsum = all_reduce(w_rows, jnp.add)
    for kk in range(TOP_K):
        eidx_ref[kk:kk + 1, :] = idx_rows[kk].astype(I32)
        wsel_ref[kk:kk + 1, :] = w_rows[kk] / wsum * ROUTED_SCALE

    tt = lax.broadcasted_iota(I32, (tm, tm), 0)
    uu = lax.broadcasted_iota(I32, (tm, tm), 1)
    before = (tt < uu).astype(BF16)
    memb = jnp.concatenate(member, axis=0)
    prefix = jnp.dot(memb.astype(BF16), before, preferred_element_type=F32) + cnt_scr[...]
    for kk in range(TOP_K):
        picked = [jnp.where(eid[g] == idx_rows[kk],
                            prefix[g * GROUP_SIZE:(g + 1) * GROUP_SIZE], 0.0) for g in range(N_GROUPS)]
        pos_ref[kk:kk + 1, :] = jnp.sum(all_reduce(picked, jnp.add), axis=0, keepdims=True).astype(I32)
    total = cnt_scr[...] + jnp.sum(memb, axis=1, keepdims=True)
    cnt_scr[...] = total
    cnt_ref[...] = jnp.broadcast_to(total, cnt_ref.shape).astype(I32)


def _post(x2, ln_g, ln_b, ma, gb, yb, w_bb, w_out, l1g, l1b, rw_hi, rw_lo, r_bias, *, tm):
    t = x2.shape[0]
    row = lambda w: pl.BlockSpec((tm, w), lambda i: (i, 0))
    tok = pl.BlockSpec((TOP_K, tm), lambda i: (0, i))
    return pl.pallas_call(
        _post_body,
        grid=(t // tm,),
        in_specs=[
            row(D_MODEL), _const_spec((1, D_MODEL)), _const_spec((1, D_MODEL)),
            row(D_MODEL), row(D_MODEL), row(SB_WIDTH),
            _const_spec(w_bb.shape), _const_spec(w_out.shape),
            _const_spec((1, D_MODEL)), _const_spec((1, D_MODEL)),
            _const_spec(rw_hi.shape), _const_spec(rw_lo.shape), _const_spec(r_bias.shape),
        ],
        out_specs=[row(D_MODEL), pl.BlockSpec((tm * ROW_SUB, LANES), lambda i: (i, 0)),
                   tok, tok, tok, pl.BlockSpec((N_EXPERTS, LANES), lambda i: (0, 0))],
        out_shape=[
            jax.ShapeDtypeStruct((t, D_MODEL), F32),
            jax.ShapeDtypeStruct((t * ROW_SUB, LANES), U32),
            jax.ShapeDtypeStruct((TOP_K, t), I32),
            jax.ShapeDtypeStruct((TOP_K, t), F32),
            jax.ShapeDtypeStruct((TOP_K, t), I32),
            jax.ShapeDtypeStruct((N_EXPERTS, LANES), I32),
        ],
        scratch_shapes=[pltpu.VMEM((N_EXPERTS, 1), F32)],
        compiler_params=pltpu.CompilerParams(
            dimension_semantics=("arbitrary",), vmem_limit_bytes=VMEM_LIMIT),
        name="post",
    )(x2, ln_g, ln_b, ma, gb, yb, w_bb, w_out, l1g, l1b, rw_hi, rw_lo, r_bias)


def _slots_body(ps_ref, eidx_ref, pos_ref, dest_ref):
    eidx = eidx_ref[...]
    dest = pos_ref[...]
    for e in range(N_EXPERTS):
        dest = dest + jnp.where(eidx == e, ps_ref[e], 0)
    dest_ref[...] = dest


def _slots(pad_starts, eidx, pos):
    spec = pl.BlockSpec(eidx.shape, lambda i, ps: (0, 0))
    return pl.pallas_call(
        _slots_body,
        grid_spec=pltpu.PrefetchScalarGridSpec(
            num_scalar_prefetch=1, grid=(1,), in_specs=[spec, spec], out_specs=spec),
        out_shape=jax.ShapeDtypeStruct(eidx.shape, I32),
        name="slots",
    )(pad_starts, eidx, pos)


def _tile_of(r):
    return pl.ds(pl.multiple_of(r * ROW_SUB, ROW_SUB), ROW_SUB)


def _dispatch_body(zrow_ref, dest_ref, h_ref, xs_ref, zero_scr, zsem, sem):
    tm = h_ref.shape[0] // ROW_SUB
    step = pl.program_id(0)

    def zero_copy(r):
        rows = pl.ds(pl.multiple_of(r * ROW_SUB, ROW_BLOCK * ROW_SUB), ROW_BLOCK * ROW_SUB)
        return pltpu.make_async_copy(zero_scr, xs_ref.at[rows], zsem)

    def row_copy(t, d):
        return pltpu.make_async_copy(h_ref.at[_tile_of(t)], xs_ref.at[_tile_of(d)], sem)

    @pl.when(step == 0)
    def _():
        zero_scr[...] = jnp.zeros_like(zero_scr)

        def start(e, c):
            r = zrow_ref[e]

            @pl.when(r >= 0)
            def _():
                zero_copy(jnp.maximum(r, 0)).start()
            return c

        def wait(e, c):
            @pl.when(zrow_ref[e] >= 0)
            def _():
                zero_copy(0).wait()
            return c

        lax.fori_loop(0, N_EXPERTS, start, 0)
        lax.fori_loop(0, N_EXPERTS, wait, 0)

    def issue(t, c):
        for kk in range(TOP_K):
            row_copy(t, dest_ref[t * TOP_K + kk]).start(priority=kk % 2)
        return c

    def drain(t, c):
        for kk in range(TOP_K):
            row_copy(0, 0).wait()
        return c

    lax.fori_loop(0, tm, issue, 0)
    lax.fori_loop(0, tm, drain, 0)


def _dispatch(zrow, dest_flat, h1t, n_rows, *, tm):
    t = h1t.shape[0] // ROW_SUB
    return pl.pallas_call(
        _dispatch_body,
        grid_spec=pltpu.PrefetchScalarGridSpec(
            num_scalar_prefetch=1,
            grid=(t // tm,),
            in_specs=[
                pl.BlockSpec((tm * TOP_K,), lambda i, z: (i,), memory_space=pltpu.SMEM),
                pl.BlockSpec((tm * ROW_SUB, LANES), lambda i, z: (i, 0)),
            ],
            out_specs=pl.BlockSpec(memory_space=pl.ANY),
            scratch_shapes=[pltpu.VMEM((ROW_BLOCK * ROW_SUB, LANES), U32),
                            pltpu.SemaphoreType.DMA(()), pltpu.SemaphoreType.DMA(())],
        ),
        out_shape=jax.ShapeDtypeStruct((n_rows * ROW_SUB, LANES), U32),
        compiler_params=pltpu.CompilerParams(
            dimension_semantics=("arbitrary",), vmem_limit_bytes=VMEM_LIMIT),
        name="dispatch",
    )(zrow, dest_flat, h1t)


def _experts_body(be_ref, nu_ref, xs_ref, wg_ref, wu_ref, wd_ref, ys_ref, wg_scr, wu_scr, wd_scr):
    blk = pl.program_id(0)
    prev = jnp.maximum(blk - 1, 0)

    @pl.when((blk < nu_ref[0]) & ((blk == 0) | (be_ref[blk] != be_ref[prev])))
    def _():
        wg_scr[...] = wg_ref[0].astype(BF16)
        wu_scr[...] = wu_ref[0].astype(BF16)
        wd_scr[...] = wd_ref[0].astype(BF16)

    @pl.when(blk < nu_ref[0])
    def _():
        xb = jnp.concatenate(_unpack_rows(xs_ref, ROW_BLOCK), axis=1).astype(BF16)
        gate = jnp.dot(xb, wg_scr[...], preferred_element_type=F32)
        up = jnp.dot(xb, wu_scr[...], preferred_element_type=F32)
        hid = (jax.nn.silu(gate) * up).astype(BF16)
        _pack_rows(ys_ref, jnp.dot(hid, wd_scr[...], preferred_element_type=F32))

    @pl.when(blk >= nu_ref[0])
    def _():
        ys_ref[...] = jnp.zeros_like(ys_ref)


def _experts(block_e, n_used, xs, w_gate, w_up, w_down):
    n_blocks = xs.shape[0] // (ROW_BLOCK * ROW_SUB)
    last = lambda b, be, nu: jnp.minimum(b, nu[0] - 1)
    blk = (ROW_BLOCK * ROW_SUB, LANES)
    return pl.pallas_call(
        _experts_body,
        grid_spec=pltpu.PrefetchScalarGridSpec(
            num_scalar_prefetch=2,
            grid=(n_blocks,),
            in_specs=[
                pl.BlockSpec(blk, lambda b, be, nu: (last(b, be, nu), 0)),
                pl.BlockSpec((1, D_MODEL, EXPERT_FF), lambda b, be, nu: (be[last(b, be, nu)], 0, 0)),
                pl.BlockSpec((1, D_MODEL, EXPERT_FF), lambda b, be, nu: (be[last(b, be, nu)], 0, 0)),
                pl.BlockSpec((1, EXPERT_FF, D_MODEL), lambda b, be, nu: (be[last(b, be, nu)], 0, 0)),
            ],
            out_specs=pl.BlockSpec(blk, lambda b, be, nu: (b, 0)),
            scratch_shapes=[pltpu.VMEM((D_MODEL, EXPERT_FF), BF16),
                            pltpu.VMEM((D_MODEL, EXPERT_FF), BF16),
                            pltpu.VMEM((EXPERT_FF, D_MODEL), BF16)],
        ),
        out_shape=jax.ShapeDtypeStruct(xs.shape, U32),
        compiler_params=pltpu.CompilerParams(
            dimension_semantics=("arbitrary",), vmem_limit_bytes=VMEM_LIMIT),
        name="experts",
    )(block_e, n_used, xs, w_gate, w_up, w_down)


def _combine_body(dest_ref, h1_ref, w_ref, p_ref, ys_ref, shg_ref, shu_ref, shd_ref,
                  pg_ref, pp_ref, g_ref, b_ref, o_ref, buf, sem):
    tm = h1_ref.shape[0]

    def row_copy(kk, t, d):
        return pltpu.make_async_copy(ys_ref.at[_tile_of(d)], buf.at[_tile_of(kk * tm + t)], sem)

    def issue(t, c):
        for kk in range(TOP_K):
            row_copy(kk, t, dest_ref[t * TOP_K + kk]).start(priority=kk % 2)
        return c

    def drain(t, c):
        for kk in range(TOP_K):
            row_copy(kk, 0, 0).wait()
        return c

    lax.fori_loop(0, tm, issue, 0)

    h1 = h1_ref[...]
    hb = h1.astype(BF16)
    hid = (jax.nn.silu(jnp.dot(hb, shg_ref[...], preferred_element_type=F32))
           * jnp.dot(hb, shu_ref[...], preferred_element_type=F32)).astype(BF16)
    r = DEEPNORM_ALPHA * h1 + jnp.dot(hid, shd_ref[...], preferred_element_type=F32)
    emb = jnp.dot(p_ref[...].astype(BF16), pp_ref[...], preferred_element_type=F32)

    lax.fori_loop(0, tm, drain, 0)
    w = w_ref[...]
    cols = [r[:, c * LANES:(c + 1) * LANES] for c in range(2 * ROW_SUB)]
    for kk in range(TOP_K):
        wk = jnp.broadcast_to(w[:, kk:kk + 1], (tm, LANES))
        for c, chunk in enumerate(_unpack_rows(buf, tm, base=kk * tm)):
            cols[c] = cols[c] + wk * chunk
    r = jnp.concatenate(cols, axis=1)
    gate = jax.nn.sigmoid(jnp.dot(r.astype(BF16), pg_ref[...], preferred_element_type=F32))
    o_ref[...] = _layer_norm(r + gate * emb, g_ref[...], b_ref[...])


def _combine(dest_flat, h1, wtok, p2, ys, sh_g, sh_u, sh_d, ple_g, ple_p, l2g, l2b, *, tm):
    t = h1.shape[0]
    row = lambda w: pl.BlockSpec((tm, w), lambda i: (i, 0))
    return pl.pallas_call(
        _combine_body,
        grid=(t // tm,),
        in_specs=[
            pl.BlockSpec((tm * TOP_K,), lambda i: (i,), memory_space=pltpu.SMEM),
            row(D_MODEL), row(TOP_K), row(PLE_DIM),
            pl.BlockSpec(memory_space=pl.ANY),
            _const_spec(sh_g.shape), _const_spec(sh_u.shape), _const_spec(sh_d.shape),
            _const_spec(ple_g.shape), _const_spec(ple_p.shape),
            _const_spec((1, D_MODEL)), _const_spec((1, D_MODEL)),
        ],
        out_specs=row(D_MODEL),
        out_shape=jax.ShapeDtypeStruct((t, D_MODEL), F32),
        scratch_shapes=[pltpu.VMEM((TOP_K * tm * ROW_SUB, LANES), U32),
                        pltpu.SemaphoreType.DMA(())],
        compiler_params=pltpu.CompilerParams(
            dimension_semantics=("arbitrary",), vmem_limit_bytes=VMEM_LIMIT),
        name="combine",
    )(dest_flat, h1, wtok, p2, ys, sh_g, sh_u, sh_d, ple_g, ple_p, l2g, l2b)


def _block_layout(counts, n_blocks):
    padded = (counts + ROW_BLOCK - 1) // ROW_BLOCK * ROW_BLOCK
    pad_ends = jnp.cumsum(padded)
    pad_starts = pad_ends - padded
    block_row0 = jnp.arange(n_blocks, dtype=I32) * ROW_BLOCK
    block_e = jnp.minimum(jnp.sum(pad_ends[None, :] <= block_row0[:, None], axis=1),
                          N_EXPERTS - 1).astype(I32)
    n_used = (pad_ends[-1:] // ROW_BLOCK).astype(I32)
    has_pad = (counts % ROW_BLOCK) != 0
    zrow = jnp.where(has_pad, pad_ends - ROW_BLOCK, -1).astype(I32)
    return pad_starts.astype(I32), block_e, n_used, zrow


def kernel(x, p, ln_in_g, ln_in_b, w_in, sg_ln_g, sg_ln_b, sg_w_s, sg_b_s, w_branch_a, w_branch_b,
           w_out, ln1_g, ln1_b, router_w, router_bias, moe_w_gate, moe_w_up, moe_w_down,
           sh_w_gate, sh_w_up, sh_w_down, ple_w_proj, ple_w_gate, ln2_g, ln2_b):
    bsz, seq, dm = x.shape
    t = bsz * seq
    vec = lambda a: a.reshape(1, -1).astype(F32)
    assert DEPTH == 1
    h = x.reshape(t, dm)
    for i in range(DEPTH):
        q, k, v, ma, gb = _inproj(
            h, vec(ln_in_g), vec(ln_in_b), w_in[i].astype(BF16), vec(sg_ln_g[i]), vec(sg_ln_b[i]),
            sg_w_s[i], sg_b_s[i].T, w_branch_a[i].astype(BF16), tm=INPROJ_ROWS)
        yb = _attn(q.reshape(bsz, seq, SB_WIDTH), k.reshape(bsz, seq, SB_WIDTH),
                   v.reshape(bsz, seq, SB_WIDTH), tq=ATTN_ROWS).reshape(t, SB_WIDTH)

        rw_t = router_w[i].T
        rw_hi = rw_t.astype(BF16)
        rw_lo = (rw_t - rw_hi.astype(F32)).astype(BF16)
        h1, h1p, eidx, wsel, pos, cnt = _post(
            h, vec(ln_in_g), vec(ln_in_b), ma, gb, yb, w_branch_b[i].astype(BF16),
            w_out[i].astype(BF16), vec(ln1_g[i]), vec(ln1_b[i]), rw_hi, rw_lo,
            router_bias[i].reshape(N_EXPERTS, 1).astype(F32), tm=POST_ROWS)

        n_blocks = t * TOP_K // ROW_BLOCK + N_EXPERTS
        pad_starts, block_e, n_used, zrow = _block_layout(cnt[:, 0], n_blocks)
        dest = _slots(pad_starts, eidx, pos).T.reshape(t * TOP_K)

        xs = _dispatch(zrow, dest, h1p, n_blocks * ROW_BLOCK, tm=DISPATCH_ROWS)
        ys = _experts(block_e, n_used, xs, moe_w_gate[i], moe_w_up[i], moe_w_down[i])
        h = _combine(dest, h1, wsel.T, p[i].reshape(t, PLE_DIM), ys,
                     sh_w_gate[i].astype(BF16), sh_w_up[i].astype(BF16), sh_w_down[i].astype(BF16),
                     ple_w_gate[i].astype(BF16), ple_w_proj[i].astype(BF16),
                     vec(ln2_g[i]), vec(ln2_b[i]), tm=COMBINE_ROWS)
    return h.reshape(bsz, seq, dm)
```

```python
import functools
import math

import jax
import jax.numpy as jnp
from jax import lax
from jax.experimental import pallas as pl
from jax.experimental.pallas import tpu as pltpu

F32 = jnp.float32
BF16 = jnp.bfloat16
I32 = jnp.int32

D_MODEL = 1024
DEPTH = 1
CHUNK = 64
PLE_DIM = 256
SG_BLOCK = 128
SG_GROUPS = 4
SG_WIDTH = 512
SB_HEADS = 8
SB_HEAD_DIM = 64
SB_WIDTH = SB_HEADS * SB_HEAD_DIM
N_EXPERTS = 64
N_GROUPS = 8
GROUP_SIZE = N_EXPERTS // N_GROUPS
TOPK_GROUPS = 4
TOP_K = 8
EXPERT_FF = 256
SHARED_FF = 256
ROUTED_SCALE = 2.5
LN_EPS = 1e-5
DEEPNORM_ALPHA = (2.0 * DEPTH) ** 0.25

_OFF_U, _OFF_V, _OFF_Q, _OFF_K, _OFF_VS, _OFF_GA, _OFF_GB, _OFF_END = (
    0, 512, 1024, 1536, 2048, 2560, 3584, 4608)

LANES = 128
ROW_SUB = 4
assert 2 * ROW_SUB * LANES == D_MODEL
U32 = jnp.uint32
ATTN_LANES = 256
ATTN_HEADS = ATTN_LANES // SB_HEAD_DIM
EXP_UNDERFLOW = -110.0
ROW_BLOCK = 512
INPROJ_ROWS, POST_ROWS, ATTN_ROWS, DISPATCH_ROWS, COMBINE_ROWS = 512, 512, 256, 256, 256
VMEM_LIMIT = 56 * 1024 * 1024

_NT = (((1,), (1,)), ((), ()))


def _layer_norm(x, g, b):
    mu = jnp.mean(x, axis=-1, keepdims=True)
    xc = x - mu
    var = jnp.mean(xc * xc, axis=-1, keepdims=True)
    return xc * lax.rsqrt(var + LN_EPS) * g + b


def _unpack_rows(ref, n, base=0):
    lo, hi = [], []
    for s in range(ROW_SUB):
        w = ref[pl.ds(base * ROW_SUB + s, n, stride=ROW_SUB), :]
        lo.append(lax.bitcast_convert_type(w << 16, F32))
        hi.append(lax.bitcast_convert_type(w & jnp.uint32(0xFFFF0000), F32))
    return lo + hi


def _pack_rows(ref, val, base=0):
    n = val.shape[0]
    bits = lambda c: lax.bitcast_convert_type(
        val[:, c * LANES:(c + 1) * LANES].astype(BF16).astype(F32), jnp.uint32)
    for s in range(ROW_SUB):
        ref[pl.ds(base * ROW_SUB + s, n, stride=ROW_SUB), :] = bits(s + ROW_SUB) | (bits(s) >> 16)


def _const_spec(shape):
    nd = len(shape)
    return pl.BlockSpec(shape, lambda *_: (0,) * nd, pipeline_mode=pl.Buffered(1))


def _inproj_body(x_ref, g_ref, b_ref, w_ref, sgg_ref, sgb_ref, ws_ref, bs_ref, wba_ref,
                 q_ref, k_ref, v_ref, ma_ref, gb_ref, ya_scr):
    tm = x_ref.shape[0]
    hb = _layer_norm(x_ref[...], g_ref[...], b_ref[...]).astype(BF16)

    def proj(lo, hi):
        return jnp.dot(hb, w_ref[:, lo:hi], preferred_element_type=F32)

    q_ref[...] = (proj(_OFF_Q, _OFF_K) * (1.0 / math.sqrt(SB_HEAD_DIM))).astype(BF16)
    k_ref[...] = proj(_OFF_K, _OFF_VS).astype(BF16)
    v_ref[...] = proj(_OFF_VS, _OFF_GA).astype(BF16)
    gb_ref[...] = jax.nn.sigmoid(proj(_OFF_GB, _OFF_END)).astype(BF16)

    yu = jax.nn.gelu(proj(_OFF_U, _OFF_V))
    vn = _layer_norm(jax.nn.gelu(proj(_OFF_V, _OFF_Q)), sgg_ref[...], sgb_ref[...]).astype(BF16)

    pi = lax.broadcasted_iota(I32, (SG_BLOCK, SG_BLOCK), 0) // CHUNK
    pj = lax.broadcasted_iota(I32, (SG_BLOCK, SG_BLOCK), 1) // CHUNK
    chunk_causal = pj <= pi
    gch = SG_WIDTH // SG_GROUPS
    for g in range(SG_GROUPS):
        wm = jnp.where(chunk_causal, ws_ref[g], 0.0).astype(BF16)
        bias = bs_ref[:, g:g + 1]
        for r in range(tm // SG_BLOCK):
            rows = slice(r * SG_BLOCK, (r + 1) * SG_BLOCK)
            cols = slice(g * gch, (g + 1) * gch)
            s = jnp.dot(wm, vn[rows, cols], preferred_element_type=F32) + bias
            ya_scr[rows, cols] = (yu[rows, cols] * s).astype(BF16)

    ba = jnp.dot(ya_scr[...], wba_ref[...], preferred_element_type=F32)
    ma_ref[...] = (jax.nn.sigmoid(proj(_OFF_GA, _OFF_GB)) * ba).astype(BF16)


def _inproj(x2, ln_g, ln_b, w_in, sg_g, sg_b, w_s, b_sT, w_ba, *, tm):
    t = x2.shape[0]
    row = lambda w: pl.BlockSpec((tm, w), lambda i: (i, 0))
    return pl.pallas_call(
        _inproj_body,
        grid=(t // tm,),
        in_specs=[
            row(D_MODEL),
            _const_spec((1, D_MODEL)), _const_spec((1, D_MODEL)),
            _const_spec(w_in.shape),
            _const_spec((1, SG_WIDTH)), _const_spec((1, SG_WIDTH)),
            _const_spec(w_s.shape), _const_spec(b_sT.shape),
            _const_spec(w_ba.shape),
        ],
        out_specs=[row(SB_WIDTH), row(SB_WIDTH), row(SB_WIDTH), row(D_MODEL), row(D_MODEL)],
        out_shape=[
            jax.ShapeDtypeStruct((t, SB_WIDTH), BF16),
            jax.ShapeDtypeStruct((t, SB_WIDTH), BF16),
            jax.ShapeDtypeStruct((t, SB_WIDTH), BF16),
            jax.ShapeDtypeStruct((t, D_MODEL), BF16),
            jax.ShapeDtypeStruct((t, D_MODEL), BF16),
        ],
        scratch_shapes=[pltpu.VMEM((tm, SG_WIDTH), BF16)],
        compiler_params=pltpu.CompilerParams(
            dimension_semantics=("arbitrary",), vmem_limit_bytes=VMEM_LIMIT),
        name="inproj",
    )(x2, ln_g, ln_b, w_in, sg_g, sg_b, w_s, b_sT, w_ba)


def _attn_body(q_ref, k_ref, v_ref, o_ref, acc_scr, car_scr, *, tq):
    qi = pl.program_id(2)
    q2 = q_ref[0]
    lane_head = lax.broadcasted_iota(I32, (tq, ATTN_LANES), 1) // SB_HEAD_DIM
    zero = jnp.zeros_like(q2)
    in_head = [lane_head == h for h in range(ATTN_HEADS)]
    qn = [jnp.where(in_head[h], -q2, zero) for h in range(ATTN_HEADS)]
    rr = lax.broadcasted_iota(I32, (tq, tq), 0)
    cc = lax.broadcasted_iota(I32, (tq, tq), 1)
    later = (rr > cc).astype(BF16)
    causal = cc < rr
    sign = jnp.uint32(0x80000000)

    def blocks(starts, diag):
        kbs = [k_ref[0, pl.ds(pl.multiple_of(s0, tq), tq), :] for s0 in starts]
        vbs = [v_ref[0, pl.ds(pl.multiple_of(s0, tq), tq), :] for s0 in starts]
        for h in range(ATTN_HEADS):
            car = car_scr[h]
            pv = None
            for n, (kb, vb) in enumerate(zip(kbs, vbs)):
                masked = diag and n == 0
                y = lax.dot_general(qn[h], kb, _NT, preferred_element_type=F32)
                neg_abs = lax.bitcast_convert_type(
                    lax.bitcast_convert_type(y, jnp.uint32) | sign, F32)
                lm = jnp.minimum(y, 0.0) - jnp.log(1.0 + jnp.exp(neg_abs))
                lb = lm - y
                if masked:
                    lm = jnp.where(causal, lm, 0.0)
                tail = jnp.dot(lm.astype(BF16), later, preferred_element_type=F32)
                a = jnp.exp(lb + tail + car)
                if masked:
                    a = jnp.where(causal, a, 0.0)
                d = jnp.dot(a.astype(BF16), vb, preferred_element_type=F32)
                pv = d if pv is None else pv + d
                car = car + jnp.sum(lm, axis=1, keepdims=True)
            acc_scr[h] += pv
            car_scr[h] = car

    acc_scr[...] = jnp.zeros_like(acc_scr)
    car_scr[...] = jnp.zeros_like(car_scr)
    blocks([qi * tq], True)

    def live():
        top = car_scr[0]
        for h in range(1, ATTN_HEADS):
            top = jnp.maximum(top, car_scr[h])
        return (jnp.max(top) > EXP_UNDERFLOW).astype(I32)

    def earlier(state):
        j, _ = state
        blocks([(qi - 1 - j) * tq], False)
        return j + 1, live()

    lax.while_loop(lambda s: (s[0] < qi) & (s[1] > 0), earlier, (jnp.int32(0), live()))
    out = acc_scr[0]
    for h in range(1, ATTN_HEADS):
        out = jnp.where(in_head[h], acc_scr[h], out)
    o_ref[0] = out.astype(BF16)


def _attn(q3, k3, v3, *, tq):
    b, s, _ = q3.shape
    qspec = pl.BlockSpec((1, tq, ATTN_LANES), lambda bi, hp, qi: (bi, qi, hp))
    kvspec = pl.BlockSpec((1, s, ATTN_LANES), lambda bi, hp, qi: (bi, 0, hp))
    return pl.pallas_call(
        functools.partial(_attn_body, tq=tq),
        grid=(b, SB_WIDTH // ATTN_LANES, s // tq),
        in_specs=[qspec, kvspec, kvspec],
        out_specs=qspec,
        out_shape=jax.ShapeDtypeStruct((b, s, SB_WIDTH), BF16),
        scratch_shapes=[pltpu.VMEM((ATTN_HEADS, tq, ATTN_LANES), F32),
                        pltpu.VMEM((ATTN_HEADS, tq, 1), F32)],
        compiler_params=pltpu.CompilerParams(
            dimension_semantics=("arbitrary", "arbitrary", "arbitrary"),
            vmem_limit_bytes=VMEM_LIMIT),
        name="attn",
    )(q3, k3, v3)


def _post_body(x_ref, g_ref, b_ref, ma_ref, gb_ref, yb_ref, wbb_ref, wout_ref, l1g_ref, l1b_ref,
               rwh_ref, rwl_ref, rb_ref,
               h1_ref, h1p_ref, eidx_ref, wsel_ref, pos_ref, cnt_ref, cnt_scr):
    tm = x_ref.shape[0]
    step = pl.program_id(0)

    @pl.when(step == 0)
    def _():
        cnt_scr[...] = jnp.zeros_like(cnt_scr)

    h = _layer_norm(x_ref[...], g_ref[...], b_ref[...])
    bb = jnp.dot(yb_ref[...], wbb_ref[...], preferred_element_type=F32)
    merged = ma_ref[...].astype(F32) + gb_ref[...].astype(F32) * bb
    o = jnp.dot(merged.astype(BF16), wout_ref[...], preferred_element_type=F32)
    h1 = _layer_norm(DEEPNORM_ALPHA * h + o, l1g_ref[...], l1b_ref[...])
    h1_ref[...] = h1
    _pack_rows(h1p_ref, h1)

    hh = h1.astype(BF16)
    hl = (h1 - hh.astype(F32)).astype(BF16)
    logits = (lax.dot_general(rwh_ref[...], hh, _NT, preferred_element_type=F32)
              + lax.dot_general(rwh_ref[...], hl, _NT, preferred_element_type=F32)
              + lax.dot_general(rwl_ref[...], hh, _NT, preferred_element_type=F32))
    scores = jax.nn.sigmoid(logits)
    sel = scores + rb_ref[...]

    sub = lax.broadcasted_iota(I32, (GROUP_SIZE, tm), 0).astype(F32)
    neg = jnp.float32(-jnp.inf)
    selg = [sel[g * GROUP_SIZE:(g + 1) * GROUP_SIZE] for g in range(N_GROUPS)]
    scg = [scores[g * GROUP_SIZE:(g + 1) * GROUP_SIZE] for g in range(N_GROUPS)]
    eid = [sub + float(g * GROUP_SIZE) for g in range(N_GROUPS)]

    grp = []
    for g in range(N_GROUPS):
        m1 = jnp.max(selg[g], axis=0, keepdims=True)
        i1 = jnp.min(jnp.where(selg[g] == m1, sub, float(GROUP_SIZE)), axis=0, keepdims=True)
        m2 = jnp.max(jnp.where(sub == i1, neg, selg[g]), axis=0, keepdims=True)
        grp.append(m1 + m2)
    cur = []
    for g in range(N_GROUPS):
        beaten = jnp.zeros((1, tm), F32)
        for o_g in range(N_GROUPS):
            if o_g == g:
                continue
            wins = (grp[o_g] > grp[g]) if o_g > g else (grp[o_g] >= grp[g])
            beaten = beaten + jnp.where(wins, 1.0, 0.0)
        cur.append(jnp.where(beaten < float(TOPK_GROUPS), selg[g], neg))

    def all_reduce(vals, op):
        acc = vals[0]
        for v in vals[1:]:
            acc = op(acc, v)
        return acc

    idx_rows, w_rows = [], []
    member = [jnp.zeros((GROUP_SIZE, tm), F32) for _ in range(N_GROUPS)]
    for _ in range(TOP_K):
        m = jnp.max(all_reduce(cur, jnp.maximum), axis=0, keepdims=True)
        cand = [jnp.where(cur[g] == m, eid[g], float(N_EXPERTS)) for g in range(N_GROUPS)]
        idx = jnp.min(all_reduce(cand, jnp.minimum), axis=0, keepdims=True)
        hit = [eid[g] == idx for g in range(N_GROUPS)]
        w = jnp.sum(all_reduce([jnp.where(hit[g], scg[g], 0.0) for g in range(N_GROUPS)], jnp.add),
                    axis=0, keepdims=True)
        cur = [jnp.where(hit[g], neg, cur[g]) for g in range(N_GROUPS)]
        member = [jnp.where(hit[g], 1.0, member[g]) for g in range(N_GROUPS)]
        idx_rows.append(idx)
        w_rows.append(w)

    wsum = all_reduce(w_rows, jnp.add)
    for kk in range(TOP_K):
        eidx_ref[kk:kk + 1, :] = idx_rows[kk].astype(I32)
        wsel_ref[kk:kk + 1, :] = w_rows[kk] / wsum * ROUTED_SCALE

    tt = lax.broadcasted_iota(I32, (tm, tm), 0)
    uu = lax.broadcasted_iota(I32, (tm, tm), 1)
    before = (tt < uu).astype(BF16)
    memb = jnp.concatenate(member, axis=0)
    prefix = jnp.dot(memb.astype(BF16), before, preferred_element_type=F32) + cnt_scr[...]
    for kk in range(TOP_K):
        picked = [jnp.where(eid[g] == idx_rows[kk],
                            prefix[g * GROUP_SIZE:(g + 1) * GROUP_SIZE], 0.0) for g in range(N_GROUPS)]
        pos_ref[kk:kk + 1, :] = jnp.sum(all_reduce(picked, jnp.add), axis=0, keepdims=True).astype(I32)
    total = cnt_scr[...] + jnp.sum(memb, axis=1, keepdims=True)
    cnt_scr[...] = total
    cnt_ref[...] = jnp.broadcast_to(total, cnt_ref.shape).astype(I32)


def _post(x2, ln_g, ln_b, ma, gb, yb, w_bb, w_out, l1g, l1b, rw_hi, rw_lo, r_bias, *, tm):
    t = x2.shape[0]
    row = lambda w: pl.BlockSpec((tm, w), lambda i: (i, 0))
    tok = pl.BlockSpec((TOP_K, tm), lambda i: (0, i))
    return pl.pallas_call(
        _post_body,
        grid=(t // tm,),
        in_specs=[
            row(D_MODEL), _const_spec((1, D_MODEL)), _const_spec((1, D_MODEL)),
            row(D_MODEL), row(D_MODEL), row(SB_WIDTH),
            _const_spec(w_bb.shape), _const_spec(w_out.shape),
            _const_spec((1, D_MODEL)), _const_spec((1, D_MODEL)),
            _const_spec(rw_hi.shape), _const_spec(rw_lo.shape), _const_spec(r_bias.shape),
        ],
        out_specs=[row(D_MODEL), pl.BlockSpec((tm * ROW_SUB, LANES), lambda i: (i, 0)),
                   tok, tok, tok, pl.BlockSpec((N_EXPERTS, LANES), lambda i: (0, 0))],
        out_shape=[
            jax.ShapeDtypeStruct((t, D_MODEL), F32),
            jax.ShapeDtypeStruct((t * ROW_SUB, LANES), U32),
            jax.ShapeDtypeStruct((TOP_K, t), I32),
            jax.ShapeDtypeStruct((TOP_K, t), F32),
            jax.ShapeDtypeStruct((TOP_K, t), I32),
            jax.ShapeDtypeStruct((N_EXPERTS, LANES), I32),
        ],
        scratch_shapes=[pltpu.VMEM((N_EXPERTS, 1), F32)],
        compiler_params=pltpu.CompilerParams(
            dimension_semantics=("arbitrary",), vmem_limit_bytes=VMEM_LIMIT),
        name="post",
    )(x2, ln_g, ln_b, ma, gb, yb, w_bb, w_out, l1g, l1b, rw_hi, rw_lo, r_bias)


def _slots_body(ps_ref, eidx_ref, pos_ref, dest_ref):
    eidx = eidx_ref[...]
    dest = pos_ref[...]
    for e in range(N_EXPERTS):
        dest = dest + jnp.where(eidx == e, ps_ref[e], 0)
    dest_ref[...] = dest


def _slots(pad_starts, eidx, pos):
    spec = pl.BlockSpec(eidx.shape, lambda i, ps: (0, 0))
    return pl.pallas_call(
        _slots_body,
        grid_spec=pltpu.PrefetchScalarGridSpec(
            num_scalar_prefetch=1, grid=(1,), in_specs=[spec, spec], out_specs=spec),
        out_shape=jax.ShapeDtypeStruct(eidx.shape, I32),
        name="slots",
    )(pad_starts, eidx, pos)


def _tile_of(r):
    return pl.ds(pl.multiple_of(r * ROW_SUB, ROW_SUB), ROW_SUB)


def _dispatch_body(zrow_ref, dest_ref, h_ref, xs_ref, zero_scr, zsem, sem):
    tm = h_ref.shape[0] // ROW_SUB
    step = pl.program_id(0)

    def zero_copy(r):
        rows = pl.ds(pl.multiple_of(r * ROW_SUB, ROW_BLOCK * ROW_SUB), ROW_BLOCK * ROW_SUB)
        return pltpu.make_async_copy(zero_scr, xs_ref.at[rows], zsem)

    def row_copy(t, d):
        return pltpu.make_async_copy(h_ref.at[_tile_of(t)], xs_ref.at[_tile_of(d)], sem)

    @pl.when(step == 0)
    def _():
        zero_scr[...] = jnp.zeros_like(zero_scr)

        def start(e, c):
            r = zrow_ref[e]

            @pl.when(r >= 0)
            def _():
                zero_copy(jnp.maximum(r, 0)).start()
            return c

        def wait(e, c):
            @pl.when(zrow_ref[e] >= 0)
            def _():
                zero_copy(0).wait()
            return c

        lax.fori_loop(0, N_EXPERTS, start, 0)
        lax.fori_loop(0, N_EXPERTS, wait, 0)

    def issue(t, c):
        for kk in range(TOP_K):
            row_copy(t, dest_ref[t * TOP_K + kk]).start(priority=kk % 2)
        return c

    def drain(t, c):
        for kk in range(TOP_K):
            row_copy(0, 0).wait()
        return c

    lax.fori_loop(0, tm, issue, 0)
    lax.fori_loop(0, tm, drain, 0)


def _dispatch(zrow, dest_flat, h1t, n_rows, *, tm):
    t = h1t.shape[0] // ROW_SUB
    return pl.pallas_call(
        _dispatch_body,
        grid_spec=pltpu.PrefetchScalarGridSpec(
            num_scalar_prefetch=1,
            grid=(t // tm,),
            in_specs=[
                pl.BlockSpec((tm * TOP_K,), lambda i, z: (i,), memory_space=pltpu.SMEM),
                pl.BlockSpec((tm * ROW_SUB, LANES), lambda i, z: (i, 0)),
            ],
            out_specs=pl.BlockSpec(memory_space=pl.ANY),
            scratch_shapes=[pltpu.VMEM((ROW_BLOCK * ROW_SUB, LANES), U32),
                            pltpu.SemaphoreType.DMA(()), pltpu.SemaphoreType.DMA(())],
        ),
        out_shape=jax.ShapeDtypeStruct((n_rows * ROW_SUB, LANES), U32),
        compiler_params=pltpu.CompilerParams(
            dimension_semantics=("arbitrary",), vmem_limit_bytes=VMEM_LIMIT),
        name="dispatch",
    )(zrow, dest_flat, h1t)


def _experts_body(r0_ref, nb_ref, xs_ref, wg_ref, wu_ref, wd_ref, ys_ref,
                  wg_scr, wu_scr, wd_scr, xbuf, ybuf, isem, osem):
    ex = pl.program_id(0)
    nblk = nb_ref[ex]
    words = ROW_BLOCK * ROW_SUB

    def hbm_rows(j):
        return pl.ds(pl.multiple_of((r0_ref[ex] + j * ROW_BLOCK) * ROW_SUB, words), words)

    def buf_rows(slot):
        return pl.ds(pl.multiple_of(slot * words, words), words)

    def in_copy(j, slot):
        return pltpu.make_async_copy(xs_ref.at[hbm_rows(j)], xbuf.at[buf_rows(slot)], isem.at[slot])

    def out_copy(j, slot):
        return pltpu.make_async_copy(ybuf.at[buf_rows(slot)], ys_ref.at[hbm_rows(j)], osem.at[slot])

    @pl.when(nblk > 0)
    def _():
        in_copy(0, 0).start()
        wg_scr[...] = wg_ref[0].astype(BF16)
        wu_scr[...] = wu_ref[0].astype(BF16)
        wd_scr[...] = wd_ref[0].astype(BF16)

    def block(j, c):
        slot = lax.rem(j, 2)

        @pl.when(j + 1 < nblk)
        def _():
            in_copy(j + 1, 1 - slot).start()

        in_copy(j, slot).wait()

        @pl.when(j >= 2)
        def _():
            out_copy(j - 2, slot).wait()

        xb = jnp.concatenate(_unpack_rows(xbuf, ROW_BLOCK, base=slot * ROW_BLOCK), axis=1).astype(BF16)
        gate = jnp.dot(xb, wg_scr[...], preferred_element_type=F32)
        up = jnp.dot(xb, wu_scr[...], preferred_element_type=F32)
        hid = (jax.nn.silu(gate) * up).astype(BF16)
        _pack_rows(ybuf, jnp.dot(hid, wd_scr[...], preferred_element_type=F32), base=slot * ROW_BLOCK)
        out_copy(j, slot).start()
        return c

    lax.fori_loop(0, nblk, block, 0)

    @pl.when(nblk >= 2)
    def _():
        out_copy(nblk - 2, lax.rem(nblk, 2)).wait()

    @pl.when(nblk >= 1)
    def _():
        out_copy(nblk - 1, lax.rem(nblk + 1, 2)).wait()


def _experts(row0, nblk, xs, w_gate, w_up, w_down):
    words = ROW_BLOCK * ROW_SUB
    wspec = lambda shape: pl.BlockSpec((1,) + shape, lambda e, r0, nb: (e, 0, 0))
    return pl.pallas_call(
        _experts_body,
        grid_spec=pltpu.PrefetchScalarGridSpec(
            num_scalar_prefetch=2,
            grid=(N_EXPERTS,),
            in_specs=[
                pl.BlockSpec(memory_space=pl.ANY),
                wspec((D_MODEL, EXPERT_FF)), wspec((D_MODEL, EXPERT_FF)), wspec((EXPERT_FF, D_MODEL)),
            ],
            out_specs=pl.BlockSpec(memory_space=pl.ANY),
            scratch_shapes=[pltpu.VMEM((D_MODEL, EXPERT_FF), BF16),
                            pltpu.VMEM((D_MODEL, EXPERT_FF), BF16),
                            pltpu.VMEM((EXPERT_FF, D_MODEL), BF16),
                            pltpu.VMEM((2 * words, LANES), U32),
                            pltpu.VMEM((2 * words, LANES), U32),
                            pltpu.SemaphoreType.DMA((2,)), pltpu.SemaphoreType.DMA((2,))],
        ),
        out_shape=jax.ShapeDtypeStruct(xs.shape, U32),
        compiler_params=pltpu.CompilerParams(
            dimension_semantics=("arbitrary",), vmem_limit_bytes=VMEM_LIMIT),
        name="experts",
    )(row0, nblk, xs, w_gate, w_up, w_down)


def _combine_body(dest_ref, h1_ref, w_ref, p_ref, ys_ref, shg_ref, shu_ref, shd_ref,
                  pg_ref, pp_ref, g_ref, b_ref, o_ref, buf, sem):
    tm = h1_ref.shape[0]

    def row_copy(kk, t, d):
        return pltpu.make_async_copy(ys_ref.at[_tile_of(d)], buf.at[_tile_of(kk * tm + t)], sem)

    def issue(t, c):
        for kk in range(TOP_K):
            row_copy(kk, t, dest_ref[t * TOP_K + kk]).start(priority=kk % 2)
        return c

    def drain(t, c):
        for kk in range(TOP_K):
            row_copy(kk, 0, 0).wait()
        return c

    lax.fori_loop(0, tm, issue, 0)

    h1 = h1_ref[...]
    hb = h1.astype(BF16)
    hid = (jax.nn.silu(jnp.dot(hb, shg_ref[...], preferred_element_type=F32))
           * jnp.dot(hb, shu_ref[...], preferred_element_type=F32)).astype(BF16)
    r = DEEPNORM_ALPHA * h1 + jnp.dot(hid, shd_ref[...], preferred_element_type=F32)
    emb = jnp.dot(p_ref[...].astype(BF16), pp_ref[...], preferred_element_type=F32)

    lax.fori_loop(0, tm, drain, 0)
    w = w_ref[...]
    cols = [r[:, c * LANES:(c + 1) * LANES] for c in range(2 * ROW_SUB)]
    for kk in range(TOP_K):
        wk = jnp.broadcast_to(w[:, kk:kk + 1], (tm, LANES))
        for c, chunk in enumerate(_unpack_rows(buf, tm, base=kk * tm)):
            cols[c] = cols[c] + wk * chunk
    r = jnp.concatenate(cols, axis=1)
    gate = jax.nn.sigmoid(jnp.dot(r.astype(BF16), pg_ref[...], preferred_element_type=F32))
    o_ref[...] = _layer_norm(r + gate * emb, g_ref[...], b_ref[...])


def _combine(dest_flat, h1, wtok, p2, ys, sh_g, sh_u, sh_d, ple_g, ple_p, l2g, l2b, *, tm):
    t = h1.shape[0]
    row = lambda w: pl.BlockSpec((tm, w), lambda i: (i, 0))
    return pl.pallas_call(
        _combine_body,
        grid=(t // tm,),
        in_specs=[
            pl.BlockSpec((tm * TOP_K,), lambda i: (i,), memory_space=pltpu.SMEM),
            row(D_MODEL), row(TOP_K), row(PLE_DIM),
            pl.BlockSpec(memory_space=pl.ANY),
            _const_spec(sh_g.shape), _const_spec(sh_u.shape), _const_spec(sh_d.shape),
            _const_spec(ple_g.shape), _const_spec(ple_p.shape),
            _const_spec((1, D_MODEL)), _const_spec((1, D_MODEL)),
        ],
        out_specs=row(D_MODEL),
        out_shape=jax.ShapeDtypeStruct((t, D_MODEL), F32),
        scratch_shapes=[pltpu.VMEM((TOP_K * tm * ROW_SUB, LANES), U32),
                        pltpu.SemaphoreType.DMA(())],
        compiler_params=pltpu.CompilerParams(
            dimension_semantics=("arbitrary",), vmem_limit_bytes=VMEM_LIMIT),
        name="combine",
    )(dest_flat, h1, wtok, p2, ys, sh_g, sh_u, sh_d, ple_g, ple_p, l2g, l2b)


def _block_layout(counts):
    nblk = (counts + ROW_BLOCK - 1) // ROW_BLOCK
    pad_ends = jnp.cumsum(nblk * ROW_BLOCK)
    pad_starts = pad_ends - nblk * ROW_BLOCK
    has_pad = (counts % ROW_BLOCK) != 0
    zrow = jnp.where(has_pad, pad_ends - ROW_BLOCK, -1).astype(I32)
    return pad_starts.astype(I32), nblk.astype(I32), zrow


def kernel(x, p, ln_in_g, ln_in_b, w_in, sg_ln_g, sg_ln_b, sg_w_s, sg_b_s, w_branch_a, w_branch_b,
           w_out, ln1_g, ln1_b, router_w, router_bias, moe_w_gate, moe_w_up, moe_w_down,
           sh_w_gate, sh_w_up, sh_w_down, ple_w_proj, ple_w_gate, ln2_g, ln2_b):
    bsz, seq, dm = x.shape
    t = bsz * seq
    vec = lambda a: a.reshape(1, -1).astype(F32)
    assert DEPTH == 1
    h = x.reshape(t, dm)
    for i in range(DEPTH):
        q, k, v, ma, gb = _inproj(
            h, vec(ln_in_g), vec(ln_in_b), w_in[i].astype(BF16), vec(sg_ln_g[i]), vec(sg_ln_b[i]),
            sg_w_s[i], sg_b_s[i].T, w_branch_a[i].astype(BF16), tm=INPROJ_ROWS)
        yb = _attn(q.reshape(bsz, seq, SB_WIDTH), k.reshape(bsz, seq, SB_WIDTH),
                   v.reshape(bsz, seq, SB_WIDTH), tq=ATTN_ROWS).reshape(t, SB_WIDTH)

        rw_t = router_w[i].T
        rw_hi = rw_t.astype(BF16)
        rw_lo = (rw_t - rw_hi.astype(F32)).astype(BF16)
        h1, h1p, eidx, wsel, pos, cnt = _post(
            h, vec(ln_in_g), vec(ln_in_b), ma, gb, yb, w_branch_b[i].astype(BF16),
            w_out[i].astype(BF16), vec(ln1_g[i]), vec(ln1_b[i]), rw_hi, rw_lo,
            router_bias[i].reshape(N_EXPERTS, 1).astype(F32), tm=POST_ROWS)

        n_blocks = t * TOP_K // ROW_BLOCK + N_EXPERTS
        pad_starts, nblk, zrow = _block_layout(cnt[:, 0])
        dest = _slots(pad_starts, eidx, pos).T.reshape(t * TOP_K)

        xs = _dispatch(zrow, dest, h1p, n_blocks * ROW_BLOCK, tm=DISPATCH_ROWS)
        ys = _experts(pad_starts, nblk, xs, moe_w_gate[i], moe_w_up[i], moe_w_down[i])
        h = _combine(dest, h1, wsel.T, p[i].reshape(t, PLE_DIM), ys,
                     sh_w_gate[i].astype(BF16), sh_w_up[i].astype(BF16), sh_w_down[i].astype(BF16),
                     ple_w_gate[i].astype(BF16), ple_w_proj[i].astype(BF16),
                     vec(ln2_g[i]), vec(ln2_b[i]), tm=COMBINE_ROWS)
    return h.reshape(bsz, seq, dm)
```

```python
import functools
import math

import jax
import jax.numpy as jnp
from jax import lax
from jax.experimental import pallas as pl
from jax.experimental.pallas import tpu as pltpu

F32 = jnp.float32
BF16 = jnp.bfloat16
I32 = jnp.int32

D_MODEL = 1024
DEPTH = 1
CHUNK = 64
PLE_DIM = 256
SG_BLOCK = 128
SG_GROUPS = 4
SG_WIDTH = 512
SB_HEADS = 8
SB_HEAD_DIM = 64
SB_WIDTH = SB_HEADS * SB_HEAD_DIM
N_EXPERTS = 64
N_GROUPS = 8
GROUP_SIZE = N_EXPERTS // N_GROUPS
TOPK_GROUPS = 4
TOP_K = 8
EXPERT_FF = 256
SHARED_FF = 256
ROUTED_SCALE = 2.5
LN_EPS = 1e-5
DEEPNORM_ALPHA = (2.0 * DEPTH) ** 0.25

_OFF_U, _OFF_V, _OFF_Q, _OFF_K, _OFF_VS, _OFF_GA, _OFF_GB, _OFF_END = (
    0, 512, 1024, 1536, 2048, 2560, 3584, 4608)

LANES = 128
ROW_SUB = 4
assert 2 * ROW_SUB * LANES == D_MODEL
U32 = jnp.uint32
ATTN_LANES = 256
ATTN_HEADS = ATTN_LANES // SB_HEAD_DIM
EXP_UNDERFLOW = -110.0
ROW_BLOCK = 512
INPROJ_ROWS, POST_ROWS, ATTN_ROWS, DISPATCH_ROWS, COMBINE_ROWS = 512, 512, 256, 512, 512
VMEM_LIMIT = 56 * 1024 * 1024

_NT = (((1,), (1,)), ((), ()))


def _layer_norm(x, g, b):
    mu = jnp.mean(x, axis=-1, keepdims=True)
    xc = x - mu
    var = jnp.mean(xc * xc, axis=-1, keepdims=True)
    return xc * lax.rsqrt(var + LN_EPS) * g + b


def _unpack_rows(ref, n, base=0):
    lo, hi = [], []
    for s in range(ROW_SUB):
        w = ref[pl.ds(base * ROW_SUB + s, n, stride=ROW_SUB), :]
        lo.append(lax.bitcast_convert_type(w << 16, F32))
        hi.append(lax.bitcast_convert_type(w & jnp.uint32(0xFFFF0000), F32))
    return lo + hi


def _pack_rows(ref, val):
    n = val.shape[0]
    bits = lambda c: lax.bitcast_convert_type(
        val[:, c * LANES:(c + 1) * LANES].astype(BF16).astype(F32), jnp.uint32)
    for s in range(ROW_SUB):
        ref[pl.ds(s, n, stride=ROW_SUB), :] = bits(s + ROW_SUB) | (bits(s) >> 16)


def _const_spec(shape):
    nd = len(shape)
    return pl.BlockSpec(shape, lambda *_: (0,) * nd, pipeline_mode=pl.Buffered(1))


def _inproj_body(x_ref, g_ref, b_ref, w_ref, sgg_ref, sgb_ref, ws_ref, bs_ref, wba_ref,
                 q_ref, k_ref, v_ref, ma_ref, gb_ref, ya_scr):
    tm = x_ref.shape[0]
    hb = _layer_norm(x_ref[...], g_ref[...], b_ref[...]).astype(BF16)

    def proj(lo, hi):
        return jnp.dot(hb, w_ref[:, lo:hi], preferred_element_type=F32)

    q_ref[...] = (proj(_OFF_Q, _OFF_K) * (1.0 / math.sqrt(SB_HEAD_DIM))).astype(BF16)
    k_ref[...] = proj(_OFF_K, _OFF_VS).astype(BF16)
    v_ref[...] = proj(_OFF_VS, _OFF_GA).astype(BF16)
    gb_ref[...] = jax.nn.sigmoid(proj(_OFF_GB, _OFF_END)).astype(BF16)

    yu = jax.nn.gelu(proj(_OFF_U, _OFF_V))
    vn = _layer_norm(jax.nn.gelu(proj(_OFF_V, _OFF_Q)), sgg_ref[...], sgb_ref[...]).astype(BF16)

    pi = lax.broadcasted_iota(I32, (SG_BLOCK, SG_BLOCK), 0) // CHUNK
    pj = lax.broadcasted_iota(I32, (SG_BLOCK, SG_BLOCK), 1) // CHUNK
    chunk_causal = pj <= pi
    gch = SG_WIDTH // SG_GROUPS
    for g in range(SG_GROUPS):
        wm = jnp.where(chunk_causal, ws_ref[g], 0.0).astype(BF16)
        bias = bs_ref[:, g:g + 1]
        for r in range(tm // SG_BLOCK):
            rows = slice(r * SG_BLOCK, (r + 1) * SG_BLOCK)
            cols = slice(g * gch, (g + 1) * gch)
            s = jnp.dot(wm, vn[rows, cols], preferred_element_type=F32) + bias
            ya_scr[rows, cols] = (yu[rows, cols] * s).astype(BF16)

    ba = jnp.dot(ya_scr[...], wba_ref[...], preferred_element_type=F32)
    ma_ref[...] = (jax.nn.sigmoid(proj(_OFF_GA, _OFF_GB)) * ba).astype(BF16)


def _inproj(x2, ln_g, ln_b, w_in, sg_g, sg_b, w_s, b_sT, w_ba, *, tm):
    t = x2.shape[0]
    row = lambda w: pl.BlockSpec((tm, w), lambda i: (i, 0))
    return pl.pallas_call(
        _inproj_body,
        grid=(t // tm,),
        in_specs=[
            row(D_MODEL),
            _const_spec((1, D_MODEL)), _const_spec((1, D_MODEL)),
            _const_spec(w_in.shape),
            _const_spec((1, SG_WIDTH)), _const_spec((1, SG_WIDTH)),
            _const_spec(w_s.shape), _const_spec(b_sT.shape),
            _const_spec(w_ba.shape),
        ],
        out_specs=[row(SB_WIDTH), row(SB_WIDTH), row(SB_WIDTH), row(D_MODEL), row(D_MODEL)],
        out_shape=[
            jax.ShapeDtypeStruct((t, SB_WIDTH), BF16),
            jax.ShapeDtypeStruct((t, SB_WIDTH), BF16),
            jax.ShapeDtypeStruct((t, SB_WIDTH), BF16),
            jax.ShapeDtypeStruct((t, D_MODEL), BF16),
            jax.ShapeDtypeStruct((t, D_MODEL), BF16),
        ],
        scratch_shapes=[pltpu.VMEM((tm, SG_WIDTH), BF16)],
        compiler_params=pltpu.CompilerParams(
            dimension_semantics=("arbitrary",), vmem_limit_bytes=VMEM_LIMIT),
        name="inproj",
    )(x2, ln_g, ln_b, w_in, sg_g, sg_b, w_s, b_sT, w_ba)


def _attn_body(q_ref, k_ref, v_ref, o_ref, acc_scr, car_scr, *, tq):
    qi = pl.program_id(2)
    q2 = q_ref[0]
    lane_head = lax.broadcasted_iota(I32, (tq, ATTN_LANES), 1) // SB_HEAD_DIM
    zero = jnp.zeros_like(q2)
    in_head = [lane_head == h for h in range(ATTN_HEADS)]
    qn = [jnp.where(in_head[h], -q2, zero) for h in range(ATTN_HEADS)]
    rr = lax.broadcasted_iota(I32, (tq, tq), 0)
    cc = lax.broadcasted_iota(I32, (tq, tq), 1)
    later = (rr > cc).astype(BF16)
    causal = cc < rr
    sign = jnp.uint32(0x80000000)

    def blocks(starts, diag):
        kbs = [k_ref[0, pl.ds(pl.multiple_of(s0, tq), tq), :] for s0 in starts]
        vbs = [v_ref[0, pl.ds(pl.multiple_of(s0, tq), tq), :] for s0 in starts]
        for h in range(ATTN_HEADS):
            car = car_scr[h]
            pv = None
            for n, (kb, vb) in enumerate(zip(kbs, vbs)):
                masked = diag and n == 0
                y = lax.dot_general(qn[h], kb, _NT, preferred_element_type=F32)
                neg_abs = lax.bitcast_convert_type(
                    lax.bitcast_convert_type(y, jnp.uint32) | sign, F32)
                lm = jnp.minimum(y, 0.0) - jnp.log(1.0 + jnp.exp(neg_abs))
                lb = lm - y
                if masked:
                    lm = jnp.where(causal, lm, 0.0)
                tail = jnp.dot(lm.astype(BF16), later, preferred_element_type=F32)
                a = jnp.exp(lb + tail + car)
                if masked:
                    a = jnp.where(causal, a, 0.0)
                d = jnp.dot(a.astype(BF16), vb, preferred_element_type=F32)
                pv = d if pv is None else pv + d
                car = car + jnp.sum(lm, axis=1, keepdims=True)
            acc_scr[h] += pv
            car_scr[h] = car

    acc_scr[...] = jnp.zeros_like(acc_scr)
    car_scr[...] = jnp.zeros_like(car_scr)
    blocks([qi * tq], True)

    def live():
        top = car_scr[0]
        for h in range(1, ATTN_HEADS):
            top = jnp.maximum(top, car_scr[h])
        return (jnp.max(top) > EXP_UNDERFLOW).astype(I32)

    def earlier(state):
        j, _ = state
        blocks([(qi - 1 - j) * tq], False)
        return j + 1, live()

    lax.while_loop(lambda s: (s[0] < qi) & (s[1] > 0), earlier, (jnp.int32(0), live()))
    out = acc_scr[0]
    for h in range(1, ATTN_HEADS):
        out = jnp.where(in_head[h], acc_scr[h], out)
    o_ref[0] = out.astype(BF16)


def _attn(q3, k3, v3, *, tq):
    b, s, _ = q3.shape
    qspec = pl.BlockSpec((1, tq, ATTN_LANES), lambda bi, hp, qi: (bi, qi, hp))
    kvspec = pl.BlockSpec((1, s, ATTN_LANES), lambda bi, hp, qi: (bi, 0, hp))
    return pl.pallas_call(
        functools.partial(_attn_body, tq=tq),
        grid=(b, SB_WIDTH // ATTN_LANES, s // tq),
        in_specs=[qspec, kvspec, kvspec],
        out_specs=qspec,
        out_shape=jax.ShapeDtypeStruct((b, s, SB_WIDTH), BF16),
        scratch_shapes=[pltpu.VMEM((ATTN_HEADS, tq, ATTN_LANES), F32),
                        pltpu.VMEM((ATTN_HEADS, tq, 1), F32)],
        compiler_params=pltpu.CompilerParams(
            dimension_semantics=("arbitrary", "arbitrary", "arbitrary"),
            vmem_limit_bytes=VMEM_LIMIT),
        name="attn",
    )(q3, k3, v3)


def _post_body(x_ref, g_ref, b_ref, ma_ref, gb_ref, yb_ref, wbb_ref, wout_ref, l1g_ref, l1b_ref,
               rwh_ref, rwl_ref, rb_ref,
               h1_ref, h1p_ref, eidx_ref, wsel_ref, pos_ref, cnt_ref, cnt_scr):
    tm = x_ref.shape[0]
    step = pl.program_id(0)

    @pl.when(step == 0)
    def _():
        cnt_scr[...] = jnp.zeros_like(cnt_scr)

    h = _layer_norm(x_ref[...], g_ref[...], b_ref[...])
    bb = jnp.dot(yb_ref[...], wbb_ref[...], preferred_element_type=F32)
    merged = ma_ref[...].astype(F32) + gb_ref[...].astype(F32) * bb
    o = jnp.dot(merged.astype(BF16), wout_ref[...], preferred_element_type=F32)
    h1 = _layer_norm(DEEPNORM_ALPHA * h + o, l1g_ref[...], l1b_ref[...])
    h1_ref[...] = h1
    _pack_rows(h1p_ref, h1)

    hh = h1.astype(BF16)
    hl = (h1 - hh.astype(F32)).astype(BF16)
    logits = (lax.dot_general(rwh_ref[...], hh, _NT, preferred_element_type=F32)
              + lax.dot_general(rwh_ref[...], hl, _NT, preferred_element_type=F32)
              + lax.dot_general(rwl_ref[...], hh, _NT, preferred_element_type=F32))
    scores = jax.nn.sigmoid(logits)
    sel = scores + rb_ref[...]

    sub = lax.broadcasted_iota(I32, (GROUP_SIZE, tm), 0).astype(F32)
    neg = jnp.float32(-jnp.inf)
    selg = [sel[g * GROUP_SIZE:(g + 1) * GROUP_SIZE] for g in range(N_GROUPS)]
    scg = [scores[g * GROUP_SIZE:(g + 1) * GROUP_SIZE] for g in range(N_GROUPS)]
    eid = [sub + float(g * GROUP_SIZE) for g in range(N_GROUPS)]

    grp = []
    for g in range(N_GROUPS):
        m1 = jnp.max(selg[g], axis=0, keepdims=True)
        i1 = jnp.min(jnp.where(selg[g] == m1, sub, float(GROUP_SIZE)), axis=0, keepdims=True)
        m2 = jnp.max(jnp.where(sub == i1, neg, selg[g]), axis=0, keepdims=True)
        grp.append(m1 + m2)
    cur = []
    for g in range(N_GROUPS):
        beaten = jnp.zeros((1, tm), F32)
        for o_g in range(N_GROUPS):
            if o_g == g:
                continue
            wins = (grp[o_g] > grp[g]) if o_g > g else (grp[o_g] >= grp[g])
            beaten = beaten + jnp.where(wins, 1.0, 0.0)
        cur.append(jnp.where(beaten < float(TOPK_GROUPS), selg[g], neg))

    def all_reduce(vals, op):
        acc = vals[0]
        for v in vals[1:]:
            acc = op(acc, v)
        return acc

    idx_rows, w_rows = [], []
    member = [jnp.zeros((GROUP_SIZE, tm), F32) for _ in range(N_GROUPS)]
    for _ in range(TOP_K):
        m = jnp.max(all_reduce(cur, jnp.maximum), axis=0, keepdims=True)
        cand = [jnp.where(cur[g] == m, eid[g], float(N_EXPERTS)) for g in range(N_GROUPS)]
        idx = jnp.min(all_reduce(cand, jnp.minimum), axis=0, keepdims=True)
        hit = [eid[g] == idx for g in range(N_GROUPS)]
        w = jnp.sum(all_reduce([jnp.where(hit[g], scg[g], 0.0) for g in range(N_GROUPS)], jnp.add),
                    axis=0, keepdims=True)
        cur = [jnp.where(hit[g], neg, cur[g]) for g in range(N_GROUPS)]
        member = [jnp.where(hit[g], 1.0, member[g]) for g in range(N_GROUPS)]
        idx_rows.append(idx)
        w_rows.append(w)

    wsum = all_reduce(w_rows, jnp.add)
    for kk in range(TOP_K):
        eidx_ref[kk:kk + 1, :] = idx_rows[kk].astype(I32)
        wsel_ref[kk:kk + 1, :] = w_rows[kk] / wsum * ROUTED_SCALE

    tt = lax.broadcasted_iota(I32, (tm, tm), 0)
    uu = lax.broadcasted_iota(I32, (tm, tm), 1)
    before = (tt < uu).astype(BF16)
    memb = jnp.concatenate(member, axis=0)
    prefix = jnp.dot(memb.astype(BF16), before, preferred_element_type=F32) + cnt_scr[...]
    for kk in range(TOP_K):
        picked = [jnp.where(eid[g] == idx_rows[kk],
                            prefix[g * GROUP_SIZE:(g + 1) * GROUP_SIZE], 0.0) for g in range(N_GROUPS)]
        pos_ref[kk:kk + 1, :] = jnp.sum(all_reduce(picked, jnp.add), axis=0, keepdims=True).astype(I32)
    total = cnt_scr[...] + jnp.sum(memb, axis=1, keepdims=True)
    cnt_scr[...] = total
    cnt_ref[...] = jnp.broadcast_to(total, cnt_ref.shape).astype(I32)


def _post(x2, ln_g, ln_b, ma, gb, yb, w_bb, w_out, l1g, l1b, rw_hi, rw_lo, r_bias, *, tm):
    t = x2.shape[0]
    row = lambda w: pl.BlockSpec((tm, w), lambda i: (i, 0))
    tok = pl.BlockSpec((TOP_K, tm), lambda i: (0, i))
    return pl.pallas_call(
        _post_body,
        grid=(t // tm,),
        in_specs=[
            row(D_MODEL), _const_spec((1, D_MODEL)), _const_spec((1, D_MODEL)),
            row(D_MODEL), row(D_MODEL), row(SB_WIDTH),
            _const_spec(w_bb.shape), _const_spec(w_out.shape),
            _const_spec((1, D_MODEL)), _const_spec((1, D_MODEL)),
            _const_spec(rw_hi.shape), _const_spec(rw_lo.shape), _const_spec(r_bias.shape),
        ],
        out_specs=[row(D_MODEL), pl.BlockSpec((tm * ROW_SUB, LANES), lambda i: (i, 0)),
                   tok, tok, tok, pl.BlockSpec((N_EXPERTS, LANES), lambda i: (0, 0))],
        out_shape=[
            jax.ShapeDtypeStruct((t, D_MODEL), F32),
            jax.ShapeDtypeStruct((t * ROW_SUB, LANES), U32),
            jax.ShapeDtypeStruct((TOP_K, t), I32),
            jax.ShapeDtypeStruct((TOP_K, t), F32),
            jax.ShapeDtypeStruct((TOP_K, t), I32),
            jax.ShapeDtypeStruct((N_EXPERTS, LANES), I32),
        ],
        scratch_shapes=[pltpu.VMEM((N_EXPERTS, 1), F32)],
        compiler_params=pltpu.CompilerParams(
            dimension_semantics=("arbitrary",), vmem_limit_bytes=VMEM_LIMIT),
        name="post",
    )(x2, ln_g, ln_b, ma, gb, yb, w_bb, w_out, l1g, l1b, rw_hi, rw_lo, r_bias)


def _slots_body(ps_ref, eidx_ref, pos_ref, dest_ref):
    eidx = eidx_ref[...]
    dest = pos_ref[...]
    for e in range(N_EXPERTS):
        dest = dest + jnp.where(eidx == e, ps_ref[e], 0)
    dest_ref[...] = dest


def _slots(pad_starts, eidx, pos):
    spec = pl.BlockSpec(eidx.shape, lambda i, ps: (0, 0))
    return pl.pallas_call(
        _slots_body,
        grid_spec=pltpu.PrefetchScalarGridSpec(
            num_scalar_prefetch=1, grid=(1,), in_specs=[spec, spec], out_specs=spec),
        out_shape=jax.ShapeDtypeStruct(eidx.shape, I32),
        name="slots",
    )(pad_starts, eidx, pos)


def _tile_of(r):
    return pl.ds(pl.multiple_of(r * ROW_SUB, ROW_SUB), ROW_SUB)


def _dispatch_body(zrow_ref, dest_ref, h_ref, xs_ref, zero_scr, zsem, sem):
    tm = h_ref.shape[0] // ROW_SUB
    step = pl.program_id(0)

    def zero_copy(r):
        rows = pl.ds(pl.multiple_of(r * ROW_SUB, ROW_BLOCK * ROW_SUB), ROW_BLOCK * ROW_SUB)
        return pltpu.make_async_copy(zero_scr, xs_ref.at[rows], zsem)

    def row_copy(t, d):
        return pltpu.make_async_copy(h_ref.at[_tile_of(t)], xs_ref.at[_tile_of(d)], sem)

    @pl.when(step == 0)
    def _():
        zero_scr[...] = jnp.zeros_like(zero_scr)

        def start(e, c):
            r = zrow_ref[e]

            @pl.when(r >= 0)
            def _():
                zero_copy(jnp.maximum(r, 0)).start()
            return c

        def wait(e, c):
            @pl.when(zrow_ref[e] >= 0)
            def _():
                zero_copy(0).wait()
            return c

        lax.fori_loop(0, N_EXPERTS, start, 0)
        lax.fori_loop(0, N_EXPERTS, wait, 0)

    def issue(t, c):
        for kk in range(TOP_K):
            row_copy(t, dest_ref[t * TOP_K + kk]).start(priority=kk % 2)
        return c

    def drain(t, c):
        for kk in range(TOP_K):
            row_copy(0, 0).wait()
        return c

    lax.fori_loop(0, tm, issue, 0)
    lax.fori_loop(0, tm, drain, 0)


def _dispatch(zrow, dest_flat, h1t, n_rows, *, tm):
    t = h1t.shape[0] // ROW_SUB
    return pl.pallas_call(
        _dispatch_body,
        grid_spec=pltpu.PrefetchScalarGridSpec(
            num_scalar_prefetch=1,
            grid=(t // tm,),
            in_specs=[
                pl.BlockSpec((tm * TOP_K,), lambda i, z: (i,), memory_space=pltpu.SMEM),
                pl.BlockSpec((tm * ROW_SUB, LANES), lambda i, z: (i, 0)),
            ],
            out_specs=pl.BlockSpec(memory_space=pl.ANY),
            scratch_shapes=[pltpu.VMEM((ROW_BLOCK * ROW_SUB, LANES), U32),
                            pltpu.SemaphoreType.DMA(()), pltpu.SemaphoreType.DMA(())],
        ),
        out_shape=jax.ShapeDtypeStruct((n_rows * ROW_SUB, LANES), U32),
        compiler_params=pltpu.CompilerParams(
            dimension_semantics=("arbitrary",), vmem_limit_bytes=VMEM_LIMIT),
        name="dispatch",
    )(zrow, dest_flat, h1t)


def _experts_body(be_ref, nu_ref, xs_ref, wg_ref, wu_ref, wd_ref, ys_ref, wg_scr, wu_scr, wd_scr):
    blk = pl.program_id(0)
    prev = jnp.maximum(blk - 1, 0)

    @pl.when((blk < nu_ref[0]) & ((blk == 0) | (be_ref[blk] != be_ref[prev])))
    def _():
        wg_scr[...] = wg_ref[0].astype(BF16)
        wu_scr[...] = wu_ref[0].astype(BF16)
        wd_scr[...] = wd_ref[0].astype(BF16)

    @pl.when(blk < nu_ref[0])
    def _():
        xb = jnp.concatenate(_unpack_rows(xs_ref, ROW_BLOCK), axis=1).astype(BF16)
        gate = jnp.dot(xb, wg_scr[...], preferred_element_type=F32)
        up = jnp.dot(xb, wu_scr[...], preferred_element_type=F32)
        hid = (jax.nn.silu(gate) * up).astype(BF16)
        _pack_rows(ys_ref, jnp.dot(hid, wd_scr[...], preferred_element_type=F32))

    @pl.when(blk >= nu_ref[0])
    def _():
        ys_ref[...] = jnp.zeros_like(ys_ref)


def _experts(block_e, n_used, xs, w_gate, w_up, w_down):
    n_blocks = xs.shape[0] // (ROW_BLOCK * ROW_SUB)
    last = lambda b, be, nu: jnp.minimum(b, nu[0] - 1)
    blk = (ROW_BLOCK * ROW_SUB, LANES)
    return pl.pallas_call(
        _experts_body,
        grid_spec=pltpu.PrefetchScalarGridSpec(
            num_scalar_prefetch=2,
            grid=(n_blocks,),
            in_specs=[
                pl.BlockSpec(blk, lambda b, be, nu: (last(b, be, nu), 0)),
                pl.BlockSpec((1, D_MODEL, EXPERT_FF), lambda b, be, nu: (be[last(b, be, nu)], 0, 0)),
                pl.BlockSpec((1, D_MODEL, EXPERT_FF), lambda b, be, nu: (be[last(b, be, nu)], 0, 0)),
                pl.BlockSpec((1, EXPERT_FF, D_MODEL), lambda b, be, nu: (be[last(b, be, nu)], 0, 0)),
            ],
            out_specs=pl.BlockSpec(blk, lambda b, be, nu: (b, 0)),
            scratch_shapes=[pltpu.VMEM((D_MODEL, EXPERT_FF), BF16),
                            pltpu.VMEM((D_MODEL, EXPERT_FF), BF16),
                            pltpu.VMEM((EXPERT_FF, D_MODEL), BF16)],
        ),
        out_shape=jax.ShapeDtypeStruct(xs.shape, U32),
        compiler_params=pltpu.CompilerParams(
            dimension_semantics=("arbitrary",), vmem_limit_bytes=VMEM_LIMIT),
        name="experts",
    )(block_e, n_used, xs, w_gate, w_up, w_down)


def _combine_body(dest_ref, h1_ref, w_ref, p_ref, ys_ref, shg_ref, shu_ref, shd_ref,
                  pg_ref, pp_ref, g_ref, b_ref, o_ref, buf, sem):
    tm = h1_ref.shape[0]

    def row_copy(kk, t, d):
        return pltpu.make_async_copy(ys_ref.at[_tile_of(d)], buf.at[_tile_of(kk * tm + t)], sem)

    def issue(t, c):
        for kk in range(TOP_K):
            row_copy(kk, t, dest_ref[t * TOP_K + kk]).start(priority=kk % 2)
        return c

    def drain(t, c):
        for kk in range(TOP_K):
            row_copy(kk, 0, 0).wait()
        return c

    lax.fori_loop(0, tm, issue, 0)

    h1 = h1_ref[...]
    hb = h1.astype(BF16)
    hid = (jax.nn.silu(jnp.dot(hb, shg_ref[...], preferred_element_type=F32))
           * jnp.dot(hb, shu_ref[...], preferred_element_type=F32)).astype(BF16)
    r = DEEPNORM_ALPHA * h1 + jnp.dot(hid, shd_ref[...], preferred_element_type=F32)
    emb = jnp.dot(p_ref[...].astype(BF16), pp_ref[...], preferred_element_type=F32)

    lax.fori_loop(0, tm, drain, 0)
    w = w_ref[...]
    cols = [r[:, c * LANES:(c + 1) * LANES] for c in range(2 * ROW_SUB)]
    for kk in range(TOP_K):
        wk = jnp.broadcast_to(w[:, kk:kk + 1], (tm, LANES))
        for c, chunk in enumerate(_unpack_rows(buf, tm, base=kk * tm)):
            cols[c] = cols[c] + wk * chunk
    r = jnp.concatenate(cols, axis=1)
    gate = jax.nn.sigmoid(jnp.dot(r.astype(BF16), pg_ref[...], preferred_element_type=F32))
    o_ref[...] = _layer_norm(r + gate * emb, g_ref[...], b_ref[...])


def _combine(dest_flat, h1, wtok, p2, ys, sh_g, sh_u, sh_d, ple_g, ple_p, l2g, l2b, *, tm):
    t = h1.shape[0]
    row = lambda w: pl.BlockSpec((tm, w), lambda i: (i, 0))
    return pl.pallas_call(
        _combine_body,
        grid=(t // tm,),
        in_specs=[
            pl.BlockSpec((tm * TOP_K,), lambda i: (i,), memory_space=pltpu.SMEM),
            row(D_MODEL), row(TOP_K), row(PLE_DIM),
            pl.BlockSpec(memory_space=pl.ANY),
            _const_spec(sh_g.shape), _const_spec(sh_u.shape), _const_spec(sh_d.shape),
            _const_spec(ple_g.shape), _const_spec(ple_p.shape),
            _const_spec((1, D_MODEL)), _const_spec((1, D_MODEL)),
        ],
        out_specs=row(D_MODEL),
        out_shape=jax.ShapeDtypeStruct((t, D_MODEL), F32),
        scratch_shapes=[pltpu.VMEM((TOP_K * tm * ROW_SUB, LANES), U32),
                        pltpu.SemaphoreType.DMA(())],
        compiler_params=pltpu.CompilerParams(
            dimension_semantics=("arbitrary",), vmem_limit_bytes=VMEM_LIMIT),
        name="combine",
    )(dest_flat, h1, wtok, p2, ys, sh_g, sh_u, sh_d, ple_g, ple_p, l2g, l2b)


def _block_layout(counts, n_blocks):
    padded = (counts + ROW_BLOCK - 1) // ROW_BLOCK * ROW_BLOCK
    pad_ends = jnp.cumsum(padded)
    pad_starts = pad_ends - padded
    block_row0 = jnp.arange(n_blocks, dtype=I32) * ROW_BLOCK
    block_e = jnp.minimum(jnp.sum(pad_ends[None, :] <= block_row0[:, None], axis=1),
                          N_EXPERTS - 1).astype(I32)
    n_used = (pad_ends[-1:] // ROW_BLOCK).astype(I32)
    has_pad = (counts % ROW_BLOCK) != 0
    zrow = jnp.where(has_pad, pad_ends - ROW_BLOCK, -1).astype(I32)
    return pad_starts.astype(I32), block_e, n_used, zrow


def kernel(x, p, ln_in_g, ln_in_b, w_in, sg_ln_g, sg_ln_b, sg_w_s, sg_b_s, w_branch_a, w_branch_b,
           w_out, ln1_g, ln1_b, router_w, router_bias, moe_w_gate, moe_w_up, moe_w_down,
           sh_w_gate, sh_w_up, sh_w_down, ple_w_proj, ple_w_gate, ln2_g, ln2_b):
    bsz, seq, dm = x.shape
    t = bsz * seq
    vec = lambda a: a.reshape(1, -1).astype(F32)
    assert DEPTH == 1
    h = x.reshape(t, dm)
    for i in range(DEPTH):
        q, k, v, ma, gb = _inproj(
            h, vec(ln_in_g), vec(ln_in_b), w_in[i].astype(BF16), vec(sg_ln_g[i]), vec(sg_ln_b[i]),
            sg_w_s[i], sg_b_s[i].T, w_branch_a[i].astype(BF16), tm=INPROJ_ROWS)
        yb = _attn(q.reshape(bsz, seq, SB_WIDTH), k.reshape(bsz, seq, SB_WIDTH),
                   v.reshape(bsz, seq, SB_WIDTH), tq=ATTN_ROWS).reshape(t, SB_WIDTH)

        rw_t = router_w[i].T
        rw_hi = rw_t.astype(BF16)
        rw_lo = (rw_t - rw_hi.astype(F32)).astype(BF16)
        h1, h1p, eidx, wsel, pos, cnt = _post(
            h, vec(ln_in_g), vec(ln_in_b), ma, gb, yb, w_branch_b[i].astype(BF16),
            w_out[i].astype(BF16), vec(ln1_g[i]), vec(ln1_b[i]), rw_hi, rw_lo,
            router_bias[i].reshape(N_EXPERTS, 1).astype(F32), tm=POST_ROWS)

        n_blocks = t * TOP_K // ROW_BLOCK + N_EXPERTS
        pad_starts, block_e, n_used, zrow = _block_layout(cnt[:, 0], n_blocks)
        dest = _slots(pad_starts, eidx, pos).T.reshape(t * TOP_K)

        xs = _dispatch(zrow, dest, h1p, n_blocks * ROW_BLOCK, tm=DISPATCH_ROWS)
        ys = _experts(block_e, n_used, xs, moe_w_gate[i], moe_w_up[i], moe_w_down[i])
        h = _combine(dest, h1, wsel.T, p[i].reshape(t, PLE_DIM), ys,
                     sh_w_gate[i].astype(BF16), sh_w_up[i].astype(BF16), sh_w_down[i].astype(BF16),
                     ple_w_gate[i].astype(BF16), ple_w_proj[i].astype(BF16),
                     vec(ln2_g[i]), vec(ln2_b[i]), tm=COMBINE_ROWS)
    return h.reshape(bsz, seq, dm)
```

```python
import functools
import math

import jax
import jax.numpy as jnp
from jax import lax
from jax.experimental import pallas as pl
from jax.experimental.pallas import tpu as pltpu

F32 = jnp.float32
BF16 = jnp.bfloat16
I32 = jnp.int32

D_MODEL = 1024
DEPTH = 1
CHUNK = 64
PLE_DIM = 256
SG_BLOCK = 128
SG_GROUPS = 4
SG_WIDTH = 512
SB_HEADS = 8
SB_HEAD_DIM = 64
SB_WIDTH = SB_HEADS * SB_HEAD_DIM
N_EXPERTS = 64
N_GROUPS = 8
GROUP_SIZE = N_EXPERTS // N_GROUPS
TOPK_GROUPS = 4
TOP_K = 8
EXPERT_FF = 256
SHARED_FF = 256
ROUTED_SCALE = 2.5
LN_EPS = 1e-5
DEEPNORM_ALPHA = (2.0 * DEPTH) ** 0.25

_OFF_U, _OFF_V, _OFF_Q, _OFF_K, _OFF_VS, _OFF_GA, _OFF_GB, _OFF_END = (
    0, 512, 1024, 1536, 2048, 2560, 3584, 4608)

LANES = 128
ROW_SUB = 4
assert 2 * ROW_SUB * LANES == D_MODEL
U32 = jnp.uint32
ATTN_LANES = 256
ATTN_HEADS = ATTN_LANES // SB_HEAD_DIM
EXP_UNDERFLOW = -110.0
ROW_BLOCK = 512
INPROJ_ROWS, POST_ROWS, ATTN_ROWS, DISPATCH_ROWS, COMBINE_ROWS = 512, 512, 256, 256, 1024
VMEM_LIMIT = 56 * 1024 * 1024

_NT = (((1,), (1,)), ((), ()))


def _layer_norm(x, g, b):
    mu = jnp.mean(x, axis=-1, keepdims=True)
    xc = x - mu
    var = jnp.mean(xc * xc, axis=-1, keepdims=True)
    return xc * lax.rsqrt(var + LN_EPS) * g + b


def _unpack_rows(ref, n, base=0):
    lo, hi = [], []
    for s in range(ROW_SUB):
        w = ref[pl.ds(base * ROW_SUB + s, n, stride=ROW_SUB), :]
        lo.append(lax.bitcast_convert_type(w << 16, F32))
        hi.append(lax.bitcast_convert_type(w & jnp.uint32(0xFFFF0000), F32))
    return lo + hi


def _pack_rows(ref, val):
    n = val.shape[0]
    bits = lambda c: lax.bitcast_convert_type(
        val[:, c * LANES:(c + 1) * LANES].astype(BF16).astype(F32), jnp.uint32)
    for s in range(ROW_SUB):
        ref[pl.ds(s, n, stride=ROW_SUB), :] = bits(s + ROW_SUB) | (bits(s) >> 16)


def _const_spec(shape):
    nd = len(shape)
    return pl.BlockSpec(shape, lambda *_: (0,) * nd, pipeline_mode=pl.Buffered(1))


def _inproj_body(x_ref, g_ref, b_ref, w_ref, sgg_ref, sgb_ref, ws_ref, bs_ref, wba_ref,
                 q_ref, k_ref, v_ref, ma_ref, gb_ref, ya_scr):
    tm = x_ref.shape[0]
    hb = _layer_norm(x_ref[...], g_ref[...], b_ref[...]).astype(BF16)

    def proj(lo, hi):
        return jnp.dot(hb, w_ref[:, lo:hi], preferred_element_type=F32)

    q_ref[...] = (proj(_OFF_Q, _OFF_K) * (1.0 / math.sqrt(SB_HEAD_DIM))).astype(BF16)
    k_ref[...] = proj(_OFF_K, _OFF_VS).astype(BF16)
    v_ref[...] = proj(_OFF_VS, _OFF_GA).astype(BF16)
    gb_ref[...] = jax.nn.sigmoid(proj(_OFF_GB, _OFF_END)).astype(BF16)

    yu = jax.nn.gelu(proj(_OFF_U, _OFF_V))
    vn = _layer_norm(jax.nn.gelu(proj(_OFF_V, _OFF_Q)), sgg_ref[...], sgb_ref[...]).astype(BF16)

    pi = lax.broadcasted_iota(I32, (SG_BLOCK, SG_BLOCK), 0) // CHUNK
    pj = lax.broadcasted_iota(I32, (SG_BLOCK, SG_BLOCK), 1) // CHUNK
    chunk_causal = pj <= pi
    gch = SG_WIDTH // SG_GROUPS
    for g in range(SG_GROUPS):
        wm = jnp.where(chunk_causal, ws_ref[g], 0.0).astype(BF16)
        bias = bs_ref[:, g:g + 1]
        for r in range(tm // SG_BLOCK):
            rows = slice(r * SG_BLOCK, (r + 1) * SG_BLOCK)
            cols = slice(g * gch, (g + 1) * gch)
            s = jnp.dot(wm, vn[rows, cols], preferred_element_type=F32) + bias
            ya_scr[rows, cols] = (yu[rows, cols] * s).astype(BF16)

    ba = jnp.dot(ya_scr[...], wba_ref[...], preferred_element_type=F32)
    ma_ref[...] = (jax.nn.sigmoid(proj(_OFF_GA, _OFF_GB)) * ba).astype(BF16)


def _inproj(x2, ln_g, ln_b, w_in, sg_g, sg_b, w_s, b_sT, w_ba, *, tm):
    t = x2.shape[0]
    row = lambda w: pl.BlockSpec((tm, w), lambda i: (i, 0))
    return pl.pallas_call(
        _inproj_body,
        grid=(t // tm,),
        in_specs=[
            row(D_MODEL),
            _const_spec((1, D_MODEL)), _const_spec((1, D_MODEL)),
            _const_spec(w_in.shape),
            _const_spec((1, SG_WIDTH)), _const_spec((1, SG_WIDTH)),
            _const_spec(w_s.shape), _const_spec(b_sT.shape),
            _const_spec(w_ba.shape),
        ],
        out_specs=[row(SB_WIDTH), row(SB_WIDTH), row(SB_WIDTH), row(D_MODEL), row(D_MODEL)],
        out_shape=[
            jax.ShapeDtypeStruct((t, SB_WIDTH), BF16),
            jax.ShapeDtypeStruct((t, SB_WIDTH), BF16),
            jax.ShapeDtypeStruct((t, SB_WIDTH), BF16),
            jax.ShapeDtypeStruct((t, D_MODEL), BF16),
            jax.ShapeDtypeStruct((t, D_MODEL), BF16),
        ],
        scratch_shapes=[pltpu.VMEM((tm, SG_WIDTH), BF16)],
        compiler_params=pltpu.CompilerParams(
            dimension_semantics=("arbitrary",), vmem_limit_bytes=VMEM_LIMIT),
        name="inproj",
    )(x2, ln_g, ln_b, w_in, sg_g, sg_b, w_s, b_sT, w_ba)


def _attn_body(q_ref, k_ref, v_ref, o_ref, acc_scr, car_scr, *, tq):
    qi = pl.program_id(2)
    q2 = q_ref[0]
    lane_head = lax.broadcasted_iota(I32, (tq, ATTN_LANES), 1) // SB_HEAD_DIM
    zero = jnp.zeros_like(q2)
    in_head = [lane_head == h for h in range(ATTN_HEADS)]
    qn = [jnp.where(in_head[h], -q2, zero) for h in range(ATTN_HEADS)]
    rr = lax.broadcasted_iota(I32, (tq, tq), 0)
    cc = lax.broadcasted_iota(I32, (tq, tq), 1)
    later = (rr > cc).astype(BF16)
    causal = cc < rr
    sign = jnp.uint32(0x80000000)

    def blocks(starts, diag):
        kbs = [k_ref[0, pl.ds(pl.multiple_of(s0, tq), tq), :] for s0 in starts]
        vbs = [v_ref[0, pl.ds(pl.multiple_of(s0, tq), tq), :] for s0 in starts]
        for h in range(ATTN_HEADS):
            car = car_scr[h]
            pv = None
            for n, (kb, vb) in enumerate(zip(kbs, vbs)):
                masked = diag and n == 0
                y = lax.dot_general(qn[h], kb, _NT, preferred_element_type=F32)
                neg_abs = lax.bitcast_convert_type(
                    lax.bitcast_convert_type(y, jnp.uint32) | sign, F32)
                lm = jnp.minimum(y, 0.0) - jnp.log(1.0 + jnp.exp(neg_abs))
                lb = lm - y
                if masked:
                    lm = jnp.where(causal, lm, 0.0)
                tail = jnp.dot(lm.astype(BF16), later, preferred_element_type=F32)
                a = jnp.exp(lb + tail + car)
                if masked:
                    a = jnp.where(causal, a, 0.0)
                d = jnp.dot(a.astype(BF16), vb, preferred_element_type=F32)
                pv = d if pv is None else pv + d
                car = car + jnp.sum(lm, axis=1, keepdims=True)
            acc_scr[h] += pv
            car_scr[h] = car

    acc_scr[...] = jnp.zeros_like(acc_scr)
    car_scr[...] = jnp.zeros_like(car_scr)
    blocks([qi * tq], True)

    def live():
        top = car_scr[0]
        for h in range(1, ATTN_HEADS):
            top = jnp.maximum(top, car_scr[h])
        return (jnp.max(top) > EXP_UNDERFLOW).astype(I32)

    def earlier(state):
        j, _ = state
        blocks([(qi - 1 - j) * tq], False)
        return j + 1, live()

    lax.while_loop(lambda s: (s[0] < qi) & (s[1] > 0), earlier, (jnp.int32(0), live()))
    out = acc_scr[0]
    for h in range(1, ATTN_HEADS):
        out = jnp.where(in_head[h], acc_scr[h], out)
    o_ref[0] = out.astype(BF16)


def _attn(q3, k3, v3, *, tq):
    b, s, _ = q3.shape
    qspec = pl.BlockSpec((1, tq, ATTN_LANES), lambda bi, hp, qi: (bi, qi, hp))
    kvspec = pl.BlockSpec((1, s, ATTN_LANES), lambda bi, hp, qi: (bi, 0, hp))
    return pl.pallas_call(
        functools.partial(_attn_body, tq=tq),
        grid=(b, SB_WIDTH // ATTN_LANES, s // tq),
        in_specs=[qspec, kvspec, kvspec],
        out_specs=qspec,
        out_shape=jax.ShapeDtypeStruct((b, s, SB_WIDTH), BF16),
        scratch_shapes=[pltpu.VMEM((ATTN_HEADS, tq, ATTN_LANES), F32),
                        pltpu.VMEM((ATTN_HEADS, tq, 1), F32)],
        compiler_params=pltpu.CompilerParams(
            dimension_semantics=("arbitrary", "arbitrary", "arbitrary"),
            vmem_limit_bytes=VMEM_LIMIT),
        name="attn",
    )(q3, k3, v3)


def _post_body(x_ref, g_ref, b_ref, ma_ref, gb_ref, yb_ref, wbb_ref, wout_ref, l1g_ref, l1b_ref,
               rwh_ref, rwl_ref, rb_ref,
               h1_ref, h1p_ref, eidx_ref, wsel_ref, pos_ref, cnt_ref, cnt_scr):
    tm = x_ref.shape[0]
    step = pl.program_id(0)

    @pl.when(step == 0)
    def _():
        cnt_scr[...] = jnp.zeros_like(cnt_scr)

    h = _layer_norm(x_ref[...], g_ref[...], b_ref[...])
    bb = jnp.dot(yb_ref[...], wbb_ref[...], preferred_element_type=F32)
    merged = ma_ref[...].astype(F32) + gb_ref[...].astype(F32) * bb
    o = jnp.dot(merged.astype(BF16), wout_ref[...], preferred_element_type=F32)
    h1 = _layer_norm(DEEPNORM_ALPHA * h + o, l1g_ref[...], l1b_ref[...])
    h1_ref[...] = h1
    _pack_rows(h1p_ref, h1)

    hh = h1.astype(BF16)
    hl = (h1 - hh.astype(F32)).astype(BF16)
    logits = (lax.dot_general(rwh_ref[...], hh, _NT, preferred_element_type=F32)
              + lax.dot_general(rwh_ref[...], hl, _NT, preferred_element_type=F32)
              + lax.dot_general(rwl_ref[...], hh, _NT, preferred_element_type=F32))
    scores = jax.nn.sigmoid(logits)
    sel = scores + rb_ref[...]

    sub = lax.broadcasted_iota(I32, (GROUP_SIZE, tm), 0).astype(F32)
    neg = jnp.float32(-jnp.inf)
    selg = [sel[g * GROUP_SIZE:(g + 1) * GROUP_SIZE] for g in range(N_GROUPS)]
    scg = [scores[g * GROUP_SIZE:(g + 1) * GROUP_SIZE] for g in range(N_GROUPS)]
    eid = [sub + float(g * GROUP_SIZE) for g in range(N_GROUPS)]

    grp = []
    for g in range(N_GROUPS):
        m1 = jnp.max(selg[g], axis=0, keepdims=True)
        i1 = jnp.min(jnp.where(selg[g] == m1, sub, float(GROUP_SIZE)), axis=0, keepdims=True)
        m2 = jnp.max(jnp.where(sub == i1, neg, selg[g]), axis=0, keepdims=True)
        grp.append(m1 + m2)
    cur = []
    for g in range(N_GROUPS):
        beaten = jnp.zeros((1, tm), F32)
        for o_g in range(N_GROUPS):
            if o_g == g:
                continue
            wins = (grp[o_g] > grp[g]) if o_g > g else (grp[o_g] >= grp[g])
            beaten = beaten + jnp.where(wins, 1.0, 0.0)
        cur.append(jnp.where(beaten < float(TOPK_GROUPS), selg[g], neg))

    def all_reduce(vals, op):
        acc = vals[0]
        for v in vals[1:]:
            acc = op(acc, v)
        return acc

    idx_rows, w_rows = [], []
    member = [jnp.zeros((GROUP_SIZE, tm), F32) for _ in range(N_GROUPS)]
    for _ in range(TOP_K):
        m = jnp.max(all_reduce(cur, jnp.maximum), axis=0, keepdims=True)
        cand = [jnp.where(cur[g] == m, eid[g], float(N_EXPERTS)) for g in range(N_GROUPS)]
        idx = jnp.min(all_reduce(cand, jnp.minimum), axis=0, keepdims=True)
        hit = [eid[g] == idx for g in range(N_GROUPS)]
        w = jnp.sum(all_reduce([jnp.where(hit[g], scg[g], 0.0) for g in range(N_GROUPS)], jnp.add),
                    axis=0, keepdims=True)
        cur = [jnp.where(hit[g], neg, cur[g]) for g in range(N_GROUPS)]
        member = [jnp.where(hit[g], 1.0, member[g]) for g in range(N_GROUPS)]
        idx_rows.append(idx)
        w_rows.append(w)

    wsum = all_reduce(w_rows, jnp.add)
    for kk in range(TOP_K):
        eidx_ref[kk:kk + 1, :] = idx_rows[kk].astype(I32)
        wsel_ref[kk:kk + 1, :] = w_rows[kk] / wsum * ROUTED_SCALE

    tt = lax.broadcasted_iota(I32, (tm, tm), 0)
    uu = lax.broadcasted_iota(I32, (tm, tm), 1)
    before = (tt < uu).astype(BF16)
    memb = jnp.concatenate(member, axis=0)
    prefix = jnp.dot(memb.astype(BF16), before, preferred_element_type=F32) + cnt_scr[...]
    for kk in range(TOP_K):
        picked = [jnp.where(eid[g] == idx_rows[kk],
                            prefix[g * GROUP_SIZE:(g + 1) * GROUP_SIZE], 0.0) for g in range(N_GROUPS)]
        pos_ref[kk:kk + 1, :] = jnp.sum(all_reduce(picked, jnp.add), axis=0, keepdims=True).astype(I32)
    total = cnt_scr[...] + jnp.sum(memb, axis=1, keepdims=True)
    cnt_scr[...] = total
    cnt_ref[...] = jnp.broadcast_to(total, cnt_ref.shape).astype(I32)


def _post(x2, ln_g, ln_b, ma, gb, yb, w_bb, w_out, l1g, l1b, rw_hi, rw_lo, r_bias, *, tm):
    t = x2.shape[0]
    row = lambda w: pl.BlockSpec((tm, w), lambda i: (i, 0))
    tok = pl.BlockSpec((TOP_K, tm), lambda i: (0, i))
    return pl.pallas_call(
        _post_body,
        grid=(t // tm,),
        in_specs=[
            row(D_MODEL), _const_spec((1, D_MODEL)), _const_spec((1, D_MODEL)),
            row(D_MODEL), row(D_MODEL), row(SB_WIDTH),
            _const_spec(w_bb.shape), _const_spec(w_out.shape),
            _const_spec((1, D_MODEL)), _const_spec((1, D_MODEL)),
            _const_spec(rw_hi.shape), _const_spec(rw_lo.shape), _const_spec(r_bias.shape),
        ],
        out_specs=[row(D_MODEL), pl.BlockSpec((tm * ROW_SUB, LANES), lambda i: (i, 0)),
                   tok, tok, tok, pl.BlockSpec((N_EXPERTS, LANES), lambda i: (0, 0))],
        out_shape=[
            jax.ShapeDtypeStruct((t, D_MODEL), F32),
            jax.ShapeDtypeStruct((t * ROW_SUB, LANES), U32),
            jax.ShapeDtypeStruct((TOP_K, t), I32),
            jax.ShapeDtypeStruct((TOP_K, t), F32),
            jax.ShapeDtypeStruct((TOP_K, t), I32),
            jax.ShapeDtypeStruct((N_EXPERTS, LANES), I32),
        ],
        scratch_shapes=[pltpu.VMEM((N_EXPERTS, 1), F32)],
        compiler_params=pltpu.CompilerParams(
            dimension_semantics=("arbitrary",), vmem_limit_bytes=VMEM_LIMIT),
        name="post",
    )(x2, ln_g, ln_b, ma, gb, yb, w_bb, w_out, l1g, l1b, rw_hi, rw_lo, r_bias)


def _slots_body(ps_ref, eidx_ref, pos_ref, dest_ref):
    eidx = eidx_ref[...]
    dest = pos_ref[...]
    for e in range(N_EXPERTS):
        dest = dest + jnp.where(eidx == e, ps_ref[e], 0)
    dest_ref[...] = dest


def _slots(pad_starts, eidx, pos):
    spec = pl.BlockSpec(eidx.shape, lambda i, ps: (0, 0))
    return pl.pallas_call(
        _slots_body,
        grid_spec=pltpu.PrefetchScalarGridSpec(
            num_scalar_prefetch=1, grid=(1,), in_specs=[spec, spec], out_specs=spec),
        out_shape=jax.ShapeDtypeStruct(eidx.shape, I32),
        name="slots",
    )(pad_starts, eidx, pos)


def _tile_of(r):
    return pl.ds(pl.multiple_of(r * ROW_SUB, ROW_SUB), ROW_SUB)


def _dispatch_body(zrow_ref, dest_ref, h_ref, xs_ref, zero_scr, zsem, sem):
    tm = h_ref.shape[0] // ROW_SUB
    step = pl.program_id(0)

    def zero_copy(r):
        rows = pl.ds(pl.multiple_of(r * ROW_SUB, ROW_BLOCK * ROW_SUB), ROW_BLOCK * ROW_SUB)
        return pltpu.make_async_copy(zero_scr, xs_ref.at[rows], zsem)

    def row_copy(t, d):
        return pltpu.make_async_copy(h_ref.at[_tile_of(t)], xs_ref.at[_tile_of(d)], sem)

    @pl.when(step == 0)
    def _():
        zero_scr[...] = jnp.zeros_like(zero_scr)

        def start(e, c):
            r = zrow_ref[e]

            @pl.when(r >= 0)
            def _():
                zero_copy(jnp.maximum(r, 0)).start()
            return c

        def wait(e, c):
            @pl.when(zrow_ref[e] >= 0)
            def _():
                zero_copy(0).wait()
            return c

        lax.fori_loop(0, N_EXPERTS, start, 0)
        lax.fori_loop(0, N_EXPERTS, wait, 0)

    def issue(t, c):
        for kk in range(TOP_K):
            row_copy(t, dest_ref[t * TOP_K + kk]).start(priority=kk % 2)
        return c

    def drain(t, c):
        for kk in range(TOP_K):
            row_copy(0, 0).wait()
        return c

    lax.fori_loop(0, tm, issue, 0)
    lax.fori_loop(0, tm, drain, 0)


def _dispatch(zrow, dest_flat, h1t, n_rows, *, tm):
    t = h1t.shape[0] // ROW_SUB
    return pl.pallas_call(
        _dispatch_body,
        grid_spec=pltpu.PrefetchScalarGridSpec(
            num_scalar_prefetch=1,
            grid=(t // tm,),
            in_specs=[
                pl.BlockSpec((tm * TOP_K,), lambda i, z: (i,), memory_space=pltpu.SMEM),
                pl.BlockSpec((tm * ROW_SUB, LANES), lambda i, z: (i, 0)),
            ],
            out_specs=pl.BlockSpec(memory_space=pl.ANY),
            scratch_shapes=[pltpu.VMEM((ROW_BLOCK * ROW_SUB, LANES), U32),
                            pltpu.SemaphoreType.DMA(()), pltpu.SemaphoreType.DMA(())],
        ),
        out_shape=jax.ShapeDtypeStruct((n_rows * ROW_SUB, LANES), U32),
        compiler_params=pltpu.CompilerParams(
            dimension_semantics=("arbitrary",), vmem_limit_bytes=VMEM_LIMIT),
        name="dispatch",
    )(zrow, dest_flat, h1t)


def _experts_body(be_ref, nu_ref, xs_ref, wg_ref, wu_ref, wd_ref, ys_ref, wg_scr, wu_scr, wd_scr):
    blk = pl.program_id(0)
    prev = jnp.maximum(blk - 1, 0)

    @pl.when((blk < nu_ref[0]) & ((blk == 0) | (be_ref[blk] != be_ref[prev])))
    def _():
        wg_scr[...] = wg_ref[0].astype(BF16)
        wu_scr[...] = wu_ref[0].astype(BF16)
        wd_scr[...] = wd_ref[0].astype(BF16)

    @pl.when(blk < nu_ref[0])
    def _():
        xb = jnp.concatenate(_unpack_rows(xs_ref, ROW_BLOCK), axis=1).astype(BF16)
        gate = jnp.dot(xb, wg_scr[...], preferred_element_type=F32)
        up = jnp.dot(xb, wu_scr[...], preferred_element_type=F32)
        hid = (jax.nn.silu(gate) * up).astype(BF16)
        _pack_rows(ys_ref, jnp.dot(hid, wd_scr[...], preferred_element_type=F32))

    @pl.when(blk >= nu_ref[0])
    def _():
        ys_ref[...] = jnp.zeros_like(ys_ref)


def _experts(block_e, n_used, xs, w_gate, w_up, w_down):
    n_blocks = xs.shape[0] // (ROW_BLOCK * ROW_SUB)
    last = lambda b, be, nu: jnp.minimum(b, nu[0] - 1)
    blk = (ROW_BLOCK * ROW_SUB, LANES)
    return pl.pallas_call(
        _experts_body,
        grid_spec=pltpu.PrefetchScalarGridSpec(
            num_scalar_prefetch=2,
            grid=(n_blocks,),
            in_specs=[
                pl.BlockSpec(blk, lambda b, be, nu: (last(b, be, nu), 0)),
                pl.BlockSpec((1, D_MODEL, EXPERT_FF), lambda b, be, nu: (be[last(b, be, nu)], 0, 0)),
                pl.BlockSpec((1, D_MODEL, EXPERT_FF), lambda b, be, nu: (be[last(b, be, nu)], 0, 0)),
                pl.BlockSpec((1, EXPERT_FF, D_MODEL), lambda b, be, nu: (be[last(b, be, nu)], 0, 0)),
            ],
            out_specs=pl.BlockSpec(blk, lambda b, be, nu: (b, 0)),
            scratch_shapes=[pltpu.VMEM((D_MODEL, EXPERT_FF), BF16),
                            pltpu.VMEM((D_MODEL, EXPERT_FF), BF16),
                            pltpu.VMEM((EXPERT_FF, D_MODEL), BF16)],
        ),
        out_shape=jax.ShapeDtypeStruct(xs.shape, U32),
        compiler_params=pltpu.CompilerParams(
            dimension_semantics=("arbitrary",), vmem_limit_bytes=VMEM_LIMIT),
        name="experts",
    )(block_e, n_used, xs, w_gate, w_up, w_down)


def _combine_body(dest_ref, h1_ref, w_ref, p_ref, ys_ref, shg_ref, shu_ref, shd_ref,
                  pg_ref, pp_ref, g_ref, b_ref, o_ref, buf, sem):
    tm = h1_ref.shape[0]

    def row_copy(kk, t, d):
        return pltpu.make_async_copy(ys_ref.at[_tile_of(d)], buf.at[_tile_of(kk * tm + t)], sem)

    def issue(t, c):
        for kk in range(TOP_K):
            row_copy(kk, t, dest_ref[t * TOP_K + kk]).start(priority=kk % 2)
        return c

    def drain(t, c):
        for kk in range(TOP_K):
            row_copy(kk, 0, 0).wait()
        return c

    lax.fori_loop(0, tm, issue, 0)

    h1 = h1_ref[...]
    hb = h1.astype(BF16)
    hid = (jax.nn.silu(jnp.dot(hb, shg_ref[...], preferred_element_type=F32))
           * jnp.dot(hb, shu_ref[...], preferred_element_type=F32)).astype(BF16)
    r = DEEPNORM_ALPHA * h1 + jnp.dot(hid, shd_ref[...], preferred_element_type=F32)
    emb = jnp.dot(p_ref[...].astype(BF16), pp_ref[...], preferred_element_type=F32)

    lax.fori_loop(0, tm, drain, 0)
    w = w_ref[...]
    cols = [r[:, c * LANES:(c + 1) * LANES] for c in range(2 * ROW_SUB)]
    for kk in range(TOP_K):
        wk = jnp.broadcast_to(w[:, kk:kk + 1], (tm, LANES))
        for c, chunk in enumerate(_unpack_rows(buf, tm, base=kk * tm)):
            cols[c] = cols[c] + wk * chunk
    r = jnp.concatenate(cols, axis=1)
    gate = jax.nn.sigmoid(jnp.dot(r.astype(BF16), pg_ref[...], preferred_element_type=F32))
    o_ref[...] = _layer_norm(r + gate * emb, g_ref[...], b_ref[...])


def _combine(dest_flat, h1, wtok, p2, ys, sh_g, sh_u, sh_d, ple_g, ple_p, l2g, l2b, *, tm):
    t = h1.shape[0]
    row = lambda w: pl.BlockSpec((tm, w), lambda i: (i, 0))
    return pl.pallas_call(
        _combine_body,
        grid=(t // tm,),
        in_specs=[
            pl.BlockSpec((tm * TOP_K,), lambda i: (i,), memory_space=pltpu.SMEM),
            row(D_MODEL), row(TOP_K), row(PLE_DIM),
            pl.BlockSpec(memory_space=pl.ANY),
            _const_spec(sh_g.shape), _const_spec(sh_u.shape), _const_spec(sh_d.shape),
            _const_spec(ple_g.shape), _const_spec(ple_p.shape),
            _const_spec((1, D_MODEL)), _const_spec((1, D_MODEL)),
        ],
        out_specs=row(D_MODEL),
        out_shape=jax.ShapeDtypeStruct((t, D_MODEL), F32),
        scratch_shapes=[pltpu.VMEM((TOP_K * tm * ROW_SUB, LANES), U32),
                        pltpu.SemaphoreType.DMA(())],
        compiler_params=pltpu.CompilerParams(
            dimension_semantics=("arbitrary",), vmem_limit_bytes=VMEM_LIMIT),
        name="combine",
    )(dest_flat, h1, wtok, p2, ys, sh_g, sh_u, sh_d, ple_g, ple_p, l2g, l2b)


def _block_layout(counts, n_blocks):
    padded = (counts + ROW_BLOCK - 1) // ROW_BLOCK * ROW_BLOCK
    pad_ends = jnp.cumsum(padded)
    pad_starts = pad_ends - padded
    block_row0 = jnp.arange(n_blocks, dtype=I32) * ROW_BLOCK
    block_e = jnp.minimum(jnp.sum(pad_ends[None, :] <= block_row0[:, None], axis=1),
                          N_EXPERTS - 1).astype(I32)
    n_used = (pad_ends[-1:] // ROW_BLOCK).astype(I32)
    has_pad = (counts % ROW_BLOCK) != 0
    zrow = jnp.where(has_pad, pad_ends - ROW_BLOCK, -1).astype(I32)
    return pad_starts.astype(I32), block_e, n_used, zrow


def kernel(x, p, ln_in_g, ln_in_b, w_in, sg_ln_g, sg_ln_b, sg_w_s, sg_b_s, w_branch_a, w_branch_b,
           w_out, ln1_g, ln1_b, router_w, router_bias, moe_w_gate, moe_w_up, moe_w_down,
           sh_w_gate, sh_w_up, sh_w_down, ple_w_proj, ple_w_gate, ln2_g, ln2_b):
    bsz, seq, dm = x.shape
    t = bsz * seq
    vec = lambda a: a.reshape(1, -1).astype(F32)
    assert DEPTH == 1
    h = x.reshape(t, dm)
    for i in range(DEPTH):
        q, k, v, ma, gb = _inproj(
            h, vec(ln_in_g), vec(ln_in_b), w_in[i].astype(BF16), vec(sg_ln_g[i]), vec(sg_ln_b[i]),
            sg_w_s[i], sg_b_s[i].T, w_branch_a[i].astype(BF16), tm=INPROJ_ROWS)
        yb = _attn(q.reshape(bsz, seq, SB_WIDTH), k.reshape(bsz, seq, SB_WIDTH),
                   v.reshape(bsz, seq, SB_WIDTH), tq=ATTN_ROWS).reshape(t, SB_WIDTH)

        rw_t = router_w[i].T
        rw_hi = rw_t.astype(BF16)
        rw_lo = (rw_t - rw_hi.astype(F32)).astype(BF16)
        h1, h1p, eidx, wsel, pos, cnt = _post(
            h, vec(ln_in_g), vec(ln_in_b), ma, gb, yb, w_branch_b[i].astype(BF16),
            w_out[i].astype(BF16), vec(ln1_g[i]), vec(ln1_b[i]), rw_hi, rw_lo,
            router_bias[i].reshape(N_EXPERTS, 1).astype(F32), tm=POST_ROWS)

        n_blocks = t * TOP_K // ROW_BLOCK + N_EXPERTS
        pad_starts, block_e, n_used, zrow = _block_layout(cnt[:, 0], n_blocks)
        dest = _slots(pad_starts, eidx, pos).T.reshape(t * TOP_K)

        xs = _dispatch(zrow, dest, h1p, n_blocks * ROW_BLOCK, tm=DISPATCH_ROWS)
        ys = _experts(block_e, n_used, xs, moe_w_gate[i], moe_w_up[i], moe_w_down[i])
        h = _combine(dest, h1, wsel.T, p[i].reshape(t, PLE_DIM), ys,
                     sh_w_gate[i].astype(BF16), sh_w_up[i].astype(BF16), sh_w_down[i].astype(BF16),
                     ple_w_gate[i].astype(BF16), ple_w_proj[i].astype(BF16),
                     vec(ln2_g[i]), vec(ln2_b[i]), tm=COMBINE_ROWS)
    return h.reshape(bsz, seq, dm)
```

```python
import functools
import math

import jax
import jax.numpy as jnp
from jax import lax
from jax.experimental import pallas as pl
from jax.experimental.pallas import tpu as pltpu

F32 = jnp.float32
BF16 = jnp.bfloat16
I32 = jnp.int32

D_MODEL = 1024
DEPTH = 1
CHUNK = 64
PLE_DIM = 256
SG_BLOCK = 128
SG_GROUPS = 4
SG_WIDTH = 512
SB_HEADS = 8
SB_HEAD_DIM = 64
SB_WIDTH = SB_HEADS * SB_HEAD_DIM
N_EXPERTS = 64
N_GROUPS = 8
GROUP_SIZE = N_EXPERTS // N_GROUPS
TOPK_GROUPS = 4
TOP_K = 8
EXPERT_FF = 256
SHARED_FF = 256
ROUTED_SCALE = 2.5
LN_EPS = 1e-5
DEEPNORM_ALPHA = (2.0 * DEPTH) ** 0.25

_OFF_U, _OFF_V, _OFF_Q, _OFF_K, _OFF_VS, _OFF_GA, _OFF_GB, _OFF_END = (
    0, 512, 1024, 1536, 2048, 2560, 3584, 4608)

LANES = 128
ROW_SUB = 4
assert 2 * ROW_SUB * LANES == D_MODEL
U32 = jnp.uint32
ATTN_LANES = 256
ATTN_HEADS = ATTN_LANES // SB_HEAD_DIM
EXP_UNDERFLOW = -110.0
ROW_BLOCK = 512
INPROJ_ROWS, POST_ROWS, ATTN_ROWS, DISPATCH_ROWS, COMBINE_ROWS = 512, 512, 256, 256, 512
VMEM_LIMIT = 56 * 1024 * 1024

_NT = (((1,), (1,)), ((), ()))


def _layer_norm(x, g, b):
    mu = jnp.mean(x, axis=-1, keepdims=True)
    xc = x - mu
    var = jnp.mean(xc * xc, axis=-1, keepdims=True)
    return xc * lax.rsqrt(var + LN_EPS) * g + b


def _unpack_rows(ref, n, base=0):
    lo, hi = [], []
    for s in range(ROW_SUB):
        w = ref[pl.ds(base * ROW_SUB + s, n, stride=ROW_SUB), :]
        lo.append(lax.bitcast_convert_type(w << 16, F32))
        hi.append(lax.bitcast_convert_type(w & jnp.uint32(0xFFFF0000), F32))
    return lo + hi


def _pack_rows(ref, val):
    n = val.shape[0]
    bits = lambda c: lax.bitcast_convert_type(
        val[:, c * LANES:(c + 1) * LANES].astype(BF16).astype(F32), jnp.uint32)
    for s in range(ROW_SUB):
        ref[pl.ds(s, n, stride=ROW_SUB), :] = bits(s + ROW_SUB) | (bits(s) >> 16)


def _const_spec(shape):
    nd = len(shape)
    return pl.BlockSpec(shape, lambda *_: (0,) * nd, pipeline_mode=pl.Buffered(1))


def _inproj_body(x_ref, g_ref, b_ref, w_ref, sgg_ref, sgb_ref, ws_ref, bs_ref, wba_ref,
                 q_ref, k_ref, v_ref, ma_ref, gb_ref, ya_scr):
    tm = x_ref.shape[0]
    hb = _layer_norm(x_ref[...], g_ref[...], b_ref[...]).astype(BF16)

    def proj(lo, hi):
        return jnp.dot(hb, w_ref[:, lo:hi], preferred_element_type=F32)

    q_ref[...] = (proj(_OFF_Q, _OFF_K) * (1.0 / math.sqrt(SB_HEAD_DIM))).astype(BF16)
    k_ref[...] = proj(_OFF_K, _OFF_VS).astype(BF16)
    v_ref[...] = proj(_OFF_VS, _OFF_GA).astype(BF16)
    gb_ref[...] = jax.nn.sigmoid(proj(_OFF_GB, _OFF_END)).astype(BF16)

    yu = jax.nn.gelu(proj(_OFF_U, _OFF_V))
    vn = _layer_norm(jax.nn.gelu(proj(_OFF_V, _OFF_Q)), sgg_ref[...], sgb_ref[...]).astype(BF16)

    pi = lax.broadcasted_iota(I32, (SG_BLOCK, SG_BLOCK), 0) // CHUNK
    pj = lax.broadcasted_iota(I32, (SG_BLOCK, SG_BLOCK), 1) // CHUNK
    chunk_causal = pj <= pi
    gch = SG_WIDTH // SG_GROUPS
    for g in range(SG_GROUPS):
        wm = jnp.where(chunk_causal, ws_ref[g], 0.0).astype(BF16)
        bias = bs_ref[:, g:g + 1]
        for r in range(tm // SG_BLOCK):
            rows = slice(r * SG_BLOCK, (r + 1) * SG_BLOCK)
            cols = slice(g * gch, (g + 1) * gch)
            s = jnp.dot(wm, vn[rows, cols], preferred_element_type=F32) + bias
            ya_scr[rows, cols] = (yu[rows, cols] * s).astype(BF16)

    ba = jnp.dot(ya_scr[...], wba_ref[...], preferred_element_type=F32)
    ma_ref[...] = (jax.nn.sigmoid(proj(_OFF_GA, _OFF_GB)) * ba).astype(BF16)


def _inproj(x2, ln_g, ln_b, w_in, sg_g, sg_b, w_s, b_sT, w_ba, *, tm):
    t = x2.shape[0]
    row = lambda w: pl.BlockSpec((tm, w), lambda i: (i, 0))
    return pl.pallas_call(
        _inproj_body,
        grid=(t // tm,),
        in_specs=[
            row(D_MODEL),
            _const_spec((1, D_MODEL)), _const_spec((1, D_MODEL)),
            _const_spec(w_in.shape),
            _const_spec((1, SG_WIDTH)), _const_spec((1, SG_WIDTH)),
            _const_spec(w_s.shape), _const_spec(b_sT.shape),
            _const_spec(w_ba.shape),
        ],
        out_specs=[row(SB_WIDTH), row(SB_WIDTH), row(SB_WIDTH), row(D_MODEL), row(D_MODEL)],
        out_shape=[
            jax.ShapeDtypeStruct((t, SB_WIDTH), BF16),
            jax.ShapeDtypeStruct((t, SB_WIDTH), BF16),
            jax.ShapeDtypeStruct((t, SB_WIDTH), BF16),
            jax.ShapeDtypeStruct((t, D_MODEL), BF16),
            jax.ShapeDtypeStruct((t, D_MODEL), BF16),
        ],
        scratch_shapes=[pltpu.VMEM((tm, SG_WIDTH), BF16)],
        compiler_params=pltpu.CompilerParams(
            dimension_semantics=("arbitrary",), vmem_limit_bytes=VMEM_LIMIT),
        name="inproj",
    )(x2, ln_g, ln_b, w_in, sg_g, sg_b, w_s, b_sT, w_ba)


def _attn_body(q_ref, k_ref, v_ref, o_ref, acc_scr, car_scr, *, tq):
    qi = pl.program_id(2)
    q2 = q_ref[0]
    lane_head = lax.broadcasted_iota(I32, (tq, ATTN_LANES), 1) // SB_HEAD_DIM
    zero = jnp.zeros_like(q2)
    in_head = [lane_head == h for h in range(ATTN_HEADS)]
    qn = [jnp.where(in_head[h], -q2, zero) for h in range(ATTN_HEADS)]
    rr = lax.broadcasted_iota(I32, (tq, tq), 0)
    cc = lax.broadcasted_iota(I32, (tq, tq), 1)
    later = (rr > cc).astype(BF16)
    causal = cc < rr
    sign = jnp.uint32(0x80000000)

    def blocks(starts, diag):
        kbs = [k_ref[0, pl.ds(pl.multiple_of(s0, tq), tq), :] for s0 in starts]
        vbs = [v_ref[0, pl.ds(pl.multiple_of(s0, tq), tq), :] for s0 in starts]
        for h in range(ATTN_HEADS):
            car = car_scr[h]
            pv = None
            for n, (kb, vb) in enumerate(zip(kbs, vbs)):
                masked = diag and n == 0
                y = lax.dot_general(qn[h], kb, _NT, preferred_element_type=F32)
                neg_abs = lax.bitcast_convert_type(
                    lax.bitcast_convert_type(y, jnp.uint32) | sign, F32)
                lm = jnp.minimum(y, 0.0) - jnp.log(1.0 + jnp.exp(neg_abs))
                lb = lm - y
                if masked:
                    lm = jnp.where(causal, lm, 0.0)
                tail = jnp.dot(lm.astype(BF16), later, preferred_element_type=F32)
                a = jnp.exp(lb + tail + car)
                if masked:
                    a = jnp.where(causal, a, 0.0)
                d = jnp.dot(a.astype(BF16), vb, preferred_element_type=F32)
                pv = d if pv is None else pv + d
                car = car + jnp.sum(lm, axis=1, keepdims=True)
            acc_scr[h] += pv
            car_scr[h] = car

    acc_scr[...] = jnp.zeros_like(acc_scr)
    car_scr[...] = jnp.zeros_like(car_scr)
    blocks([qi * tq], True)

    def live():
        top = car_scr[0]
        for h in range(1, ATTN_HEADS):
            top = jnp.maximum(top, car_scr[h])
        return (jnp.max(top) > EXP_UNDERFLOW).astype(I32)

    def earlier(state):
        j, _ = state
        blocks([(qi - 1 - j) * tq], False)
        return j + 1, live()

    lax.while_loop(lambda s: (s[0] < qi) & (s[1] > 0), earlier, (jnp.int32(0), live()))
    out = acc_scr[0]
    for h in range(1, ATTN_HEADS):
        out = jnp.where(in_head[h], acc_scr[h], out)
    o_ref[0] = out.astype(BF16)


def _attn(q3, k3, v3, *, tq):
    b, s, _ = q3.shape
    qspec = pl.BlockSpec((1, tq, ATTN_LANES), lambda bi, hp, qi: (bi, qi, hp))
    kvspec = pl.BlockSpec((1, s, ATTN_LANES), lambda bi, hp, qi: (bi, 0, hp))
    return pl.pallas_call(
        functools.partial(_attn_body, tq=tq),
        grid=(b, SB_WIDTH // ATTN_LANES, s // tq),
        in_specs=[qspec, kvspec, kvspec],
        out_specs=qspec,
        out_shape=jax.ShapeDtypeStruct((b, s, SB_WIDTH), BF16),
        scratch_shapes=[pltpu.VMEM((ATTN_HEADS, tq, ATTN_LANES), F32),
                        pltpu.VMEM((ATTN_HEADS, tq, 1), F32)],
        compiler_params=pltpu.CompilerParams(
            dimension_semantics=("arbitrary", "arbitrary", "arbitrary"),
            vmem_limit_bytes=VMEM_LIMIT),
        name="attn",
    )(q3, k3, v3)


def _post_body(x_ref, g_ref, b_ref, ma_ref, gb_ref, yb_ref, wbb_ref, wout_ref, l1g_ref, l1b_ref,
               rwh_ref, rwl_ref, rb_ref,
               h1_ref, h1p_ref, eidx_ref, wsel_ref, pos_ref, cnt_ref, cnt_scr):
    tm = x_ref.shape[0]
    step = pl.program_id(0)

    @pl.when(step == 0)
    def _():
        cnt_scr[...] = jnp.zeros_like(cnt_scr)

    h = _layer_norm(x_ref[...], g_ref[...], b_ref[...])
    bb = jnp.dot(yb_ref[...], wbb_ref[...], preferred_element_type=F32)
    merged = ma_ref[...].astype(F32) + gb_ref[...].astype(F32) * bb
    o = jnp.dot(merged.astype(BF16), wout_ref[...], preferred_element_type=F32)
    h1 = _layer_norm(DEEPNORM_ALPHA * h + o, l1g_ref[...], l1b_ref[...])
    h1_ref[...] = h1
    _pack_rows(h1p_ref, h1)

    hh = h1.astype(BF16)
    hl = (h1 - hh.astype(F32)).astype(BF16)
    logits = (lax.dot_general(rwh_ref[...], hh, _NT, preferred_element_type=F32)
              + lax.dot_general(rwh_ref[...], hl, _NT, preferred_element_type=F32)
              + lax.dot_general(rwl_ref[...], hh, _NT, preferred_element_type=F32))
    scores = jax.nn.sigmoid(logits)
    sel = scores + rb_ref[...]

    sub = lax.broadcasted_iota(I32, (GROUP_SIZE, tm), 0).astype(F32)
    neg = jnp.float32(-jnp.inf)
    selg = [sel[g * GROUP_SIZE:(g + 1) * GROUP_SIZE] for g in range(N_GROUPS)]
    scg = [scores[g * GROUP_SIZE:(g + 1) * GROUP_SIZE] for g in range(N_GROUPS)]
    eid = [sub + float(g * GROUP_SIZE) for g in range(N_GROUPS)]

    grp = []
    for g in range(N_GROUPS):
        m1 = jnp.max(selg[g], axis=0, keepdims=True)
        i1 = jnp.min(jnp.where(selg[g] == m1, sub, float(GROUP_SIZE)), axis=0, keepdims=True)
        m2 = jnp.max(jnp.where(sub == i1, neg, selg[g]), axis=0, keepdims=True)
        grp.append(m1 + m2)
    cur = []
    for g in range(N_GROUPS):
        beaten = jnp.zeros((1, tm), F32)
        for o_g in range(N_GROUPS):
            if o_g == g:
                continue
            wins = (grp[o_g] > grp[g]) if o_g > g else (grp[o_g] >= grp[g])
            beaten = beaten + jnp.where(wins, 1.0, 0.0)
        cur.append(jnp.where(beaten < float(TOPK_GROUPS), selg[g], neg))

    def all_reduce(vals, op):
        acc = vals[0]
        for v in vals[1:]:
            acc = op(acc, v)
        return acc

    idx_rows, w_rows = [], []
    member = [jnp.zeros((GROUP_SIZE, tm), F32) for _ in range(N_GROUPS)]
    for _ in range(TOP_K):
        m = jnp.max(all_reduce(cur, jnp.maximum), axis=0, keepdims=True)
        cand = [jnp.where(cur[g] == m, eid[g], float(N_EXPERTS)) for g in range(N_GROUPS)]
        idx = jnp.min(all_reduce(cand, jnp.minimum), axis=0, keepdims=True)
        hit = [eid[g] == idx for g in range(N_GROUPS)]
        w = jnp.sum(all_reduce([jnp.where(hit[g], scg[g], 0.0) for g in range(N_GROUPS)], jnp.add),
                    axis=0, keepdims=True)
        cur = [jnp.where(hit[g], neg, cur[g]) for g in range(N_GROUPS)]
        member = [jnp.where(hit[g], 1.0, member[g]) for g in range(N_GROUPS)]
        idx_rows.append(idx)
        w_rows.append(w)

    wsum = all_reduce(w_rows, jnp.add)
    for kk in range(TOP_K):
        eidx_ref[kk:kk + 1, :] = idx_rows[kk].astype(I32)
        wsel_ref[kk:kk + 1, :] = w_rows[kk] / wsum * ROUTED_SCALE

    tt = lax.broadcasted_iota(I32, (tm, tm), 0)
    uu = lax.broadcasted_iota(I32, (tm, tm), 1)
    before = (tt < uu).astype(BF16)
    memb = jnp.concatenate(member, axis=0)
    prefix = jnp.dot(memb.astype(BF16), before, preferred_element_type=F32) + cnt_scr[...]
    for kk in range(TOP_K):
        picked = [jnp.where(eid[g] == idx_rows[kk],
                            prefix[g * GROUP_SIZE:(g + 1) * GROUP_SIZE], 0.0) for g in range(N_GROUPS)]
        pos_ref[kk:kk + 1, :] = jnp.sum(all_reduce(picked, jnp.add), axis=0, keepdims=True).astype(I32)
    total = cnt_scr[...] + jnp.sum(memb, axis=1, keepdims=True)
    cnt_scr[...] = total
    cnt_ref[...] = jnp.broadcast_to(total, cnt_ref.shape).astype(I32)


def _post(x2, ln_g, ln_b, ma, gb, yb, w_bb, w_out, l1g, l1b, rw_hi, rw_lo, r_bias, *, tm):
    t = x2.shape[0]
    row = lambda w: pl.BlockSpec((tm, w), lambda i: (i, 0))
    tok = pl.BlockSpec((TOP_K, tm), lambda i: (0, i))
    return pl.pallas_call(
        _post_body,
        grid=(t // tm,),
        in_specs=[
            row(D_MODEL), _const_spec((1, D_MODEL)), _const_spec((1, D_MODEL)),
            row(D_MODEL), row(D_MODEL), row(SB_WIDTH),
            _const_spec(w_bb.shape), _const_spec(w_out.shape),
            _const_spec((1, D_MODEL)), _const_spec((1, D_MODEL)),
            _const_spec(rw_hi.shape), _const_spec(rw_lo.shape), _const_spec(r_bias.shape),
        ],
        out_specs=[row(D_MODEL), pl.BlockSpec((tm * ROW_SUB, LANES), lambda i: (i, 0)),
                   tok, tok, tok, pl.BlockSpec((N_EXPERTS, LANES), lambda i: (0, 0))],
        out_shape=[
            jax.ShapeDtypeStruct((t, D_MODEL), F32),
            jax.ShapeDtypeStruct((t * ROW_SUB, LANES), U32),
            jax.ShapeDtypeStruct((TOP_K, t), I32),
            jax.ShapeDtypeStruct((TOP_K, t), F32),
            jax.ShapeDtypeStruct((TOP_K, t), I32),
            jax.ShapeDtypeStruct((N_EXPERTS, LANES), I32),
        ],
        scratch_shapes=[pltpu.VMEM((N_EXPERTS, 1), F32)],
        compiler_params=pltpu.CompilerParams(
            dimension_semantics=("arbitrary",), vmem_limit_bytes=VMEM_LIMIT),
        name="post",
    )(x2, ln_g, ln_b, ma, gb, yb, w_bb, w_out, l1g, l1b, rw_hi, rw_lo, r_bias)


def _slots_body(ps_ref, eidx_ref, pos_ref, dest_ref):
    eidx = eidx_ref[...]
    dest = pos_ref[...]
    for e in range(N_EXPERTS):
        dest = dest + jnp.where(eidx == e, ps_ref[e], 0)
    dest_ref[...] = dest


def _slots(pad_starts, eidx, pos):
    spec = pl.BlockSpec(eidx.shape, lambda i, ps: (0, 0))
    return pl.pallas_call(
        _slots_body,
        grid_spec=pltpu.PrefetchScalarGridSpec(
            num_scalar_prefetch=1, grid=(1,), in_specs=[spec, spec], out_specs=spec),
        out_shape=jax.ShapeDtypeStruct(eidx.shape, I32),
        name="slots",
    )(pad_starts, eidx, pos)


def _tile_of(r):
    return pl.ds(pl.multiple_of(r * ROW_SUB, ROW_SUB), ROW_SUB)


def _dispatch_body(zrow_ref, dest_ref, h_ref, xs_ref, zero_scr, zsem, sem):
    tm = h_ref.shape[0] // ROW_SUB
    step = pl.program_id(0)

    def zero_copy(r):
        rows = pl.ds(pl.multiple_of(r * ROW_SUB, ROW_BLOCK * ROW_SUB), ROW_BLOCK * ROW_SUB)
        return pltpu.make_async_copy(zero_scr, xs_ref.at[rows], zsem)

    def row_copy(t, d):
        return pltpu.make_async_copy(h_ref.at[_tile_of(t)], xs_ref.at[_tile_of(d)], sem)

    @pl.when(step == 0)
    def _():
        zero_scr[...] = jnp.zeros_like(zero_scr)

        def start(e, c):
            r = zrow_ref[e]

            @pl.when(r >= 0)
            def _():
                zero_copy(jnp.maximum(r, 0)).start()
            return c

        def wait(e, c):
            @pl.when(zrow_ref[e] >= 0)
            def _():
                zero_copy(0).wait()
            return c

        lax.fori_loop(0, N_EXPERTS, start, 0)
        lax.fori_loop(0, N_EXPERTS, wait, 0)

    def issue(t, c):
        for kk in range(TOP_K):
            row_copy(t, dest_ref[t * TOP_K + kk]).start(priority=kk % 2)
        return c

    def drain(t, c):
        for kk in range(TOP_K):
            row_copy(0, 0).wait()
        return c

    lax.fori_loop(0, tm, issue, 0)
    lax.fori_loop(0, tm, drain, 0)


def _dispatch(zrow, dest_flat, h1t, n_rows, *, tm):
    t = h1t.shape[0] // ROW_SUB
    return pl.pallas_call(
        _dispatch_body,
        grid_spec=pltpu.PrefetchScalarGridSpec(
            num_scalar_prefetch=1,
            grid=(t // tm,),
            in_specs=[
                pl.BlockSpec((tm * TOP_K,), lambda i, z: (i,), memory_space=pltpu.SMEM),
                pl.BlockSpec((tm * ROW_SUB, LANES), lambda i, z: (i, 0)),
            ],
            out_specs=pl.BlockSpec(memory_space=pl.ANY),
            scratch_shapes=[pltpu.VMEM((ROW_BLOCK * ROW_SUB, LANES), U32),
                            pltpu.SemaphoreType.DMA(()), pltpu.SemaphoreType.DMA(())],
        ),
        out_shape=jax.ShapeDtypeStruct((n_rows * ROW_SUB, LANES), U32),
        compiler_params=pltpu.CompilerParams(
            dimension_semantics=("arbitrary",), vmem_limit_bytes=VMEM_LIMIT),
        name="dispatch",
    )(zrow, dest_flat, h1t)


def _experts_body(be_ref, nu_ref, xs_ref, wg_ref, wu_ref, wd_ref, ys_ref, wg_scr, wu_scr, wd_scr):
    blk = pl.program_id(0)
    prev = jnp.maximum(blk - 1, 0)

    @pl.when((blk < nu_ref[0]) & ((blk == 0) | (be_ref[blk] != be_ref[prev])))
    def _():
        wg_scr[...] = wg_ref[0].astype(BF16)
        wu_scr[...] = wu_ref[0].astype(BF16)
        wd_scr[...] = wd_ref[0].astype(BF16)

    @pl.when(blk < nu_ref[0])
    def _():
        xb = jnp.concatenate(_unpack_rows(xs_ref, ROW_BLOCK), axis=1).astype(BF16)
        gate = jnp.dot(xb, wg_scr[...], preferred_element_type=F32)
        up = jnp.dot(xb, wu_scr[...], preferred_element_type=F32)
        hid = (jax.nn.silu(gate) * up).astype(BF16)
        _pack_rows(ys_ref, jnp.dot(hid, wd_scr[...], preferred_element_type=F32))

    @pl.when(blk >= nu_ref[0])
    def _():
        ys_ref[...] = jnp.zeros_like(ys_ref)


def _experts(block_e, n_used, xs, w_gate, w_up, w_down):
    n_blocks = xs.shape[0] // (ROW_BLOCK * ROW_SUB)
    last = lambda b, be, nu: jnp.minimum(b, nu[0] - 1)
    blk = (ROW_BLOCK * ROW_SUB, LANES)
    return pl.pallas_call(
        _experts_body,
        grid_spec=pltpu.PrefetchScalarGridSpec(
            num_scalar_prefetch=2,
            grid=(n_blocks,),
            in_specs=[
                pl.BlockSpec(blk, lambda b, be, nu: (last(b, be, nu), 0)),
                pl.BlockSpec((1, D_MODEL, EXPERT_FF), lambda b, be, nu: (be[last(b, be, nu)], 0, 0)),
                pl.BlockSpec((1, D_MODEL, EXPERT_FF), lambda b, be, nu: (be[last(b, be, nu)], 0, 0)),
                pl.BlockSpec((1, EXPERT_FF, D_MODEL), lambda b, be, nu: (be[last(b, be, nu)], 0, 0)),
            ],
            out_specs=pl.BlockSpec(blk, lambda b, be, nu: (b, 0)),
            scratch_shapes=[pltpu.VMEM((D_MODEL, EXPERT_FF), BF16),
                            pltpu.VMEM((D_MODEL, EXPERT_FF), BF16),
                            pltpu.VMEM((EXPERT_FF, D_MODEL), BF16)],
        ),
        out_shape=jax.ShapeDtypeStruct(xs.shape, U32),
        compiler_params=pltpu.CompilerParams(
            dimension_semantics=("arbitrary",), vmem_limit_bytes=VMEM_LIMIT),
        name="experts",
    )(block_e, n_used, xs, w_gate, w_up, w_down)


def _combine_body(dest_ref, h1_ref, w_ref, p_ref, ys_ref, shg_ref, shu_ref, shd_ref,
                  pg_ref, pp_ref, g_ref, b_ref, o_ref, buf, sem):
    tm = h1_ref.shape[0]

    def row_copy(kk, t, d):
        return pltpu.make_async_copy(ys_ref.at[_tile_of(d)], buf.at[_tile_of(kk * tm + t)], sem)

    def issue(t, c):
        for kk in range(TOP_K):
            row_copy(kk, t, dest_ref[t * TOP_K + kk]).start(priority=kk % 2)
        return c

    def drain(t, c):
        for kk in range(TOP_K):
            row_copy(kk, 0, 0).wait()
        return c

    lax.fori_loop(0, tm, issue, 0)

    h1 = h1_ref[...]
    hb = h1.astype(BF16)
    hid = (jax.nn.silu(jnp.dot(hb, shg_ref[...], preferred_element_type=F32))
           * jnp.dot(hb, shu_ref[...], preferred_element_type=F32)).astype(BF16)
    r = DEEPNORM_ALPHA * h1 + jnp.dot(hid, shd_ref[...], preferred_element_type=F32)
    emb = jnp.dot(p_ref[...].astype(BF16), pp_ref[...], preferred_element_type=F32)

    lax.fori_loop(0, tm, drain, 0)
    w = w_ref[...]
    cols = [r[:, c * LANES:(c + 1) * LANES] for c in range(2 * ROW_SUB)]
    for kk in range(TOP_K):
        wk = jnp.broadcast_to(w[:, kk:kk + 1], (tm, LANES))
        for c, chunk in enumerate(_unpack_rows(buf, tm, base=kk * tm)):
            cols[c] = cols[c] + wk * chunk
    r = jnp.concatenate(cols, axis=1)
    gate = jax.nn.sigmoid(jnp.dot(r.astype(BF16), pg_ref[...], preferred_element_type=F32))
    o_ref[...] = _layer_norm(r + gate * emb, g_ref[...], b_ref[...])


def _combine(dest_flat, h1, wtok, p2, ys, sh_g, sh_u, sh_d, ple_g, ple_p, l2g, l2b, *, tm):
    t = h1.shape[0]
    row = lambda w: pl.BlockSpec((tm, w), lambda i: (i, 0))
    return pl.pallas_call(
        _combine_body,
        grid=(t // tm,),
        in_specs=[
            pl.BlockSpec((tm * TOP_K,), lambda i: (i,), memory_space=pltpu.SMEM),
            row(D_MODEL), row(TOP_K), row(PLE_DIM),
            pl.BlockSpec(memory_space=pl.ANY),
            _const_spec(sh_g.shape), _const_spec(sh_u.shape), _const_spec(sh_d.shape),
            _const_spec(ple_g.shape), _const_spec(ple_p.shape),
            _const_spec((1, D_MODEL)), _const_spec((1, D_MODEL)),
        ],
        out_specs=row(D_MODEL),
        out_shape=jax.ShapeDtypeStruct((t, D_MODEL), F32),
        scratch_shapes=[pltpu.VMEM((TOP_K * tm * ROW_SUB, LANES), U32),
                        pltpu.SemaphoreType.DMA(())],
        compiler_params=pltpu.CompilerParams(
            dimension_semantics=("arbitrary",), vmem_limit_bytes=VMEM_LIMIT),
        name="combine",
    )(dest_flat, h1, wtok, p2, ys, sh_g, sh_u, sh_d, ple_g, ple_p, l2g, l2b)


def _block_layout(counts, n_blocks):
    padded = (counts + ROW_BLOCK - 1) // ROW_BLOCK * ROW_BLOCK
    pad_ends = jnp.cumsum(padded)
    pad_starts = pad_ends - padded
    block_row0 = jnp.arange(n_blocks, dtype=I32) * ROW_BLOCK
    block_e = jnp.minimum(jnp.sum(pad_ends[None, :] <= block_row0[:, None], axis=1),
                          N_EXPERTS - 1).astype(I32)
    n_used = (pad_ends[-1:] // ROW_BLOCK).astype(I32)
    has_pad = (counts % ROW_BLOCK) != 0
    zrow = jnp.where(has_pad, pad_ends - ROW_BLOCK, -1).astype(I32)
    return pad_starts.astype(I32), block_e, n_used, zrow


def kernel(x, p, ln_in_g, ln_in_b, w_in, sg_ln_g, sg_ln_b, sg_w_s, sg_b_s, w_branch_a, w_branch_b,
           w_out, ln1_g, ln1_b, router_w, router_bias, moe_w_gate, moe_w_up, moe_w_down,
           sh_w_gate, sh_w_up, sh_w_down, ple_w_proj, ple_w_gate, ln2_g, ln2_b):
    bsz, seq, dm = x.shape
    t = bsz * seq
    vec = lambda a: a.reshape(1, -1).astype(F32)
    assert DEPTH == 1
    h = x.reshape(t, dm)
    for i in range(DEPTH):
        q, k, v, ma, gb = _inproj(
            h, vec(ln_in_g), vec(ln_in_b), w_in[i].astype(BF16), vec(sg_ln_g[i]), vec(sg_ln_b[i]),
            sg_w_s[i], sg_b_s[i].T, w_branch_a[i].astype(BF16), tm=INPROJ_ROWS)
        yb = _attn(q.reshape(bsz, seq, SB_WIDTH), k.reshape(bsz, seq, SB_WIDTH),
                   v.reshape(bsz, seq, SB_WIDTH), tq=ATTN_ROWS).reshape(t, SB_WIDTH)

        rw_t = router_w[i].T
        rw_hi = rw_t.astype(BF16)
        rw_lo = (rw_t - rw_hi.astype(F32)).astype(BF16)
        h1, h1p, eidx, wsel, pos, cnt = _post(
            h, vec(ln_in_g), vec(ln_in_b), ma, gb, yb, w_branch_b[i].astype(BF16),
            w_out[i].astype(BF16), vec(ln1_g[i]), vec(ln1_b[i]), rw_hi, rw_lo,
            router_bias[i].reshape(N_EXPERTS, 1).astype(F32), tm=POST_ROWS)

        n_blocks = t * TOP_K // ROW_BLOCK + N_EXPERTS
        pad_starts, block_e, n_used, zrow = _block_layout(cnt[:, 0], n_blocks)
        dest = _slots(pad_starts, eidx, pos).T.reshape(t * TOP_K)

        xs = _dispatch(zrow, dest, h1p, n_blocks * ROW_BLOCK, tm=DISPATCH_ROWS)
        ys = _experts(block_e, n_used, xs, moe_w_gate[i], moe_w_up[i], moe_w_down[i])
        h = _combine(dest, h1, wsel.T, p[i].reshape(t, PLE_DIM), ys,
                     sh_w_gate[i].astype(BF16), sh_w_up[i].astype(BF16), sh_w_down[i].astype(BF16),
                     ple_w_gate[i].astype(BF16), ple_w_proj[i].astype(BF16),
                     vec(ln2_g[i]), vec(ln2_b[i]), tm=COMBINE_ROWS)
    return h.reshape(bsz, seq, dm)
```

```python
import functools
import math

import jax
import jax.numpy as jnp
from jax import lax
from jax.experimental import pallas as pl
from jax.experimental.pallas import tpu as pltpu

F32 = jnp.float32
BF16 = jnp.bfloat16
I32 = jnp.int32

D_MODEL = 1024
DEPTH = 1
CHUNK = 64
PLE_DIM = 256
SG_BLOCK = 128
SG_GROUPS = 4
SG_WIDTH = 512
SB_HEADS = 8
SB_HEAD_DIM = 64
SB_WIDTH = SB_HEADS * SB_HEAD_DIM
N_EXPERTS = 64
N_GROUPS = 8
GROUP_SIZE = N_EXPERTS // N_GROUPS
TOPK_GROUPS = 4
TOP_K = 8
EXPERT_FF = 256
SHARED_FF = 256
ROUTED_SCALE = 2.5
LN_EPS = 1e-5
DEEPNORM_ALPHA = (2.0 * DEPTH) ** 0.25

_OFF_U, _OFF_V, _OFF_Q, _OFF_K, _OFF_VS, _OFF_GA, _OFF_GB, _OFF_END = (
    0, 512, 1024, 1536, 2048, 2560, 3584, 4608)

LANES = 128
ROW_SUB = 4
assert 2 * ROW_SUB * LANES == D_MODEL
U32 = jnp.uint32
ATTN_LANES = 256
ATTN_HEADS = ATTN_LANES // SB_HEAD_DIM
EXP_UNDERFLOW = -110.0
ROW_BLOCK = 512
INPROJ_ROWS, POST_ROWS, ATTN_ROWS, DISPATCH_ROWS, COMBINE_ROWS = 1024, 1024, 256, 256, 512
VMEM_LIMIT = 56 * 1024 * 1024

_NT = (((1,), (1,)), ((), ()))


def _layer_norm(x, g, b):
    mu = jnp.mean(x, axis=-1, keepdims=True)
    xc = x - mu
    var = jnp.mean(xc * xc, axis=-1, keepdims=True)
    return xc * lax.rsqrt(var + LN_EPS) * g + b


def _unpack_rows(ref, n, base=0):
    lo, hi = [], []
    for s in range(ROW_SUB):
        w = ref[pl.ds(base * ROW_SUB + s, n, stride=ROW_SUB), :]
        lo.append(lax.bitcast_convert_type(w << 16, F32))
        hi.append(lax.bitcast_convert_type(w & jnp.uint32(0xFFFF0000), F32))
    return lo + hi


def _pack_rows(ref, val):
    n = val.shape[0]
    bits = lambda c: lax.bitcast_convert_type(
        val[:, c * LANES:(c + 1) * LANES].astype(BF16).astype(F32), jnp.uint32)
    for s in range(ROW_SUB):
        ref[pl.ds(s, n, stride=ROW_SUB), :] = bits(s + ROW_SUB) | (bits(s) >> 16)


def _const_spec(shape):
    nd = len(shape)
    return pl.BlockSpec(shape, lambda *_: (0,) * nd, pipeline_mode=pl.Buffered(1))


def _inproj_body(x_ref, g_ref, b_ref, w_ref, sgg_ref, sgb_ref, ws_ref, bs_ref, wba_ref,
                 q_ref, k_ref, v_ref, ma_ref, gb_ref, ya_scr):
    tm = x_ref.shape[0]
    hb = _layer_norm(x_ref[...], g_ref[...], b_ref[...]).astype(BF16)

    def proj(lo, hi):
        return jnp.dot(hb, w_ref[:, lo:hi], preferred_element_type=F32)

    q_ref[...] = (proj(_OFF_Q, _OFF_K) * (1.0 / math.sqrt(SB_HEAD_DIM))).astype(BF16)
    k_ref[...] = proj(_OFF_K, _OFF_VS).astype(BF16)
    v_ref[...] = proj(_OFF_VS, _OFF_GA).astype(BF16)
    gb_ref[...] = jax.nn.sigmoid(proj(_OFF_GB, _OFF_END)).astype(BF16)

    yu = jax.nn.gelu(proj(_OFF_U, _OFF_V))
    vn = _layer_norm(jax.nn.gelu(proj(_OFF_V, _OFF_Q)), sgg_ref[...], sgb_ref[...]).astype(BF16)

    pi = lax.broadcasted_iota(I32, (SG_BLOCK, SG_BLOCK), 0) // CHUNK
    pj = lax.broadcasted_iota(I32, (SG_BLOCK, SG_BLOCK), 1) // CHUNK
    chunk_causal = pj <= pi
    gch = SG_WIDTH // SG_GROUPS
    for g in range(SG_GROUPS):
        wm = jnp.where(chunk_causal, ws_ref[g], 0.0).astype(BF16)
        bias = bs_ref[:, g:g + 1]
        for r in range(tm // SG_BLOCK):
            rows = slice(r * SG_BLOCK, (r + 1) * SG_BLOCK)
            cols = slice(g * gch, (g + 1) * gch)
            s = jnp.dot(wm, vn[rows, cols], preferred_element_type=F32) + bias
            ya_scr[rows, cols] = (yu[rows, cols] * s).astype(BF16)

    ba = jnp.dot(ya_scr[...], wba_ref[...], preferred_element_type=F32)
    ma_ref[...] = (jax.nn.sigmoid(proj(_OFF_GA, _OFF_GB)) * ba).astype(BF16)


def _inproj(x2, ln_g, ln_b, w_in, sg_g, sg_b, w_s, b_sT, w_ba, *, tm):
    t = x2.shape[0]
    row = lambda w: pl.BlockSpec((tm, w), lambda i: (i, 0))
    return pl.pallas_call(
        _inproj_body,
        grid=(t // tm,),
        in_specs=[
            row(D_MODEL),
            _const_spec((1, D_MODEL)), _const_spec((1, D_MODEL)),
            _const_spec(w_in.shape),
            _const_spec((1, SG_WIDTH)), _const_spec((1, SG_WIDTH)),
            _const_spec(w_s.shape), _const_spec(b_sT.shape),
            _const_spec(w_ba.shape),
        ],
        out_specs=[row(SB_WIDTH), row(SB_WIDTH), row(SB_WIDTH), row(D_MODEL), row(D_MODEL)],
        out_shape=[
            jax.ShapeDtypeStruct((t, SB_WIDTH), BF16),
            jax.ShapeDtypeStruct((t, SB_WIDTH), BF16),
            jax.ShapeDtypeStruct((t, SB_WIDTH), BF16),
            jax.ShapeDtypeStruct((t, D_MODEL), BF16),
            jax.ShapeDtypeStruct((t, D_MODEL), BF16),
        ],
        scratch_shapes=[pltpu.VMEM((tm, SG_WIDTH), BF16)],
        compiler_params=pltpu.CompilerParams(
            dimension_semantics=("arbitrary",), vmem_limit_bytes=VMEM_LIMIT),
        name="inproj",
    )(x2, ln_g, ln_b, w_in, sg_g, sg_b, w_s, b_sT, w_ba)


def _attn_body(q_ref, k_ref, v_ref, o_ref, acc_scr, car_scr, *, tq):
    qi = pl.program_id(2)
    q2 = q_ref[0]
    lane_head = lax.broadcasted_iota(I32, (tq, ATTN_LANES), 1) // SB_HEAD_DIM
    zero = jnp.zeros_like(q2)
    in_head = [lane_head == h for h in range(ATTN_HEADS)]
    qn = [jnp.where(in_head[h], -q2, zero) for h in range(ATTN_HEADS)]
    rr = lax.broadcasted_iota(I32, (tq, tq), 0)
    cc = lax.broadcasted_iota(I32, (tq, tq), 1)
    later = (rr > cc).astype(BF16)
    causal = cc < rr
    sign = jnp.uint32(0x80000000)

    def blocks(starts, diag):
        kbs = [k_ref[0, pl.ds(pl.multiple_of(s0, tq), tq), :] for s0 in starts]
        vbs = [v_ref[0, pl.ds(pl.multiple_of(s0, tq), tq), :] for s0 in starts]
        for h in range(ATTN_HEADS):
            car = car_scr[h]
            pv = None
            for n, (kb, vb) in enumerate(zip(kbs, vbs)):
                masked = diag and n == 0
                y = lax.dot_general(qn[h], kb, _NT, preferred_element_type=F32)
                neg_abs = lax.bitcast_convert_type(
                    lax.bitcast_convert_type(y, jnp.uint32) | sign, F32)
                lm = jnp.minimum(y, 0.0) - jnp.log(1.0 + jnp.exp(neg_abs))
                lb = lm - y
                if masked:
                    lm = jnp.where(causal, lm, 0.0)
                tail = jnp.dot(lm.astype(BF16), later, preferred_element_type=F32)
                a = jnp.exp(lb + tail + car)
                if masked:
                    a = jnp.where(causal, a, 0.0)
                d = jnp.dot(a.astype(BF16), vb, preferred_element_type=F32)
                pv = d if pv is None else pv + d
                car = car + jnp.sum(lm, axis=1, keepdims=True)
            acc_scr[h] += pv
            car_scr[h] = car

    acc_scr[...] = jnp.zeros_like(acc_scr)
    car_scr[...] = jnp.zeros_like(car_scr)
    blocks([qi * tq], True)

    def live():
        top = car_scr[0]
        for h in range(1, ATTN_HEADS):
            top = jnp.maximum(top, car_scr[h])
        return (jnp.max(top) > EXP_UNDERFLOW).astype(I32)

    def earlier(state):
        j, _ = state
        blocks([(qi - 1 - j) * tq], False)
        return j + 1, live()

    lax.while_loop(lambda s: (s[0] < qi) & (s[1] > 0), earlier, (jnp.int32(0), live()))
    out = acc_scr[0]
    for h in range(1, ATTN_HEADS):
        out = jnp.where(in_head[h], acc_scr[h], out)
    o_ref[0] = out.astype(BF16)


def _attn(q3, k3, v3, *, tq):
    b, s, _ = q3.shape
    qspec = pl.BlockSpec((1, tq, ATTN_LANES), lambda bi, hp, qi: (bi, qi, hp))
    kvspec = pl.BlockSpec((1, s, ATTN_LANES), lambda bi, hp, qi: (bi, 0, hp))
    return pl.pallas_call(
        functools.partial(_attn_body, tq=tq),
        grid=(b, SB_WIDTH // ATTN_LANES, s // tq),
        in_specs=[qspec, kvspec, kvspec],
        out_specs=qspec,
        out_shape=jax.ShapeDtypeStruct((b, s, SB_WIDTH), BF16),
        scratch_shapes=[pltpu.VMEM((ATTN_HEADS, tq, ATTN_LANES), F32),
                        pltpu.VMEM((ATTN_HEADS, tq, 1), F32)],
        compiler_params=pltpu.CompilerParams(
            dimension_semantics=("arbitrary", "arbitrary", "arbitrary"),
            vmem_limit_bytes=VMEM_LIMIT),
        name="attn",
    )(q3, k3, v3)


def _post_body(x_ref, g_ref, b_ref, ma_ref, gb_ref, yb_ref, wbb_ref, wout_ref, l1g_ref, l1b_ref,
               rwh_ref, rwl_ref, rb_ref,
               h1_ref, h1p_ref, eidx_ref, wsel_ref, pos_ref, cnt_ref, cnt_scr):
    tm = x_ref.shape[0]
    step = pl.program_id(0)

    @pl.when(step == 0)
    def _():
        cnt_scr[...] = jnp.zeros_like(cnt_scr)

    h = _layer_norm(x_ref[...], g_ref[...], b_ref[...])
    bb = jnp.dot(yb_ref[...], wbb_ref[...], preferred_element_type=F32)
    merged = ma_ref[...].astype(F32) + gb_ref[...].astype(F32) * bb
    o = jnp.dot(merged.astype(BF16), wout_ref[...], preferred_element_type=F32)
    h1 = _layer_norm(DEEPNORM_ALPHA * h + o, l1g_ref[...], l1b_ref[...])
    h1_ref[...] = h1
    _pack_rows(h1p_ref, h1)

    hh = h1.astype(BF16)
    hl = (h1 - hh.astype(F32)).astype(BF16)
    logits = (lax.dot_general(rwh_ref[...], hh, _NT, preferred_element_type=F32)
              + lax.dot_general(rwh_ref[...], hl, _NT, preferred_element_type=F32)
              + lax.dot_general(rwl_ref[...], hh, _NT, preferred_element_type=F32))
    scores = jax.nn.sigmoid(logits)
    sel = scores + rb_ref[...]

    sub = lax.broadcasted_iota(I32, (GROUP_SIZE, tm), 0).astype(F32)
    neg = jnp.float32(-jnp.inf)
    selg = [sel[g * GROUP_SIZE:(g + 1) * GROUP_SIZE] for g in range(N_GROUPS)]
    scg = [scores[g * GROUP_SIZE:(g + 1) * GROUP_SIZE] for g in range(N_GROUPS)]
    eid = [sub + float(g * GROUP_SIZE) for g in range(N_GROUPS)]

    grp = []
    for g in range(N_GROUPS):
        m1 = jnp.max(selg[g], axis=0, keepdims=True)
        i1 = jnp.min(jnp.where(selg[g] == m1, sub, float(GROUP_SIZE)), axis=0, keepdims=True)
        m2 = jnp.max(jnp.where(sub == i1, neg, selg[g]), axis=0, keepdims=True)
        grp.append(m1 + m2)
    cur = []
    for g in range(N_GROUPS):
        beaten = jnp.zeros((1, tm), F32)
        for o_g in range(N_GROUPS):
            if o_g == g:
                continue
            wins = (grp[o_g] > grp[g]) if o_g > g else (grp[o_g] >= grp[g])
            beaten = beaten + jnp.where(wins, 1.0, 0.0)
        cur.append(jnp.where(beaten < float(TOPK_GROUPS), selg[g], neg))

    def all_reduce(vals, op):
        acc = vals[0]
        for v in vals[1:]:
            acc = op(acc, v)
        return acc

    idx_rows, w_rows = [], []
    member = [jnp.zeros((GROUP_SIZE, tm), F32) for _ in range(N_GROUPS)]
    for _ in range(TOP_K):
        m = jnp.max(all_reduce(cur, jnp.maximum), axis=0, keepdims=True)
        cand = [jnp.where(cur[g] == m, eid[g], float(N_EXPERTS)) for g in range(N_GROUPS)]
        idx = jnp.min(all_reduce(cand, jnp.minimum), axis=0, keepdims=True)
        hit = [eid[g] == idx for g in range(N_GROUPS)]
        w = jnp.sum(all_reduce([jnp.where(hit[g], scg[g], 0.0) for g in range(N_GROUPS)], jnp.add),
                    axis=0, keepdims=True)
        cur = [jnp.where(hit[g], neg, cur[g]) for g in range(N_GROUPS)]
        member = [jnp.where(hit[g], 1.0, member[g]) for g in range(N_GROUPS)]
        idx_rows.append(idx)
        w_rows.append(w)

    wsum = all_reduce(w_rows, jnp.add)
    for kk in range(TOP_K):
        eidx_ref[kk:kk + 1, :] = idx_rows[kk].astype(I32)
        wsel_ref[kk:kk + 1, :] = w_rows[kk] / wsum * ROUTED_SCALE

    tt = lax.broadcasted_iota(I32, (tm, tm), 0)
    uu = lax.broadcasted_iota(I32, (tm, tm), 1)
    before = (tt < uu).astype(BF16)
    memb = jnp.concatenate(member, axis=0)
    prefix = jnp.dot(memb.astype(BF16), before, preferred_element_type=F32) + cnt_scr[...]
    for kk in range(TOP_K):
        picked = [jnp.where(eid[g] == idx_rows[kk],
                            prefix[g * GROUP_SIZE:(g + 1) * GROUP_SIZE], 0.0) for g in range(N_GROUPS)]
        pos_ref[kk:kk + 1, :] = jnp.sum(all_reduce(picked, jnp.add), axis=0, keepdims=True).astype(I32)
    total = cnt_scr[...] + jnp.sum(memb, axis=1, keepdims=True)
    cnt_scr[...] = total
    cnt_ref[...] = jnp.broadcast_to(total, cnt_ref.shape).astype(I32)


def _post(x2, ln_g, ln_b, ma, gb, yb, w_bb, w_out, l1g, l1b, rw_hi, rw_lo, r_bias, *, tm):
    t = x2.shape[0]
    row = lambda w: pl.BlockSpec((tm, w), lambda i: (i, 0))
    tok = pl.BlockSpec((TOP_K, tm), lambda i: (0, i))
    return pl.pallas_call(
        _post_body,
        grid=(t // tm,),
        in_specs=[
            row(D_MODEL), _const_spec((1, D_MODEL)), _const_spec((1, D_MODEL)),
            row(D_MODEL), row(D_MODEL), row(SB_WIDTH),
            _const_spec(w_bb.shape), _const_spec(w_out.shape),
            _const_spec((1, D_MODEL)), _const_spec((1, D_MODEL)),
            _const_spec(rw_hi.shape), _const_spec(rw_lo.shape), _const_spec(r_bias.shape),
        ],
        out_specs=[row(D_MODEL), pl.BlockSpec((tm * ROW_SUB, LANES), lambda i: (i, 0)),
                   tok, tok, tok, pl.BlockSpec((N_EXPERTS, LANES), lambda i: (0, 0))],
        out_shape=[
            jax.ShapeDtypeStruct((t, D_MODEL), F32),
            jax.ShapeDtypeStruct((t * ROW_SUB, LANES), U32),
            jax.ShapeDtypeStruct((TOP_K, t), I32),
            jax.ShapeDtypeStruct((TOP_K, t), F32),
            jax.ShapeDtypeStruct((TOP_K, t), I32),
            jax.ShapeDtypeStruct((N_EXPERTS, LANES), I32),
        ],
        scratch_shapes=[pltpu.VMEM((N_EXPERTS, 1), F32)],
        compiler_params=pltpu.CompilerParams(
            dimension_semantics=("arbitrary",), vmem_limit_bytes=VMEM_LIMIT),
        name="post",
    )(x2, ln_g, ln_b, ma, gb, yb, w_bb, w_out, l1g, l1b, rw_hi, rw_lo, r_bias)


def _slots_body(ps_ref, eidx_ref, pos_ref, dest_ref):
    eidx = eidx_ref[...]
    dest = pos_ref[...]
    for e in range(N_EXPERTS):
        dest = dest + jnp.where(eidx == e, ps_ref[e], 0)
    dest_ref[...] = dest


def _slots(pad_starts, eidx, pos):
    spec = pl.BlockSpec(eidx.shape, lambda i, ps: (0, 0))
    return pl.pallas_call(
        _slots_body,
        grid_spec=pltpu.PrefetchScalarGridSpec(
            num_scalar_prefetch=1, grid=(1,), in_specs=[spec, spec], out_specs=spec),
        out_shape=jax.ShapeDtypeStruct(eidx.shape, I32),
        name="slots",
    )(pad_starts, eidx, pos)


def _tile_of(r):
    return pl.ds(pl.multiple_of(r * ROW_SUB, ROW_SUB), ROW_SUB)


def _dispatch_body(zrow_ref, dest_ref, h_ref, xs_ref, zero_scr, zsem, sem):
    tm = h_ref.shape[0] // ROW_SUB
    step = pl.program_id(0)

    def zero_copy(r):
        rows = pl.ds(pl.multiple_of(r * ROW_SUB, ROW_BLOCK * ROW_SUB), ROW_BLOCK * ROW_SUB)
        return pltpu.make_async_copy(zero_scr, xs_ref.at[rows], zsem)

    def row_copy(t, d):
        return pltpu.make_async_copy(h_ref.at[_tile_of(t)], xs_ref.at[_tile_of(d)], sem)

    @pl.when(step == 0)
    def _():
        zero_scr[...] = jnp.zeros_like(zero_scr)

        def start(e, c):
            r = zrow_ref[e]

            @pl.when(r >= 0)
            def _():
                zero_copy(jnp.maximum(r, 0)).start()
            return c

        def wait(e, c):
            @pl.when(zrow_ref[e] >= 0)
            def _():
                zero_copy(0).wait()
            return c

        lax.fori_loop(0, N_EXPERTS, start, 0)
        lax.fori_loop(0, N_EXPERTS, wait, 0)

    def issue(t, c):
        for kk in range(TOP_K):
            row_copy(t, dest_ref[t * TOP_K + kk]).start(priority=kk % 2)
        return c

    def drain(t, c):
        for kk in range(TOP_K):
            row_copy(0, 0).wait()
        return c

    lax.fori_loop(0, tm, issue, 0)
    lax.fori_loop(0, tm, drain, 0)


def _dispatch(zrow, dest_flat, h1t, n_rows, *, tm):
    t = h1t.shape[0] // ROW_SUB
    return pl.pallas_call(
        _dispatch_body,
        grid_spec=pltpu.PrefetchScalarGridSpec(
            num_scalar_prefetch=1,
            grid=(t // tm,),
            in_specs=[
                pl.BlockSpec((tm * TOP_K,), lambda i, z: (i,), memory_space=pltpu.SMEM),
                pl.BlockSpec((tm * ROW_SUB, LANES), lambda i, z: (i, 0)),
            ],
            out_specs=pl.BlockSpec(memory_space=pl.ANY),
            scratch_shapes=[pltpu.VMEM((ROW_BLOCK * ROW_SUB, LANES), U32),
                            pltpu.SemaphoreType.DMA(()), pltpu.SemaphoreType.DMA(())],
        ),
        out_shape=jax.ShapeDtypeStruct((n_rows * ROW_SUB, LANES), U32),
        compiler_params=pltpu.CompilerParams(
            dimension_semantics=("arbitrary",), vmem_limit_bytes=VMEM_LIMIT),
        name="dispatch",
    )(zrow, dest_flat, h1t)


def _experts_body(be_ref, nu_ref, xs_ref, wg_ref, wu_ref, wd_ref, ys_ref, wg_scr, wu_scr, wd_scr):
    blk = pl.program_id(0)
    prev = jnp.maximum(blk - 1, 0)

    @pl.when((blk < nu_ref[0]) & ((blk == 0) | (be_ref[blk] != be_ref[prev])))
    def _():
        wg_scr[...] = wg_ref[0].astype(BF16)
        wu_scr[...] = wu_ref[0].astype(BF16)
        wd_scr[...] = wd_ref[0].astype(BF16)

    @pl.when(blk < nu_ref[0])
    def _():
        xb = jnp.concatenate(_unpack_rows(xs_ref, ROW_BLOCK), axis=1).astype(BF16)
        gate = jnp.dot(xb, wg_scr[...], preferred_element_type=F32)
        up = jnp.dot(xb, wu_scr[...], preferred_element_type=F32)
        hid = (jax.nn.silu(gate) * up).astype(BF16)
        _pack_rows(ys_ref, jnp.dot(hid, wd_scr[...], preferred_element_type=F32))

    @pl.when(blk >= nu_ref[0])
    def _():
        ys_ref[...] = jnp.zeros_like(ys_ref)


def _experts(block_e, n_used, xs, w_gate, w_up, w_down):
    n_blocks = xs.shape[0] // (ROW_BLOCK * ROW_SUB)
    last = lambda b, be, nu: jnp.minimum(b, nu[0] - 1)
    blk = (ROW_BLOCK * ROW_SUB, LANES)
    return pl.pallas_call(
        _experts_body,
        grid_spec=pltpu.PrefetchScalarGridSpec(
            num_scalar_prefetch=2,
            grid=(n_blocks,),
            in_specs=[
                pl.BlockSpec(blk, lambda b, be, nu: (last(b, be, nu), 0)),
                pl.BlockSpec((1, D_MODEL, EXPERT_FF), lambda b, be, nu: (be[last(b, be, nu)], 0, 0)),
                pl.BlockSpec((1, D_MODEL, EXPERT_FF), lambda b, be, nu: (be[last(b, be, nu)], 0, 0)),
                pl.BlockSpec((1, EXPERT_FF, D_MODEL), lambda b, be, nu: (be[last(b, be, nu)], 0, 0)),
            ],
            out_specs=pl.BlockSpec(blk, lambda b, be, nu: (b, 0)),
            scratch_shapes=[pltpu.VMEM((D_MODEL, EXPERT_FF), BF16),
                            pltpu.VMEM((D_MODEL, EXPERT_FF), BF16),
                            pltpu.VMEM((EXPERT_FF, D_MODEL), BF16)],
        ),
        out_shape=jax.ShapeDtypeStruct(xs.shape, U32),
        compiler_params=pltpu.CompilerParams(
            dimension_semantics=("arbitrary",), vmem_limit_bytes=VMEM_LIMIT),
        name="experts",
    )(block_e, n_used, xs, w_gate, w_up, w_down)


def _combine_body(dest_ref, h1_ref, w_ref, p_ref, ys_ref, shg_ref, shu_ref, shd_ref,
                  pg_ref, pp_ref, g_ref, b_ref, o_ref, buf, sem):
    tm = h1_ref.shape[0]

    def row_copy(kk, t, d):
        return pltpu.make_async_copy(ys_ref.at[_tile_of(d)], buf.at[_tile_of(kk * tm + t)], sem)

    def issue(t, c):
        for kk in range(TOP_K):
            row_copy(kk, t, dest_ref[t * TOP_K + kk]).start(priority=kk % 2)
        return c

    def drain(t, c):
        for kk in range(TOP_K):
            row_copy(kk, 0, 0).wait()
        return c

    lax.fori_loop(0, tm, issue, 0)

    h1 = h1_ref[...]
    hb = h1.astype(BF16)
    hid = (jax.nn.silu(jnp.dot(hb, shg_ref[...], preferred_element_type=F32))
           * jnp.dot(hb, shu_ref[...], preferred_element_type=F32)).astype(BF16)
    r = DEEPNORM_ALPHA * h1 + jnp.dot(hid, shd_ref[...], preferred_element_type=F32)
    emb = jnp.dot(p_ref[...].astype(BF16), pp_ref[...], preferred_element_type=F32)

    lax.fori_loop(0, tm, drain, 0)
    w = w_ref[...]
    cols = [r[:, c * LANES:(c + 1) * LANES] for c in range(2 * ROW_SUB)]
    for kk in range(TOP_K):
        wk = jnp.broadcast_to(w[:, kk:kk + 1], (tm, LANES))
        for c, chunk in enumerate(_unpack_rows(buf, tm, base=kk * tm)):
            cols[c] = cols[c] + wk * chunk
    r = jnp.concatenate(cols, axis=1)
    gate = jax.nn.sigmoid(jnp.dot(r.astype(BF16), pg_ref[...], preferred_element_type=F32))
    o_ref[...] = _layer_norm(r + gate * emb, g_ref[...], b_ref[...])


def _combine(dest_flat, h1, wtok, p2, ys, sh_g, sh_u, sh_d, ple_g, ple_p, l2g, l2b, *, tm):
    t = h1.shape[0]
    row = lambda w: pl.BlockSpec((tm, w), lambda i: (i, 0))
    return pl.pallas_call(
        _combine_body,
        grid=(t // tm,),
        in_specs=[
            pl.BlockSpec((tm * TOP_K,), lambda i: (i,), memory_space=pltpu.SMEM),
            row(D_MODEL), row(TOP_K), row(PLE_DIM),
            pl.BlockSpec(memory_space=pl.ANY),
            _const_spec(sh_g.shape), _const_spec(sh_u.shape), _const_spec(sh_d.shape),
            _const_spec(ple_g.shape), _const_spec(ple_p.shape),
            _const_spec((1, D_MODEL)), _const_spec((1, D_MODEL)),
        ],
        out_specs=row(D_MODEL),
        out_shape=jax.ShapeDtypeStruct((t, D_MODEL), F32),
        scratch_shapes=[pltpu.VMEM((TOP_K * tm * ROW_SUB, LANES), U32),
                        pltpu.SemaphoreType.DMA(())],
        compiler_params=pltpu.CompilerParams(
            dimension_semantics=("arbitrary",), vmem_limit_bytes=VMEM_LIMIT),
        name="combine",
    )(dest_flat, h1, wtok, p2, ys, sh_g, sh_u, sh_d, ple_g, ple_p, l2g, l2b)


def _block_layout(counts, n_blocks):
    padded = (counts + ROW_BLOCK - 1) // ROW_BLOCK * ROW_BLOCK
    pad_ends = jnp.cumsum(padded)
    pad_starts = pad_ends - padded
    block_row0 = jnp.arange(n_blocks, dtype=I32) * ROW_BLOCK
    block_e = jnp.minimum(jnp.sum(pad_ends[None, :] <= block_row0[:, None], axis=1),
                          N_EXPERTS - 1).astype(I32)
    n_used = (pad_ends[-1:] // ROW_BLOCK).astype(I32)
    has_pad = (counts % ROW_BLOCK) != 0
    zrow = jnp.where(has_pad, pad_ends - ROW_BLOCK, -1).astype(I32)
    return pad_starts.astype(I32), block_e, n_used, zrow


def kernel(x, p, ln_in_g, ln_in_b, w_in, sg_ln_g, sg_ln_b, sg_w_s, sg_b_s, w_branch_a, w_branch_b,
           w_out, ln1_g, ln1_b, router_w, router_bias, moe_w_gate, moe_w_up, moe_w_down,
           sh_w_gate, sh_w_up, sh_w_down, ple_w_proj, ple_w_gate, ln2_g, ln2_b):
    bsz, seq, dm = x.shape
    t = bsz * seq
    vec = lambda a: a.reshape(1, -1).astype(F32)
    assert DEPTH == 1
    h = x.reshape(t, dm)
    for i in range(DEPTH):
        q, k, v, ma, gb = _inproj(
            h, vec(ln_in_g), vec(ln_in_b), w_in[i].astype(BF16), vec(sg_ln_g[i]), vec(sg_ln_b[i]),
            sg_w_s[i], sg_b_s[i].T, w_branch_a[i].astype(BF16), tm=INPROJ_ROWS)
        yb = _attn(q.reshape(bsz, seq, SB_WIDTH), k.reshape(bsz, seq, SB_WIDTH),
                   v.reshape(bsz, seq, SB_WIDTH), tq=ATTN_ROWS).reshape(t, SB_WIDTH)

        rw_t = router_w[i].T
        rw_hi = rw_t.astype(BF16)
        rw_lo = (rw_t - rw_hi.astype(F32)).astype(BF16)
        h1, h1p, eidx, wsel, pos, cnt = _post(
            h, vec(ln_in_g), vec(ln_in_b), ma, gb, yb, w_branch_b[i].astype(BF16),
            w_out[i].astype(BF16), vec(ln1_g[i]), vec(ln1_b[i]), rw_hi, rw_lo,
            router_bias[i].reshape(N_EXPERTS, 1).astype(F32), tm=POST_ROWS)

        n_blocks = t * TOP_K // ROW_BLOCK + N_EXPERTS
        pad_starts, block_e, n_used, zrow = _block_layout(cnt[:, 0], n_blocks)
        dest = _slots(pad_starts, eidx, pos).T.reshape(t * TOP_K)

        xs = _dispatch(zrow, dest, h1p, n_blocks * ROW_BLOCK, tm=DISPATCH_ROWS)
        ys = _experts(block_e, n_used, xs, moe_w_gate[i], moe_w_up[i], moe_w_down[i])
        h = _combine(dest, h1, wsel.T, p[i].reshape(t, PLE_DIM), ys,
                     sh_w_gate[i].astype(BF16), sh_w_up[i].astype(BF16), sh_w_down[i].astype(BF16),
                     ple_w_gate[i].astype(BF16), ple_w_proj[i].astype(BF16),
                     vec(ln2_g[i]), vec(ln2_b[i]), tm=COMBINE_ROWS)
    return h.reshape(bsz, seq, dm)
```
